```python
import jax
import jax.numpy as jnp
from jax import lax
import numpy as np

D_MODEL = 1024
BATCH = 32
SEQ = 2048
DEPTH = 2

GRID_W = 64
CTX_LEN = 256
N_MIXERS = 2
N_ATTN_LAYERS = (DEPTH + N_MIXERS - 1) // N_MIXERS
N_RET_LAYERS = DEPTH // N_MIXERS

HEAD_DIM = 64
N_HEADS = D_MODEL // HEAD_DIM
N_KV_HEADS = N_HEADS // 4
GQA_GROUP = N_HEADS // N_KV_HEADS
WINDOW = 128
ATTN_BLOCK = 128
ATTN_PROJ = (N_HEADS + 2 * N_KV_HEADS) * HEAD_DIM

RET_HEADS = D_MODEL // 256
RET_QK_DIM = D_MODEL // RET_HEADS
RET_V_DIM = 2 * D_MODEL // RET_HEADS
RET_VWIDTH = 2 * D_MODEL
RET_CHUNK = 128
RET_PROJ = 2 * D_MODEL + 2 * RET_VWIDTH

D_FF = -(-8 * D_MODEL // (3 * 256)) * 256

ROPE_BASE = 10000.0
EPS = 1e-6
NEG_INF = -1e30

kernel_name = 'hybrid_swa_sink_retention_dit'


def rms_norm(x, g):
    xf = x.astype(jnp.float32)
    y = xf * lax.rsqrt(jnp.mean(xf * xf, axis=-1, keepdims=True) + EPS)
    return (y * g.astype(jnp.float32)).astype(x.dtype)


def modulate(h, shift, scale):
    return h * (1 + scale) + shift


def grid_positions(n):
    rows = n // GRID_W
    row = jnp.broadcast_to(jnp.arange(rows, dtype=jnp.int32)[:, None], (rows, GRID_W)).reshape(n)
    col = jnp.broadcast_to(jnp.arange(GRID_W, dtype=jnp.int32)[None, :], (rows, GRID_W)).reshape(n)
    return row, col


def rope_tables(n, head_dim):
    row, col = grid_positions(n)
    axis_dim = head_dim // 2
    inv = ROPE_BASE ** (-jnp.arange(0, axis_dim, 2, dtype=jnp.float32) / axis_dim)
    ang_r = row.astype(jnp.float32)[:, None] * inv
    ang_c = col.astype(jnp.float32)[:, None] * inv
    return jnp.cos(ang_r), jnp.sin(ang_r), jnp.cos(ang_c), jnp.sin(ang_c)


def rotate_axis(x, cos, sin):
    x1, x2 = jnp.split(x, 2, axis=-1)
    cos = cos[:, None, :]
    sin = sin[:, None, :]
    return jnp.concatenate([x1 * cos - x2 * sin, x1 * sin + x2 * cos], axis=-1)


def rope_2d(x, tables):
    cos_r, sin_r, cos_c, sin_c = tables
    xr, xc = jnp.split(x.astype(jnp.float32), 2, axis=-1)
    out = jnp.concatenate([rotate_axis(xr, cos_r, sin_r), rotate_axis(xc, cos_c, sin_c)], axis=-1)
    return out.astype(x.dtype)


def swiglu(h, w_in, w_out):
    gate, up = jnp.split(h @ w_in, 2, axis=-1)
    return (jax.nn.silu(gate) * up) @ w_out


def windowed_gqa_sink(h_x, h_c, w_qkv, q_gain, k_gain, sink, w_o, need_ctx_out):
    B, S, _ = h_x.shape
    L = h_c.shape[1]
    nb = S // ATTN_BLOCK
    band = ATTN_BLOCK + 2 * WINDOW
    scale = HEAD_DIM ** -0.5
    qd = N_HEADS * HEAD_DIM
    kvd = N_KV_HEADS * HEAD_DIM
    tables = rope_tables(S, HEAD_DIM)
    sink_g = sink.astype(jnp.float32).reshape(N_KV_HEADS, GQA_GROUP)[None, :, :, None, None]

    q_x, k_x, v_x = jnp.split(h_x @ w_qkv, [qd, qd + kvd], axis=-1)
    q_x = rope_2d(rms_norm(q_x.reshape(B, S, N_HEADS, HEAD_DIM), q_gain), tables)
    q_x = q_x.reshape(B, S, N_KV_HEADS, GQA_GROUP, HEAD_DIM)
    k_x = rope_2d(rms_norm(k_x.reshape(B, S, N_KV_HEADS, HEAD_DIM), k_gain), tables)
    v_x = v_x.reshape(B, S, N_KV_HEADS, HEAD_DIM)
    k_c, v_c = jnp.split(h_c @ w_qkv[:, qd:], 2, axis=-1)
    k_c = rms_norm(k_c.reshape(B, L, N_KV_HEADS, HEAD_DIM), k_gain)
    v_c = v_c.reshape(B, L, N_KV_HEADS, HEAD_DIM)

    pad = ((0, 0), (WINDOW, WINDOW), (0, 0), (0, 0))
    k_pad = jnp.pad(k_x, pad)
    v_pad = jnp.pad(v_x, pad)
    r_idx = jnp.arange(ATTN_BLOCK, dtype=jnp.int32)[:, None]
    n_idx = jnp.arange(band, dtype=jnp.int32)[None, :]
    in_window = (n_idx >= r_idx) & (n_idx - r_idx <= 2 * WINDOW)

    def block(b):
        start = b * ATTN_BLOCK
        qb = lax.dynamic_slice_in_dim(q_x, start, ATTN_BLOCK, axis=1)
        kb = lax.dynamic_slice_in_dim(k_pad, start, band, axis=1)
        vb = lax.dynamic_slice_in_dim(v_pad, start, band, axis=1)
        key_pos = start - WINDOW + n_idx
        valid = in_window & (key_pos >= 0) & (key_pos < S)
        s_ctx = jnp.einsum('bqkgd,bnkd->bkgqn', qb, k_c, preferred_element_type=jnp.float32) * scale
        s_loc = jnp.einsum('bqkgd,bnkd->bkgqn', qb, kb, preferred_element_type=jnp.float32) * scale
        s_loc = jnp.where(valid, s_loc, NEG_INF)
        sink_col = jnp.broadcast_to(sink_g, s_ctx.shape[:-1] + (1,))
        p = jax.nn.softmax(jnp.concatenate([s_ctx, s_loc, sink_col], axis=-1), axis=-1).astype(vb.dtype)
        return (jnp.einsum('bkgqn,bnkd->bqkgd', p[..., :L], v_c)
                + jnp.einsum('bkgqn,bnkd->bqkgd', p[..., L:L + band], vb))

    o_x = jnp.moveaxis(lax.map(block, jnp.arange(nb, dtype=jnp.int32)), 0, 1).reshape(B, S, qd)
    out_x = o_x @ w_o

    out_c = None
    if need_ctx_out:
        q_c = rms_norm((h_c @ w_qkv[:, :qd]).reshape(B, L, N_HEADS, HEAD_DIM), q_gain)
        q_c = q_c.reshape(B, L, N_KV_HEADS, GQA_GROUP, HEAD_DIM)
        s_c = jnp.einsum('bqkgd,bnkd->bkgqn', q_c, k_c, preferred_element_type=jnp.float32) * scale
        sink_col = jnp.broadcast_to(sink_g, s_c.shape[:-1] + (1,))
        p_c = jax.nn.softmax(jnp.concatenate([s_c, sink_col], axis=-1), axis=-1).astype(v_c.dtype)
        o_c = jnp.einsum('bkgqn,bnkd->bqkgd', p_c[..., :L], v_c)
        out_c = o_c.reshape(B, L, qd) @ w_o
    return out_c, out_x


def retention_chunked(q, k, v, log_gamma, state0):
    B, T, H, _ = q.shape
    nc = T // RET_CHUNK
    pos = jnp.arange(RET_CHUNK, dtype=jnp.float32)
    diff = pos[:, None] - pos[None, :]
    intra = jnp.where(diff >= 0, jnp.exp(log_gamma[:, None, None] * jnp.maximum(diff, 0.0)), 0.0)
    q_decay = jnp.exp(log_gamma[None, :] * (pos + 1.0)[:, None])[None, :, :, None]
    k_decay = jnp.exp(log_gamma[None, :] * (RET_CHUNK - 1.0 - pos)[:, None])[None, :, :, None]
    chunk_decay = jnp.exp(log_gamma * RET_CHUNK)[None, :, None, None]

    def to_chunks(a):
        return jnp.moveaxis(a.reshape(B, nc, RET_CHUNK, H, a.shape[-1]), 1, 0)

    def step(state, inp):
        qc, kc, vc = inp
        scores = jnp.einsum('bnhd,bmhd->bhnm', qc, kc) * intra
        inner = jnp.einsum('bhnm,bmhe->bnhe', scores, vc)
        cross = jnp.einsum('bnhd,bhde->bnhe', qc, state) * q_decay
        new_state = state * chunk_decay + jnp.einsum('bmhd,bmhe->bhde', kc * k_decay, vc)
        return new_state, inner + cross

    state, out = lax.scan(step, state0, (to_chunks(q), to_chunks(k), to_chunks(v)))
    return jnp.moveaxis(out, 0, 1).reshape(B, T, H, v.shape[-1]), state


def retention_final_state(k, v, log_gamma):
    T = k.shape[1]
    pos = jnp.arange(T, dtype=jnp.float32)
    decay = jnp.exp(log_gamma[None, :] * (T - 1.0 - pos)[:, None])[None, :, :, None]
    return jnp.einsum('bthd,bthe->bhde', k * decay, v)


def bidir_retention(h_x, h_c, w_qkvg, decay_logit, gn_gain, w_o, need_ctx_out):
    B, S, _ = h_x.shape
    L = h_c.shape[1]
    qk = RET_HEADS * RET_QK_DIM
    f32 = jnp.float32
    tables = rope_tables(S, RET_QK_DIM)
    log_g = jax.nn.log_sigmoid(decay_logit.astype(f32))
    k_scale = RET_QK_DIM ** -0.5

    q_x, k_x, v_x, g_x = jnp.split(h_x @ w_qkvg, [qk, 2 * qk, 2 * qk + RET_VWIDTH], axis=-1)
    q_x = rope_2d(q_x.reshape(B, S, RET_HEADS, RET_QK_DIM), tables).astype(f32)
    k_x = rope_2d(k_x.reshape(B, S, RET_HEADS, RET_QK_DIM), tables).astype(f32) * k_scale
    v_x = v_x.reshape(B, S, RET_HEADS, RET_V_DIM).astype(f32)
    k_c, v_c = jnp.split(h_c @ w_qkvg[:, qk:2 * qk + RET_VWIDTH], [qk], axis=-1)
    k_c = k_c.reshape(B, L, RET_HEADS, RET_QK_DIM).astype(f32) * k_scale
    v_c = v_c.reshape(B, L, RET_HEADS, RET_V_DIM).astype(f32)

    def flip(a):
        return jnp.flip(a, axis=1)

    def gated_out(o, g):
        mu = jnp.mean(o, axis=-1, keepdims=True)
        var = jnp.mean(jnp.square(o - mu), axis=-1, keepdims=True)
        y = ((o - mu) * lax.rsqrt(var + EPS)).reshape(o.shape[0], o.shape[1], RET_VWIDTH) * gn_gain.astype(f32)
        return (jax.nn.silu(g) * y.astype(g.dtype)) @ w_o

    out_c = None
    if need_ctx_out:
        q_c = (h_c @ w_qkvg[:, :qk]).reshape(B, L, RET_HEADS, RET_QK_DIM).astype(f32)
        g_c = h_c @ w_qkvg[:, 2 * qk + RET_VWIDTH:]
        zero_state = jnp.zeros((B, RET_HEADS, RET_QK_DIM, RET_V_DIM), f32)
        oc_f, state_f = retention_chunked(q_c, k_c, v_c, log_g[0], zero_state)
        oc_b, state_b = retention_chunked(flip(q_c), flip(k_c), flip(v_c), log_g[1], zero_state)
        out_c = gated_out(oc_f + flip(oc_b), g_c)
    else:
        state_f = retention_final_state(k_c, v_c, log_g[0])
        state_b = retention_final_state(flip(k_c), flip(v_c), log_g[1])

    ox_f, _ = retention_chunked(q_x, k_x, v_x, log_g[0], state_f)
    ox_b, _ = retention_chunked(flip(q_x), flip(k_x), flip(v_x), log_g[1], state_b)
    out_x = gated_out(ox_f + flip(ox_b), g_x)
    return out_c, out_x


def setup_inputs(seed: int = 0) -> dict:
    key = jax.random.key(seed)
    ks = jax.random.split(key, 19)
    f32 = jnp.float32

    def nrm(k, shape, scale):
        return jax.random.normal(k, shape, f32) * scale

    gamma = 1.0 - 2.0 ** (-5.0 - np.arange(RET_HEADS))
    decay_init = jnp.asarray(np.log(gamma / (1.0 - gamma)), dtype=f32)
    return {
        'x': nrm(ks[0], (BATCH, SEQ, D_MODEL), 1.0),
        'c': nrm(ks[1], (BATCH, D_MODEL), 1.0),
        'ctx': nrm(ks[2], (BATCH, CTX_LEN, D_MODEL), 1.0),
        'c_ctx': nrm(ks[3], (D_MODEL,), 1.0),
        'ada_w': nrm(ks[4], (DEPTH, D_MODEL, 6 * D_MODEL), 0.5 * D_MODEL ** -0.5),
        'ada_b': nrm(ks[5], (DEPTH, 6 * D_MODEL), 0.02),
        'norm1_g': 1.0 + nrm(ks[6], (DEPTH, D_MODEL), 0.02),
        'norm2_g': 1.0 + nrm(ks[7], (DEPTH, D_MODEL), 0.02),
        'ffn_w_in': nrm(ks[8], (DEPTH, D_MODEL, 2 * D_FF), D_MODEL ** -0.5),
        'ffn_w_out': nrm(ks[9], (DEPTH, D_FF, D_MODEL), D_FF ** -0.5),
        'attn_w_qkv': nrm(ks[10], (N_ATTN_LAYERS, D_MODEL, ATTN_PROJ), D_MODEL ** -0.5),
        'attn_q_norm': 1.0 + nrm(ks[11], (N_ATTN_LAYERS, HEAD_DIM), 0.02),
        'attn_k_norm': 1.0 + nrm(ks[12], (N_ATTN_LAYERS, HEAD_DIM), 0.02),
        'attn_sink': nrm(ks[13], (N_ATTN_LAYERS, N_HEADS), 0.5),
        'attn_w_o': nrm(ks[14], (N_ATTN_LAYERS, N_HEADS * HEAD_DIM, D_MODEL), (N_HEADS * HEAD_DIM) ** -0.5),
        'ret_w_qkvg': nrm(ks[15], (N_RET_LAYERS, D_MODEL, RET_PROJ), D_MODEL ** -0.5),
        'ret_decay_logit': decay_init[None, None, :] + nrm(ks[16], (N_RET_LAYERS, 2, RET_HEADS), 0.01),
        'ret_gn_g': 1.0 + nrm(ks[17], (N_RET_LAYERS, RET_VWIDTH), 0.02),
        'ret_w_o': nrm(ks[18], (N_RET_LAYERS, RET_VWIDTH, D_MODEL), RET_VWIDTH ** -0.5),
    }


def reference(x, c, ctx, c_ctx, ada_w, ada_b, norm1_g, norm2_g, ffn_w_in, ffn_w_out,
              attn_w_qkv, attn_q_norm, attn_k_norm, attn_sink, attn_w_o,
              ret_w_qkvg, ret_decay_logit, ret_gn_g, ret_w_o):
    c_act = jax.nn.silu(c)[:, None, :]
    cc_act = jax.nn.silu(c_ctx)[None, None, :]
    y_ctx = ctx
    for i in range(DEPTH):
        need_ctx_out = i < DEPTH - 1
        mx = jnp.split(c_act @ ada_w[i] + ada_b[i], 6, axis=-1)
        mc = jnp.split(cc_act @ ada_w[i] + ada_b[i], 6, axis=-1)
        h_x = modulate(rms_norm(x, norm1_g[i]), mx[0], mx[1])
        h_c = modulate(rms_norm(y_ctx, norm1_g[i]), mc[0], mc[1])
        j = i // N_MIXERS
        if i % N_MIXERS == 0:
            out_c, out_x = windowed_gqa_sink(h_x, h_c, attn_w_qkv[j], attn_q_norm[j], attn_k_norm[j],
                                             attn_sink[j], attn_w_o[j], need_ctx_out)
        else:
            out_c, out_x = bidir_retention(h_x, h_c, ret_w_qkvg[j], ret_decay_logit[j], ret_gn_g[j],
                                           ret_w_o[j], need_ctx_out)
        x = x + mx[2] * out_x
        x = x + mx[5] * swiglu(modulate(rms_norm(x, norm2_g[i]), mx[3], mx[4]), ffn_w_in[i], ffn_w_out[i])
        if need_ctx_out:
            y_ctx = y_ctx + mc[2] * out_c
            y_ctx = y_ctx + mc[5] * swiglu(modulate(rms_norm(y_ctx, norm2_g[i]), mc[3], mc[4]),
                                           ffn_w_in[i], ffn_w_out[i])
    return x
```

```python
import functools

import jax
import jax.numpy as jnp
from jax import lax
from jax.experimental import pallas as pl
from jax.experimental.pallas import tpu as pltpu

F32 = jnp.float32
BF16 = jnp.bfloat16

D_MODEL = 1024
GRID_W = 64
HEAD_DIM = 64
N_HEADS = D_MODEL // HEAD_DIM
N_KV_HEADS = N_HEADS // 4
WINDOW = 128
ATTN_BLOCK = 128
RET_HEADS = 4
RET_QK_DIM = 256
RET_V_DIM = 512
RET_VWIDTH = 2 * D_MODEL
RET_CHUNK = 128
D_FF = 2816
FF_CHUNK = 1408
ROPE_BASE = 10000.0
EPS = 1e-6
NEG_INF = -1e30
LANES = 128

VMEM_LIMIT = 56 * 1024 * 1024


def _dot(a, b):
    return jnp.dot(a, b, preferred_element_type=F32)


def _dot_nt(a, b):
    return lax.dot_general(a, b, (((1,), (1,)), ((), ())), preferred_element_type=F32)


def _dot_tn(a, b):
    return lax.dot_general(a, b, (((0,), (0,)), ((), ())), preferred_element_type=F32)


def _rms(x, g):
    return x * lax.rsqrt(jnp.mean(x * x, axis=-1, keepdims=True) + EPS) * g


def _silu(x):
    return x * jax.nn.sigmoid(x)


def _params(n_axes):
    return pltpu.CompilerParams(dimension_semantics=("arbitrary",) * n_axes, vmem_limit_bytes=VMEM_LIMIT)


def _const2(shape):
    return pl.BlockSpec(shape, lambda b, j: (0,) * len(shape), pipeline_mode=pl.Buffered(1))


def _mod_spec(k, per_batch):
    if per_batch:
        return pl.BlockSpec((None, 1, D_MODEL), lambda b, j, k=k: (b, 0, k))
    return pl.BlockSpec((None, 1, D_MODEL), lambda b, j, k=k: (0, 0, k))


def _tok_spec(tm, width):
    return pl.BlockSpec((None, tm, width), lambda b, j: (b, j, 0))


def _mods_kernel(c_ref, w_ref, b_ref, o_ref):
    a = _silu(c_ref[...]).astype(BF16)
    o_ref[...] = _dot(a, w_ref[...].astype(BF16)) + b_ref[...]


def _mods(cin, ada_w, ada_b):
    depth, d, n = ada_w.shape
    rows = cin.shape[0]
    tn = 1536
    return pl.pallas_call(
        _mods_kernel,
        grid=(depth, n // tn),
        in_specs=[
            pl.BlockSpec((rows, d), lambda i, j: (0, 0)),
            pl.BlockSpec((None, d, tn), lambda i, j: (i, 0, j)),
            pl.BlockSpec((None, 1, tn), lambda i, j: (i, 0, j)),
        ],
        out_specs=pl.BlockSpec((None, rows, tn), lambda i, j: (i, 0, j)),
        out_shape=jax.ShapeDtypeStruct((depth, rows, n), F32),
        compiler_params=_params(2),
        name="adaln_mods",
    )(cin, ada_w, ada_b.reshape(depth, 1, n))


def _dup_halves(r):
    lane = lax.broadcasted_iota(jnp.int32, r.shape, 1)
    lo = lane < HEAD_DIM
    r64 = pltpu.roll(r, HEAD_DIM, 1)
    return jnp.where(lo, r, r64), jnp.where(lo, r64, r)


def _qkv0_kernel(x_ref, sh_ref, sc_ref, g1_ref, w_ref, gain_ref, gsum_ref, gbc_ref, cos_ref, sin_ref,
                 q_ref, kk_ref, vv_ref):
    qd = N_HEADS * HEAD_DIM
    qkd = qd + N_KV_HEADS * HEAD_DIM
    h = _rms(x_ref[...], g1_ref[...]) * (1.0 + sc_ref[...]) + sh_ref[...]
    qkv = _dot(h.astype(BF16), w_ref[...])
    qk = qkv[:, :qkd]
    ms = _dot((qk * qk).astype(BF16), gsum_ref[...])
    inv = lax.rsqrt(ms + EPS)
    hi = inv.astype(BF16)
    lo = (inv - hi.astype(F32)).astype(BF16)
    invf = _dot(jnp.concatenate([hi, lo], axis=1), gbc_ref[...])
    t = qk * gain_ref[...]
    cos = cos_ref[...]
    sin = sin_ref[...]
    lane = lax.broadcasted_iota(jnp.int32, cos.shape, 1)
    first = (lane % 32) < 16
    for cb in range(qkd // LANES):
        sl = slice(cb * LANES, (cb + 1) * LANES)
        tb = t[:, sl]
        sw = jnp.where(first, pltpu.roll(tb, LANES - 16, 1), pltpu.roll(tb, 16, 1))
        r = (tb * cos + sw * sin) * invf[:, sl]
        if cb < qd // LANES:
            q_ref[:, sl] = r.astype(BF16)
        else:
            p = cb - qd // LANES
            a, b = _dup_halves(r)
            kk_ref[:, (2 * p) * LANES:(2 * p + 1) * LANES] = a.astype(BF16)
            kk_ref[:, (2 * p + 1) * LANES:(2 * p + 2) * LANES] = b.astype(BF16)
    for p in range(N_KV_HEADS * HEAD_DIM // LANES):
        a, b = _dup_halves(qkv[:, qkd + p * LANES:qkd + (p + 1) * LANES])
        vv_ref[:, (2 * p) * LANES:(2 * p + 1) * LANES] = a.astype(BF16)
        vv_ref[:, (2 * p + 1) * LANES:(2 * p + 2) * LANES] = b.astype(BF16)


def _qkv0(xin, mods, per_batch, g1, w, gain, gsum, gbc, cos, sin, tm):
    b, t, d = xin.shape
    kvw = 2 * N_KV_HEADS * HEAD_DIM
    return pl.pallas_call(
        _qkv0_kernel,
        grid=(b, t // tm),
        in_specs=[
            _tok_spec(tm, d),
            _mod_spec(0, per_batch),
            _mod_spec(1, per_batch),
            _const2((1, d)),
            _const2(w.shape),
            _const2(gain.shape),
            _const2(gsum.shape),
            _const2(gbc.shape),
            pl.BlockSpec((tm, LANES), lambda b, j: (j, 0)),
            pl.BlockSpec((tm, LANES), lambda b, j: (j, 0)),
        ],
        out_specs=[_tok_spec(tm, d), _tok_spec(tm, kvw), _tok_spec(tm, kvw)],
        out_shape=[
            jax.ShapeDtypeStruct((b, t, d), BF16),
            jax.ShapeDtypeStruct((b, t, kvw), BF16),
            jax.ShapeDtypeStruct((b, t, kvw), BF16),
        ],
        compiler_params=_params(2),
        name="attn_qkv_proj",
    )(xin, mods, mods, g1, w, gain, gsum, gbc, cos, sin)


def _attn_kernel(sink_ref, q_ref, kc_ref, vc_ref, *rest, has_local, nb):
    if has_local:
        ka_ref, kb_ref, kcn_ref, va_ref, vb_ref, vcn_ref, o_ref = rest
        j = pl.program_id(1)
        row = lax.broadcasted_iota(jnp.int32, (ATTN_BLOCK, ATTN_BLOCK), 0)
        col = lax.broadcasted_iota(jnp.int32, (ATTN_BLOCK, ATTN_BLOCK), 1)
        mask_a = (col >= row) & (j > 0)
        mask_c = (col <= row) & (j < nb - 1)
    else:
        (o_ref,) = rest
    tq = q_ref.shape[0]
    lo = lax.broadcasted_iota(jnp.int32, (tq, LANES), 1) < HEAD_DIM
    zero = jnp.zeros((tq, LANES), BF16)
    for p in range(N_HEADS // 2):
        g = p // 2
        gs = slice(g * LANES, (g + 1) * LANES)
        qb = q_ref[:, p * LANES:(p + 1) * LANES]
        outs = []
        for half in range(2):
            sk = sink_ref[2 * p + half]
            qm = jnp.where(lo, qb, zero) if half == 0 else jnp.where(lo, zero, qb)
            blocks = [(_dot_nt(qm, kc_ref[:, gs]), vc_ref[:, gs])]
            if has_local:
                blocks.append((jnp.where(mask_a, _dot_nt(qm, ka_ref[:, gs]), NEG_INF), va_ref[:, gs]))
                blocks.append((_dot_nt(qm, kb_ref[:, gs]), vb_ref[:, gs]))
                blocks.append((jnp.where(mask_c, _dot_nt(qm, kcn_ref[:, gs]), NEG_INF), vcn_ref[:, gs]))
            m = jnp.full((tq, 1), sk, F32)
            for s, _ in blocks:
                m = jnp.maximum(m, jnp.max(s, axis=1, keepdims=True))
            den = jnp.exp(sk - m)
            acc = jnp.zeros((tq, LANES), F32)
            for s, v in blocks:
                e = jnp.exp(s - m)
                den = den + jnp.sum(e, axis=1, keepdims=True)
                acc = acc + _dot(e.astype(BF16), v)
            outs.append(acc / den)
        o_ref[:, p * LANES:(p + 1) * LANES] = jnp.where(lo, outs[0], outs[1]).astype(BF16)


def _attn_x(sink, q, kk, vv, kkc, vvc):
    b, s, d = q.shape
    l = kkc.shape[1]
    kvw = kk.shape[2]
    nb = s // ATTN_BLOCK
    blk = lambda f: pl.BlockSpec((None, ATTN_BLOCK, kvw), f)
    prev = lambda b, j: (b, jnp.maximum(j - 1, 0), 0)
    cur = lambda b, j: (b, j, 0)
    nxt = lambda b, j: (b, jnp.minimum(j + 1, nb - 1), 0)
    ctx = pl.BlockSpec((None, l, kvw), lambda b, j: (b, 0, 0))
    return pl.pallas_call(
        functools.partial(_attn_kernel, has_local=True, nb=nb),
        grid=(b, nb),
        in_specs=[
            pl.BlockSpec(memory_space=pltpu.SMEM),
            pl.BlockSpec((None, ATTN_BLOCK, d), cur),
            ctx, ctx,
            blk(prev), blk(cur), blk(nxt),
            blk(prev), blk(cur), blk(nxt),
        ],
        out_specs=pl.BlockSpec((None, ATTN_BLOCK, d), cur),
        out_shape=jax.ShapeDtypeStruct((b, s, d), BF16),
        compiler_params=_params(2),
        name="attn_latent",
    )(sink, q, kkc, vvc, kk, kk, kk, vv, vv, vv)


def _attn_c(sink, qc, kkc, vvc):
    b, l, d = qc.shape
    kvw = kkc.shape[2]
    return pl.pallas_call(
        functools.partial(_attn_kernel, has_local=False, nb=1),
        grid=(b,),
        in_specs=[
            pl.BlockSpec(memory_space=pltpu.SMEM),
            pl.BlockSpec((None, l, d), lambda b: (b, 0, 0)),
            pl.BlockSpec((None, l, kvw), lambda b: (b, 0, 0)),
            pl.BlockSpec((None, l, kvw), lambda b: (b, 0, 0)),
        ],
        out_specs=pl.BlockSpec((None, l, d), lambda b: (b, 0, 0)),
        out_shape=jax.ShapeDtypeStruct((b, l, d), BF16),
        compiler_params=_params(1),
        name="attn_context",
    )(sink, qc, kkc, vvc)


def _ffn_tail(x1, shf_ref, scf_ref, gtf_ref, g2_ref, win_ref, wout_ref, out_ref):
    h2 = (_rms(x1, g2_ref[...]) * (1.0 + scf_ref[...]) + shf_ref[...]).astype(BF16)
    acc = jnp.zeros(x1.shape, F32)
    for c in range(D_FF // FF_CHUNK):
        gate = _dot(h2, win_ref[:, c * FF_CHUNK:(c + 1) * FF_CHUNK])
        up = _dot(h2, win_ref[:, D_FF + c * FF_CHUNK:D_FF + (c + 1) * FF_CHUNK])
        act = (_silu(gate) * up).astype(BF16)
        acc = acc + _dot(act, wout_ref[c * FF_CHUNK:(c + 1) * FF_CHUNK, :])
    out_ref[...] = x1 + gtf_ref[...] * acc


def _post0_kernel(x_ref, o_ref, gtm_ref, shf_ref, scf_ref, gtf_ref, g2_ref, wo_ref, win_ref, wout_ref, out_ref):
    x1 = x_ref[...] + gtm_ref[...] * _dot(o_ref[...], wo_ref[...])
    _ffn_tail(x1, shf_ref, scf_ref, gtf_ref, g2_ref, win_ref, wout_ref, out_ref)


def _post0(xin, o, mods, per_batch, g2, wo, win, wout, tm):
    b, t, d = xin.shape
    return pl.pallas_call(
        _post0_kernel,
        grid=(b, t // tm),
        in_specs=[
            _tok_spec(tm, d), _tok_spec(tm, d),
            _mod_spec(2, per_batch), _mod_spec(3, per_batch), _mod_spec(4, per_batch), _mod_spec(5, per_batch),
            _const2((1, d)), _const2(wo.shape), _const2(win.shape), _const2(wout.shape),
        ],
        out_specs=_tok_spec(tm, d),
        out_shape=jax.ShapeDtypeStruct((b, t, d), F32),
        compiler_params=_params(2),
        name="attn_out_ffn",
    )(xin, o, mods, mods, mods, mods, g2, wo, win, wout)


def _post1_kernel(x_ref, y_ref, shm_ref, scm_ref, gtm_ref, shf_ref, scf_ref, gtf_ref, g1_ref, g2_ref,
                  wg_ref, wo_ref, win_ref, wout_ref, out_ref):
    x = x_ref[...]
    h1 = (_rms(x, g1_ref[...]) * (1.0 + scm_ref[...]) + shm_ref[...]).astype(BF16)
    gated = (_silu(_dot(h1, wg_ref[...])) * y_ref[...].astype(F32)).astype(BF16)
    x1 = x + gtm_ref[...] * _dot(gated, wo_ref[...])
    _ffn_tail(x1, shf_ref, scf_ref, gtf_ref, g2_ref, win_ref, wout_ref, out_ref)


def _post1(xin, y, mods, g1, g2, wg, wo, win, wout, tm):
    b, t, d = xin.shape
    return pl.pallas_call(
        _post1_kernel,
        grid=(b, t // tm),
        in_specs=[
            _tok_spec(tm, d), _tok_spec(tm, y.shape[2]),
            *[_mod_spec(k, True) for k in range(6)],
            _const2((1, d)), _const2((1, d)),
            _const2(wg.shape), _const2(wo.shape), _const2(win.shape), _const2(wout.shape),
        ],
        out_specs=_tok_spec(tm, d),
        out_shape=jax.ShapeDtypeStruct((b, t, d), F32),
        compiler_params=_params(2),
        name="ret_out_ffn",
    )(xin, y, mods, mods, mods, mods, mods, mods, g1, g2, wg, wo, win, wout)


def _proj1_kernel(x_ref, sh_ref, sc_ref, g1_ref, w_ref, cos_ref, sin_ref, q_ref, k_ref, v_ref):
    qk_w = 2 * RET_HEADS * RET_QK_DIM
    hb = (_rms(x_ref[...], g1_ref[...]) * (1.0 + sc_ref[...]) + sh_ref[...]).astype(BF16)
    qk = _dot(hb, w_ref[:, :qk_w])
    k_scale = RET_QK_DIM ** -0.5
    for cb in range(qk_w // LANES):
        ts = slice((cb % 2) * LANES, (cb % 2 + 1) * LANES)
        tb = qk[:, cb * LANES:(cb + 1) * LANES]
        r = tb * cos_ref[:, ts] + pltpu.roll(tb, LANES // 2, 1) * sin_ref[:, ts]
        if cb < qk_w // (2 * LANES):
            q_ref[:, cb * LANES:(cb + 1) * LANES] = r.astype(BF16)
        else:
            cc = cb - qk_w // (2 * LANES)
            k_ref[:, cc * LANES:(cc + 1) * LANES] = (r * k_scale).astype(BF16)
    v_ref[...] = _dot(hb, w_ref[:, qk_w:]).astype(BF16)


def _proj1(xin, mods, per_batch, g1, w, cos, sin, tm):
    b, t, d = xin.shape
    qw = RET_HEADS * RET_QK_DIM
    return pl.pallas_call(
        _proj1_kernel,
        grid=(b, t // tm),
        in_specs=[
            _tok_spec(tm, d),
            _mod_spec(0, per_batch), _mod_spec(1, per_batch),
            _const2((1, d)), _const2(w.shape),
            pl.BlockSpec((tm, 2 * LANES), lambda b, j: (j, 0)),
            pl.BlockSpec((tm, 2 * LANES), lambda b, j: (j, 0)),
        ],
        out_specs=[_tok_spec(tm, qw), _tok_spec(tm, qw), _tok_spec(tm, RET_VWIDTH)],
        out_shape=[
            jax.ShapeDtypeStruct((b, t, qw), BF16),
            jax.ShapeDtypeStruct((b, t, qw), BF16),
            jax.ShapeDtypeStruct((b, t, RET_VWIDTH), BF16),
        ],
        compiler_params=_params(2),
        name="ret_qkv_proj",
    )(xin, mods, mods, g1, w, cos, sin)


def _ret_kernel(dl_ref, q_ref, k_ref, v_ref, kc_ref, vc_ref, gn_ref, y_ref, o_scr, sf_scr, sb_scr, *, nc):
    hd = pl.program_id(1)
    c_len = RET_CHUNK
    l_ctx = kc_ref.shape[0]
    shape = (c_len, c_len)

    def log_decay(direction, shp):
        return -jnp.log(1.0 + jnp.exp(-jnp.full(shp, dl_ref[direction, hd], F32)))

    n = lax.broadcasted_iota(jnp.int32, shape, 0).astype(F32)
    m = lax.broadcasted_iota(jnp.int32, shape, 1).astype(F32)
    comb = (jnp.where(n >= m, jnp.exp(log_decay(0, shape) * jnp.maximum(n - m, 0.0)), 0.0)
            + jnp.where(m >= n, jnp.exp(log_decay(1, shape) * jnp.maximum(m - n, 0.0)), 0.0))
    nk = lax.broadcasted_iota(jnp.int32, (c_len, RET_QK_DIM), 0).astype(F32)
    nv = lax.broadcasted_iota(jnp.int32, (c_len, RET_V_DIM), 0).astype(F32)
    qdf = jnp.exp(log_decay(0, nv.shape) * (nv + 1.0))
    kdf = jnp.exp(log_decay(0, nk.shape) * (c_len - 1.0 - nk))
    cdf = jnp.exp(log_decay(0, (1, RET_V_DIM)) * float(c_len))
    qdb = jnp.exp(log_decay(1, nv.shape) * (c_len - nv))
    kdb = jnp.exp(log_decay(1, nk.shape) * nk)
    cdb = jnp.exp(log_decay(1, (1, RET_V_DIM)) * float(c_len))

    tt = lax.broadcasted_iota(jnp.int32, (l_ctx, RET_QK_DIM), 0).astype(F32)
    kc = kc_ref[...].astype(F32)
    vc = vc_ref[...]
    sf_scr[...] = _dot_tn((kc * jnp.exp(log_decay(0, tt.shape) * (l_ctx - 1.0 - tt))).astype(BF16), vc)
    sb_scr[...] = _dot_tn((kc * jnp.exp(log_decay(1, tt.shape) * tt)).astype(BF16), vc)

    def fwd(c, carry):
        sl = pl.ds(pl.multiple_of(c * c_len, c_len), c_len)
        qc = q_ref[sl, :]
        kx = k_ref[sl, :]
        vx = v_ref[sl, :]
        st = sf_scr[...]
        inner = _dot((_dot_nt(qc, kx) * comb).astype(BF16), vx)
        cross = _dot(qc, st.astype(BF16)) * qdf
        o_scr[sl, :] = inner + cross
        sf_scr[...] = st * cdf + _dot_tn((kx.astype(F32) * kdf).astype(BF16), vx)
        return carry

    lax.fori_loop(0, nc, fwd, 0)

    def bwd(i, carry):
        c = nc - 1 - i
        sl = pl.ds(pl.multiple_of(c * c_len, c_len), c_len)
        qc = q_ref[sl, :]
        kx = k_ref[sl, :]
        vx = v_ref[sl, :]
        st = sb_scr[...]
        o_scr[sl, :] = o_scr[sl, :] + _dot(qc, st.astype(BF16)) * qdb
        sb_scr[...] = st * cdb + _dot_tn((kx.astype(F32) * kdb).astype(BF16), vx)
        return carry

    lax.fori_loop(0, nc, bwd, 0)

    gn = gn_ref[...]

    def norm(c, carry):
        sl = pl.ds(pl.multiple_of(c * c_len, c_len), c_len)
        o = o_scr[sl, :]
        d = o - jnp.mean(o, axis=-1, keepdims=True)
        var = jnp.mean(d * d, axis=-1, keepdims=True)
        y_ref[sl, :] = (d * lax.rsqrt(var + EPS) * gn).astype(BF16)
        return carry

    lax.fori_loop(0, nc, norm, 0)


def _retention(dl, q, k, v, kc, vc, gn):
    b, s, _ = q.shape
    l = kc.shape[1]
    nc = s // RET_CHUNK
    hspec = lambda t, w: pl.BlockSpec((None, t, w), lambda b, h: (b, 0, h))
    return pl.pallas_call(
        functools.partial(_ret_kernel, nc=nc),
        grid=(b, RET_HEADS),
        in_specs=[
            pl.BlockSpec(memory_space=pltpu.SMEM),
            hspec(s, RET_QK_DIM), hspec(s, RET_QK_DIM), hspec(s, RET_V_DIM),
            hspec(l, RET_QK_DIM), hspec(l, RET_V_DIM),
            pl.BlockSpec((1, RET_V_DIM), lambda b, h: (0, h)),
        ],
        out_specs=hspec(s, RET_V_DIM),
        out_shape=jax.ShapeDtypeStruct((b, s, RET_VWIDTH), BF16),
        scratch_shapes=[
            pltpu.VMEM((s, RET_V_DIM), F32),
            pltpu.VMEM((RET_QK_DIM, RET_V_DIM), F32),
            pltpu.VMEM((RET_QK_DIM, RET_V_DIM), F32),
        ],
        compiler_params=_params(2),
        name="retention",
    )(dl, q, k, v, kc, vc, gn)


def _rope_tables(n, head_dim):
    rows = n // GRID_W
    row = jnp.broadcast_to(jnp.arange(rows, dtype=jnp.int32)[:, None], (rows, GRID_W)).reshape(n)
    col = jnp.broadcast_to(jnp.arange(GRID_W, dtype=jnp.int32)[None, :], (rows, GRID_W)).reshape(n)
    axis_dim = head_dim // 2
    inv = ROPE_BASE ** (-jnp.arange(0, axis_dim, 2, dtype=F32) / axis_dim)
    ang_r = row.astype(F32)[:, None] * inv
    ang_c = col.astype(F32)[:, None] * inv
    cos = jnp.concatenate([jnp.cos(ang_r)] * 2 + [jnp.cos(ang_c)] * 2, axis=-1)
    sin = jnp.concatenate([-jnp.sin(ang_r), jnp.sin(ang_r), -jnp.sin(ang_c), jnp.sin(ang_c)], axis=-1)
    return cos, sin


def kernel(x, c, ctx, c_ctx, ada_w, ada_b, norm1_g, norm2_g, ffn_w_in, ffn_w_out, attn_w_qkv, attn_q_norm,
           attn_k_norm, attn_sink, attn_w_o, ret_w_qkvg, ret_decay_logit, ret_gn_g, ret_w_o):
    b, s, d = x.shape
    l = ctx.shape[1]
    qd = N_HEADS * HEAD_DIM
    kvd = N_KV_HEADS * HEAD_DIM
    tm_x = 256
    tm_c = min(256, l)

    rows = -(-(b + 1) // 8) * 8
    cin = jnp.concatenate([c, c_ctx[None, :], jnp.zeros((rows - b - 1, d), F32)], axis=0)
    mods = _mods(cin, ada_w, ada_b)
    mx0 = mods[0, :b].reshape(b, 1, 6 * d)
    mc0 = mods[0, b:b + 1].reshape(1, 1, 6 * d)
    mx1 = mods[1, :b].reshape(b, 1, 6 * d)
    mc1 = mods[1, b:b + 1].reshape(1, 1, 6 * d)

    w_qkv = attn_w_qkv[0].astype(BF16)
    scale = HEAD_DIM ** -0.5
    gain = jnp.concatenate([jnp.tile(attn_q_norm[0] * scale, N_HEADS), jnp.tile(attn_k_norm[0], N_KV_HEADS)])[None, :]
    n_norm_heads = N_HEADS + N_KV_HEADS
    head_of = jnp.arange(qd + kvd) // HEAD_DIM
    gsum = ((head_of[:, None] == jnp.arange(LANES)[None, :]).astype(F32) / HEAD_DIM).astype(BF16)
    gbc_half = (jnp.arange(LANES)[:, None] == head_of[None, :]).astype(BF16)
    gbc = jnp.concatenate([gbc_half, gbc_half], axis=0)
    del n_norm_heads
    cos64, sin64 = _rope_tables(s, HEAD_DIM)
    cos_x = jnp.tile(cos64, (1, LANES // HEAD_DIM))
    sin_x = jnp.tile(sin64, (1, LANES // HEAD_DIM))
    cos_c = jnp.ones((l, LANES), F32)
    sin_c = jnp.zeros((l, LANES), F32)
    g1 = norm1_g[0][None, :]
    g2 = norm2_g[0][None, :]
    q_x, kk_x, vv_x = _qkv0(x, mx0, True, g1, w_qkv, gain, gsum, gbc, cos_x, sin_x, tm_x)
    q_c, kk_c, vv_c = _qkv0(ctx, mc0, False, g1, w_qkv, gain, gsum, gbc, cos_c, sin_c, tm_c)
    sink = attn_sink[0].astype(F32)
    o_x = _attn_x(sink, q_x, kk_x, vv_x, kk_c, vv_c)
    o_c = _attn_c(sink, q_c, kk_c, vv_c)
    w_o = attn_w_o[0].astype(BF16)
    w_in0 = ffn_w_in[0].astype(BF16)
    w_out0 = ffn_w_out[0].astype(BF16)
    x1 = _post0(x, o_x, mx0, True, g2, w_o, w_in0, w_out0, tm_x)
    y_ctx = _post0(ctx, o_c, mc0, False, g2, w_o, w_in0, w_out0, tm_c)

    qk_w = 2 * RET_HEADS * RET_QK_DIM
    w_all = ret_w_qkvg[0].astype(BF16)
    w_qkv1 = w_all[:, :qk_w + RET_VWIDTH]
    w_g = w_all[:, qk_w + RET_VWIDTH:]
    cos256, sin256 = _rope_tables(s, RET_QK_DIM)
    g1 = norm1_g[1][None, :]
    g2 = norm2_g[1][None, :]
    q1, k1, v1 = _proj1(x1, mx1, True, g1, w_qkv1, cos256, sin256, tm_x)
    ones_c = jnp.ones((l, 2 * LANES), F32)
    _, k1c, v1c = _proj1(y_ctx, mc1, False, g1, w_qkv1, ones_c, jnp.zeros_like(ones_c), tm_c)
    y = _retention(ret_decay_logit[0].astype(F32), q1, k1, v1, k1c, v1c, ret_gn_g[0][None, :])
    return _post1(x1, y, mx1, g1, g2, w_g, ret_w_o[0].astype(BF16), ffn_w_in[1].astype(BF16),
                  ffn_w_out[1].astype(BF16), tm_x)
```

```python
import functools

import jax
import jax.numpy as jnp
from jax import lax
from jax.experimental import pallas as pl
from jax.experimental.pallas import tpu as pltpu

F32 = jnp.float32
BF16 = jnp.bfloat16

D_MODEL = 1024
GRID_W = 64
HEAD_DIM = 64
N_HEADS = D_MODEL // HEAD_DIM
N_KV_HEADS = N_HEADS // 4
WINDOW = 128
ATTN_BLOCK = 128
RET_HEADS = 4
RET_QK_DIM = 256
RET_V_DIM = 512
RET_VWIDTH = 2 * D_MODEL
RET_CHUNK = 128
RET_UNROLL = 4
D_FF = 2816
MXU_TILE = 256
FF_CHUNKS = ((0, 6 * MXU_TILE), (6 * MXU_TILE, D_FF))
ROPE_BASE = 10000.0
EPS = 1e-6
NEG_INF = -1e30
LANES = 128

VMEM_LIMIT = 56 * 1024 * 1024


def _dot(a, b):
    return jnp.dot(a, b, preferred_element_type=F32)


def _dot_nt(a, b):
    return lax.dot_general(a, b, (((1,), (1,)), ((), ())), preferred_element_type=F32)


def _dot_tn(a, b):
    return lax.dot_general(a, b, (((0,), (0,)), ((), ())), preferred_element_type=F32)


def _rms(x, g):
    return x * lax.rsqrt(jnp.mean(x * x, axis=-1, keepdims=True) + EPS) * g


def _silu(x):
    return x * jax.nn.sigmoid(x)


def _params(n_axes):
    return pltpu.CompilerParams(dimension_semantics=("arbitrary",) * n_axes, vmem_limit_bytes=VMEM_LIMIT)


def _const2(shape):
    return pl.BlockSpec(shape, lambda b, j: (0,) * len(shape), pipeline_mode=pl.Buffered(1))


def _mod_spec(k, per_batch):
    if per_batch:
        return pl.BlockSpec((None, 1, D_MODEL), lambda b, j, k=k: (b, 0, k))
    return pl.BlockSpec((None, 1, D_MODEL), lambda b, j, k=k: (0, 0, k))


def _tok_spec(tm, width):
    return pl.BlockSpec((None, tm, width), lambda b, j: (b, j, 0))


def _mods_kernel(c_ref, w_ref, b_ref, o_ref):
    a = _silu(c_ref[...]).astype(BF16)
    o_ref[...] = _dot(a, w_ref[...].astype(BF16)) + b_ref[...]


def _mods(cin, ada_w, ada_b):
    depth, d, n = ada_w.shape
    rows = cin.shape[0]
    tn = 1536
    return pl.pallas_call(
        _mods_kernel,
        grid=(depth, n // tn),
        in_specs=[
            pl.BlockSpec((rows, d), lambda i, j: (0, 0)),
            pl.BlockSpec((None, d, tn), lambda i, j: (i, 0, j)),
            pl.BlockSpec((None, 1, tn), lambda i, j: (i, 0, j)),
        ],
        out_specs=pl.BlockSpec((None, rows, tn), lambda i, j: (i, 0, j)),
        out_shape=jax.ShapeDtypeStruct((depth, rows, n), F32),
        compiler_params=_params(2),
        name="adaln_mods",
    )(cin, ada_w, ada_b.reshape(depth, 1, n))


def _dup_halves(r):
    lane = lax.broadcasted_iota(jnp.int32, r.shape, 1)
    lo = lane < HEAD_DIM
    r64 = pltpu.roll(r, HEAD_DIM, 1)
    return jnp.where(lo, r, r64), jnp.where(lo, r64, r)


def _qkv0_kernel(x_ref, sh_ref, sc_ref, g1_ref, w_ref, gain_ref, gsum_ref, gbc_ref, cos_ref, sin_ref,
                 q_ref, kk_ref, vv_ref):
    qd = N_HEADS * HEAD_DIM
    qkd = qd + N_KV_HEADS * HEAD_DIM
    h = _rms(x_ref[...], g1_ref[...]) * (1.0 + sc_ref[...]) + sh_ref[...]
    qkv = _dot(h.astype(BF16), w_ref[...])
    qk = qkv[:, :qkd]
    ms = _dot((qk * qk).astype(BF16), gsum_ref[...])
    inv = lax.rsqrt(ms + EPS)
    hi = inv.astype(BF16)
    lo = (inv - hi.astype(F32)).astype(BF16)
    invf = _dot(jnp.concatenate([hi, lo], axis=1), gbc_ref[...])
    t = qk * gain_ref[...]
    cos = cos_ref[...]
    sin = sin_ref[...]
    lane = lax.broadcasted_iota(jnp.int32, cos.shape, 1)
    first = (lane % 32) < 16
    for cb in range(qkd // LANES):
        sl = slice(cb * LANES, (cb + 1) * LANES)
        tb = t[:, sl]
        sw = jnp.where(first, pltpu.roll(tb, LANES - 16, 1), pltpu.roll(tb, 16, 1))
        r = (tb * cos + sw * sin) * invf[:, sl]
        if cb < qd // LANES:
            q_ref[:, sl] = r.astype(BF16)
        else:
            p = cb - qd // LANES
            a, b = _dup_halves(r)
            kk_ref[:, (2 * p) * LANES:(2 * p + 1) * LANES] = a.astype(BF16)
            kk_ref[:, (2 * p + 1) * LANES:(2 * p + 2) * LANES] = b.astype(BF16)
    for p in range(N_KV_HEADS * HEAD_DIM // LANES):
        a, b = _dup_halves(qkv[:, qkd + p * LANES:qkd + (p + 1) * LANES])
        vv_ref[:, (2 * p) * LANES:(2 * p + 1) * LANES] = a.astype(BF16)
        vv_ref[:, (2 * p + 1) * LANES:(2 * p + 2) * LANES] = b.astype(BF16)


def _qkv0(xin, mods, per_batch, g1, w, gain, gsum, gbc, cos, sin, tm):
    b, t, d = xin.shape
    kvw = 2 * N_KV_HEADS * HEAD_DIM
    return pl.pallas_call(
        _qkv0_kernel,
        grid=(b, t // tm),
        in_specs=[
            _tok_spec(tm, d),
            _mod_spec(0, per_batch),
            _mod_spec(1, per_batch),
            _const2((1, d)),
            _const2(w.shape),
            _const2(gain.shape),
            _const2(gsum.shape),
            _const2(gbc.shape),
            pl.BlockSpec((tm, LANES), lambda b, j: (j, 0)),
            pl.BlockSpec((tm, LANES), lambda b, j: (j, 0)),
        ],
        out_specs=[_tok_spec(tm, d), _tok_spec(tm, kvw), _tok_spec(tm, kvw)],
        out_shape=[
            jax.ShapeDtypeStruct((b, t, d), BF16),
            jax.ShapeDtypeStruct((b, t, kvw), BF16),
            jax.ShapeDtypeStruct((b, t, kvw), BF16),
        ],
        compiler_params=_params(2),
        name="attn_qkv_proj",
    )(xin, mods, mods, g1, w, gain, gsum, gbc, cos, sin)


def _attn_kernel(sink_ref, q_ref, kc_ref, vc_ref, *rest, has_local, nb):
    if has_local:
        ka_ref, kb_ref, kcn_ref, va_ref, vb_ref, vcn_ref, o_ref = rest
        j = pl.program_id(1)
        stacked = (N_HEADS // N_KV_HEADS * ATTN_BLOCK, ATTN_BLOCK)
        row = lax.broadcasted_iota(jnp.int32, stacked, 0) % ATTN_BLOCK
        col = lax.broadcasted_iota(jnp.int32, stacked, 1)
        mask_a = (col >= row) & (j > 0)
        mask_c = (col <= row) & (j < nb - 1)
    else:
        (o_ref,) = rest
    tq = q_ref.shape[0]
    l_ctx = kc_ref.shape[0]
    lo = lax.broadcasted_iota(jnp.int32, (tq, LANES), 1) < HEAD_DIM
    zero = jnp.zeros((tq, LANES), BF16)
    group = N_HEADS // N_KV_HEADS
    for g in range(N_KV_HEADS):
        gs = slice(g * LANES, (g + 1) * LANES)
        qa = q_ref[:, (2 * g) * LANES:(2 * g + 1) * LANES]
        qb = q_ref[:, (2 * g + 1) * LANES:(2 * g + 2) * LANES]
        q4 = jnp.concatenate([jnp.where(lo, qa, zero), jnp.where(lo, zero, qa),
                              jnp.where(lo, qb, zero), jnp.where(lo, zero, qb)], axis=0)
        sink = jnp.concatenate([jnp.full((tq, 1), sink_ref[group * g + i], F32) for i in range(group)], axis=0)
        if has_local:
            keys = jnp.concatenate([kc_ref[:, gs], ka_ref[:, gs], kb_ref[:, gs], kcn_ref[:, gs]], axis=0)
            vals = jnp.concatenate([vc_ref[:, gs], va_ref[:, gs], vb_ref[:, gs], vcn_ref[:, gs]], axis=0)
        else:
            keys = kc_ref[:, gs]
            vals = vc_ref[:, gs]
        s = _dot_nt(q4, keys)
        parts = [s[:, i * LANES:(i + 1) * LANES] for i in range(s.shape[1] // LANES)]
        if has_local:
            ia = l_ctx // LANES
            parts[ia] = jnp.where(mask_a, parts[ia], NEG_INF)
            parts[ia + 2] = jnp.where(mask_c, parts[ia + 2], NEG_INF)
        mx = parts[0]
        for part in parts[1:]:
            mx = jnp.maximum(mx, part)
        m = jnp.maximum(sink, jnp.max(mx, axis=1, keepdims=True))
        es = [jnp.exp(part - m) for part in parts]
        tot = es[0]
        for e in es[1:]:
            tot = tot + e
        den = jnp.exp(sink - m) + jnp.sum(tot, axis=1, keepdims=True)
        p = jnp.concatenate([e.astype(BF16) for e in es], axis=1)
        o = _dot(p, vals) * (1.0 / den)
        o_ref[:, (2 * g) * LANES:(2 * g + 1) * LANES] = jnp.where(lo, o[:tq], o[tq:2 * tq]).astype(BF16)
        o_ref[:, (2 * g + 1) * LANES:(2 * g + 2) * LANES] = jnp.where(lo, o[2 * tq:3 * tq], o[3 * tq:]).astype(BF16)


def _attn_x(sink, q, kk, vv, kkc, vvc):
    b, s, d = q.shape
    l = kkc.shape[1]
    kvw = kk.shape[2]
    nb = s // ATTN_BLOCK
    blk = lambda f: pl.BlockSpec((None, ATTN_BLOCK, kvw), f)
    prev = lambda b, j: (b, jnp.maximum(j - 1, 0), 0)
    cur = lambda b, j: (b, j, 0)
    nxt = lambda b, j: (b, jnp.minimum(j + 1, nb - 1), 0)
    ctx = pl.BlockSpec((None, l, kvw), lambda b, j: (b, 0, 0))
    return pl.pallas_call(
        functools.partial(_attn_kernel, has_local=True, nb=nb),
        grid=(b, nb),
        in_specs=[
            pl.BlockSpec(memory_space=pltpu.SMEM),
            pl.BlockSpec((None, ATTN_BLOCK, d), cur),
            ctx, ctx,
            blk(prev), blk(cur), blk(nxt),
            blk(prev), blk(cur), blk(nxt),
        ],
        out_specs=pl.BlockSpec((None, ATTN_BLOCK, d), cur),
        out_shape=jax.ShapeDtypeStruct((b, s, d), BF16),
        compiler_params=_params(2),
        name="attn_latent",
    )(sink, q, kkc, vvc, kk, kk, kk, vv, vv, vv)


def _attn_c(sink, qc, kkc, vvc):
    b, l, d = qc.shape
    kvw = kkc.shape[2]
    return pl.pallas_call(
        functools.partial(_attn_kernel, has_local=False, nb=1),
        grid=(b,),
        in_specs=[
            pl.BlockSpec(memory_space=pltpu.SMEM),
            pl.BlockSpec((None, l, d), lambda b: (b, 0, 0)),
            pl.BlockSpec((None, l, kvw), lambda b: (b, 0, 0)),
            pl.BlockSpec((None, l, kvw), lambda b: (b, 0, 0)),
        ],
        out_specs=pl.BlockSpec((None, l, d), lambda b: (b, 0, 0)),
        out_shape=jax.ShapeDtypeStruct((b, l, d), BF16),
        compiler_params=_params(1),
        name="attn_context",
    )(sink, qc, kkc, vvc)


def _ffn_tail(x1, shf_ref, scf_ref, gtf_ref, g2_ref, win_ref, wout_ref, out_ref):
    h2 = (_rms(x1, g2_ref[...]) * (1.0 + scf_ref[...]) + shf_ref[...]).astype(BF16)
    acc = jnp.zeros(x1.shape, F32)
    for lo, hi in FF_CHUNKS:
        gate = _dot(h2, win_ref[:, lo:hi])
        up = _dot(h2, win_ref[:, D_FF + lo:D_FF + hi])
        act = (_silu(gate) * up).astype(BF16)
        acc = acc + _dot(act, wout_ref[lo:hi, :])
    out_ref[...] = x1 + gtf_ref[...] * acc


def _post0_kernel(x_ref, o_ref, gtm_ref, shf_ref, scf_ref, gtf_ref, g2_ref, wo_ref, win_ref, wout_ref, out_ref):
    x1 = x_ref[...] + gtm_ref[...] * _dot(o_ref[...], wo_ref[...])
    _ffn_tail(x1, shf_ref, scf_ref, gtf_ref, g2_ref, win_ref, wout_ref, out_ref)


def _post0(xin, o, mods, per_batch, g2, wo, win, wout, tm):
    b, t, d = xin.shape
    return pl.pallas_call(
        _post0_kernel,
        grid=(b, t // tm),
        in_specs=[
            _tok_spec(tm, d), _tok_spec(tm, d),
            _mod_spec(2, per_batch), _mod_spec(3, per_batch), _mod_spec(4, per_batch), _mod_spec(5, per_batch),
            _const2((1, d)), _const2(wo.shape), _const2(win.shape), _const2(wout.shape),
        ],
        out_specs=_tok_spec(tm, d),
        out_shape=jax.ShapeDtypeStruct((b, t, d), F32),
        compiler_params=_params(2),
        name="attn_out_ffn",
    )(xin, o, mods, mods, mods, mods, g2, wo, win, wout)


def _post1_kernel(x_ref, y_ref, shm_ref, scm_ref, gtm_ref, shf_ref, scf_ref, gtf_ref, g1_ref, g2_ref,
                  wg_ref, wo_ref, win_ref, wout_ref, out_ref):
    x = x_ref[...]
    h1 = (_rms(x, g1_ref[...]) * (1.0 + scm_ref[...]) + shm_ref[...]).astype(BF16)
    gated = (_silu(_dot(h1, wg_ref[...])) * y_ref[...].astype(F32)).astype(BF16)
    x1 = x + gtm_ref[...] * _dot(gated, wo_ref[...])
    _ffn_tail(x1, shf_ref, scf_ref, gtf_ref, g2_ref, win_ref, wout_ref, out_ref)


def _post1(xin, y, mods, g1, g2, wg, wo, win, wout, tm):
    b, t, d = xin.shape
    return pl.pallas_call(
        _post1_kernel,
        grid=(b, t // tm),
        in_specs=[
            _tok_spec(tm, d), _tok_spec(tm, y.shape[2]),
            *[_mod_spec(k, True) for k in range(6)],
            _const2((1, d)), _const2((1, d)),
            _const2(wg.shape), _const2(wo.shape), _const2(win.shape), _const2(wout.shape),
        ],
        out_specs=_tok_spec(tm, d),
        out_shape=jax.ShapeDtypeStruct((b, t, d), F32),
        compiler_params=_params(2),
        name="ret_out_ffn",
    )(xin, y, mods, mods, mods, mods, mods, mods, g1, g2, wg, wo, win, wout)


def _proj1_kernel(x_ref, sh_ref, sc_ref, g1_ref, w_ref, cos_ref, sin_ref, q_ref, k_ref, v_ref):
    qk_w = 2 * RET_HEADS * RET_QK_DIM
    hb = (_rms(x_ref[...], g1_ref[...]) * (1.0 + sc_ref[...]) + sh_ref[...]).astype(BF16)
    qk = _dot(hb, w_ref[:, :qk_w])
    k_scale = RET_QK_DIM ** -0.5
    for cb in range(qk_w // LANES):
        ts = slice((cb % 2) * LANES, (cb % 2 + 1) * LANES)
        tb = qk[:, cb * LANES:(cb + 1) * LANES]
        r = tb * cos_ref[:, ts] + pltpu.roll(tb, LANES // 2, 1) * sin_ref[:, ts]
        if cb < qk_w // (2 * LANES):
            q_ref[:, cb * LANES:(cb + 1) * LANES] = r.astype(BF16)
        else:
            cc = cb - qk_w // (2 * LANES)
            k_ref[:, cc * LANES:(cc + 1) * LANES] = (r * k_scale).astype(BF16)
    v_ref[...] = _dot(hb, w_ref[:, qk_w:]).astype(BF16)


def _proj1(xin, mods, per_batch, g1, w, cos, sin, tm):
    b, t, d = xin.shape
    qw = RET_HEADS * RET_QK_DIM
    return pl.pallas_call(
        _proj1_kernel,
        grid=(b, t // tm),
        in_specs=[
            _tok_spec(tm, d),
            _mod_spec(0, per_batch), _mod_spec(1, per_batch),
            _const2((1, d)), _const2(w.shape),
            pl.BlockSpec((tm, 2 * LANES), lambda b, j: (j, 0)),
            pl.BlockSpec((tm, 2 * LANES), lambda b, j: (j, 0)),
        ],
        out_specs=[_tok_spec(tm, qw), _tok_spec(tm, qw), _tok_spec(tm, RET_VWIDTH)],
        out_shape=[
            jax.ShapeDtypeStruct((b, t, qw), BF16),
            jax.ShapeDtypeStruct((b, t, qw), BF16),
            jax.ShapeDtypeStruct((b, t, RET_VWIDTH), BF16),
        ],
        compiler_params=_params(2),
        name="ret_qkv_proj",
    )(xin, mods, mods, g1, w, cos, sin)


def _ret_kernel(dl_ref, q_ref, k_ref, v_ref, kc_ref, vc_ref, gn_ref, y_ref,
                o_scr, sf_scr, sb_scr, comb_scr, qdf_scr, qdb_scr, kdf_scr, kdb_scr, cdf_scr, cdb_scr,
                ctxf_scr, ctxb_scr, *, nc):
    hd = pl.program_id(0)
    c_len = RET_CHUNK
    l_ctx = kc_ref.shape[0]

    @pl.when(pl.program_id(1) == 0)
    def _decay_tables():
        def log_decay(direction, shp):
            return -jnp.log(1.0 + jnp.exp(-jnp.full(shp, dl_ref[direction, hd], F32)))

        shape = (c_len, c_len)
        n = lax.broadcasted_iota(jnp.int32, shape, 0).astype(F32)
        m = lax.broadcasted_iota(jnp.int32, shape, 1).astype(F32)
        comb_scr[...] = (jnp.where(n >= m, jnp.exp(log_decay(0, shape) * jnp.maximum(n - m, 0.0)), 0.0)
                         + jnp.where(m >= n, jnp.exp(log_decay(1, shape) * jnp.maximum(m - n, 0.0)), 0.0))
        nk = lax.broadcasted_iota(jnp.int32, (c_len, RET_QK_DIM), 0).astype(F32)
        nv = lax.broadcasted_iota(jnp.int32, (c_len, RET_V_DIM), 0).astype(F32)
        qdf_scr[...] = jnp.exp(log_decay(0, nv.shape) * (nv + 1.0))
        kdf_scr[...] = jnp.exp(log_decay(0, nk.shape) * (c_len - 1.0 - nk))
        cdf_scr[...] = jnp.exp(log_decay(0, (1, RET_V_DIM)) * float(c_len))
        qdb_scr[...] = jnp.exp(log_decay(1, nv.shape) * (c_len - nv))
        kdb_scr[...] = jnp.exp(log_decay(1, nk.shape) * nk)
        cdb_scr[...] = jnp.exp(log_decay(1, (1, RET_V_DIM)) * float(c_len))
        tt = lax.broadcasted_iota(jnp.int32, (l_ctx, RET_QK_DIM), 0).astype(F32)
        ctxf_scr[...] = jnp.exp(log_decay(0, tt.shape) * (l_ctx - 1.0 - tt))
        ctxb_scr[...] = jnp.exp(log_decay(1, tt.shape) * tt)

    kc = kc_ref[...].astype(F32)
    vc = vc_ref[...]
    sf_scr[...] = _dot_tn((kc * ctxf_scr[...]).astype(BF16), vc)
    sb_scr[...] = _dot_tn((kc * ctxb_scr[...]).astype(BF16), vc)

    def chunk(c):
        sl = pl.ds(pl.multiple_of(c * c_len, c_len), c_len)
        return sl, q_ref[sl, :], k_ref[sl, :], v_ref[sl, :]

    def forward(c):
        sl, qc, kx, vx = chunk(c)
        st = sf_scr[...]
        inner = _dot((_dot_nt(qc, kx) * comb_scr[...]).astype(BF16), vx)
        out = inner + _dot(qc, st.astype(BF16)) * qdf_scr[...]
        sf_scr[...] = st * cdf_scr[...] + _dot_tn((kx.astype(F32) * kdf_scr[...]).astype(BF16), vx)
        return sl, out

    def backward(c):
        sl, qc, kx, vx = chunk(c)
        st = sb_scr[...]
        out = _dot(qc, st.astype(BF16)) * qdb_scr[...]
        sb_scr[...] = st * cdb_scr[...] + _dot_tn((kx.astype(F32) * kdb_scr[...]).astype(BF16), vx)
        return sl, out

    def first_visit(i, carry):
        sl, out = forward(i)
        o_scr[sl, :] = out
        sl, out = backward(nc - 1 - i)
        o_scr[sl, :] = out
        return carry

    def finish(sl, out):
        o = o_scr[sl, :] + out
        d = o - jnp.mean(o, axis=-1, keepdims=True)
        var = jnp.mean(d * d, axis=-1, keepdims=True)
        y_ref[sl, :] = (d * lax.rsqrt(var + EPS) * gn_ref[...]).astype(BF16)

    def second_visit(i, carry):
        finish(*forward(i))
        finish(*backward(nc - 1 - i))
        return carry

    lax.fori_loop(0, nc // 2, first_visit, 0, unroll=RET_UNROLL)
    lax.fori_loop(nc // 2, nc, second_visit, 0, unroll=RET_UNROLL)


def _retention(dl, q, k, v, kc, vc, gn):
    b, s, _ = q.shape
    l = kc.shape[1]
    nc = s // RET_CHUNK
    assert nc % 2 == 0
    hspec = lambda t, w: pl.BlockSpec((None, t, w), lambda h, b: (b, 0, h))
    return pl.pallas_call(
        functools.partial(_ret_kernel, nc=nc),
        grid=(RET_HEADS, b),
        in_specs=[
            pl.BlockSpec(memory_space=pltpu.SMEM),
            hspec(s, RET_QK_DIM), hspec(s, RET_QK_DIM), hspec(s, RET_V_DIM),
            hspec(l, RET_QK_DIM), hspec(l, RET_V_DIM),
            pl.BlockSpec((1, RET_V_DIM), lambda h, b: (0, h)),
        ],
        out_specs=hspec(s, RET_V_DIM),
        out_shape=jax.ShapeDtypeStruct((b, s, RET_VWIDTH), BF16),
        scratch_shapes=[
            pltpu.VMEM((s, RET_V_DIM), F32),
            pltpu.VMEM((RET_QK_DIM, RET_V_DIM), F32),
            pltpu.VMEM((RET_QK_DIM, RET_V_DIM), F32),
            pltpu.VMEM((RET_CHUNK, RET_CHUNK), F32),
            pltpu.VMEM((RET_CHUNK, RET_V_DIM), F32),
            pltpu.VMEM((RET_CHUNK, RET_V_DIM), F32),
            pltpu.VMEM((RET_CHUNK, RET_QK_DIM), F32),
            pltpu.VMEM((RET_CHUNK, RET_QK_DIM), F32),
            pltpu.VMEM((1, RET_V_DIM), F32),
            pltpu.VMEM((1, RET_V_DIM), F32),
            pltpu.VMEM((l, RET_QK_DIM), F32),
            pltpu.VMEM((l, RET_QK_DIM), F32),
        ],
        compiler_params=_params(2),
        name="retention",
    )(dl, q, k, v, kc, vc, gn)


def _rope_tables(n, head_dim):
    rows = n // GRID_W
    row = jnp.broadcast_to(jnp.arange(rows, dtype=jnp.int32)[:, None], (rows, GRID_W)).reshape(n)
    col = jnp.broadcast_to(jnp.arange(GRID_W, dtype=jnp.int32)[None, :], (rows, GRID_W)).reshape(n)
    axis_dim = head_dim // 2
    inv = ROPE_BASE ** (-jnp.arange(0, axis_dim, 2, dtype=F32) / axis_dim)
    ang_r = row.astype(F32)[:, None] * inv
    ang_c = col.astype(F32)[:, None] * inv
    cos = jnp.concatenate([jnp.cos(ang_r)] * 2 + [jnp.cos(ang_c)] * 2, axis=-1)
    sin = jnp.concatenate([-jnp.sin(ang_r), jnp.sin(ang_r), -jnp.sin(ang_c), jnp.sin(ang_c)], axis=-1)
    return cos, sin


def kernel(x, c, ctx, c_ctx, ada_w, ada_b, norm1_g, norm2_g, ffn_w_in, ffn_w_out, attn_w_qkv, attn_q_norm,
           attn_k_norm, attn_sink, attn_w_o, ret_w_qkvg, ret_decay_logit, ret_gn_g, ret_w_o):
    b, s, d = x.shape
    l = ctx.shape[1]
    qd = N_HEADS * HEAD_DIM
    kvd = N_KV_HEADS * HEAD_DIM
    tm_x = 256
    tm_c = min(256, l)

    rows = -(-(b + 1) // 8) * 8
    cin = jnp.concatenate([c, c_ctx[None, :], jnp.zeros((rows - b - 1, d), F32)], axis=0)
    mods = _mods(cin, ada_w, ada_b)
    mx0 = mods[0, :b].reshape(b, 1, 6 * d)
    mc0 = mods[0, b:b + 1].reshape(1, 1, 6 * d)
    mx1 = mods[1, :b].reshape(b, 1, 6 * d)
    mc1 = mods[1, b:b + 1].reshape(1, 1, 6 * d)

    w_qkv = attn_w_qkv[0].astype(BF16)
    scale = HEAD_DIM ** -0.5
    gain = jnp.concatenate([jnp.tile(attn_q_norm[0] * scale, N_HEADS), jnp.tile(attn_k_norm[0], N_KV_HEADS)])[None, :]
    n_norm_heads = N_HEADS + N_KV_HEADS
    head_of = jnp.arange(qd + kvd) // HEAD_DIM
    gsum = ((head_of[:, None] == jnp.arange(LANES)[None, :]).astype(F32) / HEAD_DIM).astype(BF16)
    gbc_half = (jnp.arange(LANES)[:, None] == head_of[None, :]).astype(BF16)
    gbc = jnp.concatenate([gbc_half, gbc_half], axis=0)
    del n_norm_heads
    cos64, sin64 = _rope_tables(s, HEAD_DIM)
    cos_x = jnp.tile(cos64, (1, LANES // HEAD_DIM))
    sin_x = jnp.tile(sin64, (1, LANES // HEAD_DIM))
    cos_c = jnp.ones((l, LANES), F32)
    sin_c = jnp.zeros((l, LANES), F32)
    g1 = norm1_g[0][None, :]
    g2 = norm2_g[0][None, :]
    q_x, kk_x, vv_x = _qkv0(x, mx0, True, g1, w_qkv, gain, gsum, gbc, cos_x, sin_x, tm_x)
    q_c, kk_c, vv_c = _qkv0(ctx, mc0, False, g1, w_qkv, gain, gsum, gbc, cos_c, sin_c, tm_c)
    sink = attn_sink[0].astype(F32)
    o_x = _attn_x(sink, q_x, kk_x, vv_x, kk_c, vv_c)
    o_c = _attn_c(sink, q_c, kk_c, vv_c)
    w_o = attn_w_o[0].astype(BF16)
    w_in0 = ffn_w_in[0].astype(BF16)
    w_out0 = ffn_w_out[0].astype(BF16)
    x1 = _post0(x, o_x, mx0, True, g2, w_o, w_in0, w_out0, tm_x)
    y_ctx = _post0(ctx, o_c, mc0, False, g2, w_o, w_in0, w_out0, tm_c)

    qk_w = 2 * RET_HEADS * RET_QK_DIM
    w_all = ret_w_qkvg[0].astype(BF16)
    w_qkv1 = w_all[:, :qk_w + RET_VWIDTH]
    w_g = w_all[:, qk_w + RET_VWIDTH:]
    cos256, sin256 = _rope_tables(s, RET_QK_DIM)
    g1 = norm1_g[1][None, :]
    g2 = norm2_g[1][None, :]
    q1, k1, v1 = _proj1(x1, mx1, True, g1, w_qkv1, cos256, sin256, tm_x)
    ones_c = jnp.ones((l, 2 * LANES), F32)
    _, k1c, v1c = _proj1(y_ctx, mc1, False, g1, w_qkv1, ones_c, jnp.zeros_like(ones_c), tm_c)
    y = _retention(ret_decay_logit[0].astype(F32), q1, k1, v1, k1c, v1c, ret_gn_g[0][None, :])
    return _post1(x1, y, mx1, g1, g2, w_g, ret_w_o[0].astype(BF16), ffn_w_in[1].astype(BF16),
                  ffn_w_out[1].astype(BF16), tm_x)
```

```python
import functools

import jax
import jax.numpy as jnp
from jax import lax
from jax.experimental import pallas as pl
from jax.experimental.pallas import tpu as pltpu

F32 = jnp.float32
BF16 = jnp.bfloat16

D_MODEL = 1024
GRID_W = 64
HEAD_DIM = 64
N_HEADS = D_MODEL // HEAD_DIM
N_KV_HEADS = N_HEADS // 4
WINDOW = 128
ATTN_BLOCK = 128
RET_HEADS = 4
RET_QK_DIM = 256
RET_V_DIM = 512
RET_VWIDTH = 2 * D_MODEL
RET_CHUNK = 256
ATTN_PAIRS_X = 2
ATTN_PAIRS_C = 1
RET_UNROLL = 4
D_FF = 2816
MXU_TILE = 256
FF_CHUNKS = ((0, 6 * MXU_TILE), (6 * MXU_TILE, D_FF))
ROPE_BASE = 10000.0
EPS = 1e-6
NEG_INF = -1e30
LANES = 128

VMEM_LIMIT = 56 * 1024 * 1024
TOKEN_TILE = 512


def _dot(a, b):
    return jnp.dot(a, b, preferred_element_type=F32)


def _dot_nt(a, b):
    return lax.dot_general(a, b, (((1,), (1,)), ((), ())), preferred_element_type=F32)


def _dot_tn(a, b):
    return lax.dot_general(a, b, (((0,), (0,)), ((), ())), preferred_element_type=F32)


def _rms(x, g):
    return x * lax.rsqrt(jnp.mean(x * x, axis=-1, keepdims=True) + EPS) * g


def _silu(x):
    return x * jax.nn.sigmoid(x)


def _params(n_axes):
    return pltpu.CompilerParams(dimension_semantics=("arbitrary",) * n_axes, vmem_limit_bytes=VMEM_LIMIT)


def _const2(shape):
    return pl.BlockSpec(shape, lambda b, j: (0,) * len(shape), pipeline_mode=pl.Buffered(1))


def _mod_spec(k, per_batch):
    if per_batch:
        return pl.BlockSpec((None, 1, D_MODEL), lambda b, j, k=k: (b, 0, k))
    return pl.BlockSpec((None, 1, D_MODEL), lambda b, j, k=k: (0, 0, k))


def _tok_spec(tm, width):
    return pl.BlockSpec((None, tm, width), lambda b, j: (b, j, 0))


def _mods_kernel(c_ref, w_ref, b_ref, o_ref):
    a = _silu(c_ref[...]).astype(BF16)
    o_ref[...] = _dot(a, w_ref[...].astype(BF16)) + b_ref[...]


def _mods(cin, ada_w, ada_b):
    depth, d, n = ada_w.shape
    rows = cin.shape[0]
    tn = 1536
    return pl.pallas_call(
        _mods_kernel,
        grid=(depth, n // tn),
        in_specs=[
            pl.BlockSpec((rows, d), lambda i, j: (0, 0)),
            pl.BlockSpec((None, d, tn), lambda i, j: (i, 0, j)),
            pl.BlockSpec((None, 1, tn), lambda i, j: (i, 0, j)),
        ],
        out_specs=pl.BlockSpec((None, rows, tn), lambda i, j: (i, 0, j)),
        out_shape=jax.ShapeDtypeStruct((depth, rows, n), F32),
        compiler_params=_params(2),
        name="adaln_mods",
    )(cin, ada_w, ada_b.reshape(depth, 1, n))


def _dup_halves(r):
    lane = lax.broadcasted_iota(jnp.int32, r.shape, 1)
    lo = lane < HEAD_DIM
    r64 = pltpu.roll(r, HEAD_DIM, 1)
    return jnp.where(lo, r, r64), jnp.where(lo, r64, r)


def _qkv0_kernel(x_ref, sh_ref, sc_ref, g1_ref, w_ref, gain_ref, gsum_ref, gbc_ref, cos_ref, sin_ref,
                 q_ref, kk_ref, vv_ref):
    qd = N_HEADS * HEAD_DIM
    qkd = qd + N_KV_HEADS * HEAD_DIM
    h = _rms(x_ref[...], g1_ref[...]) * (1.0 + sc_ref[...]) + sh_ref[...]
    qkv = _dot(h.astype(BF16), w_ref[...])
    qk = qkv[:, :qkd]
    ms = _dot((qk * qk).astype(BF16), gsum_ref[...])
    inv = lax.rsqrt(ms + EPS)
    hi = inv.astype(BF16)
    lo = (inv - hi.astype(F32)).astype(BF16)
    invf = _dot(jnp.concatenate([hi, lo], axis=1), gbc_ref[...])
    t = qk * gain_ref[...]
    cos = cos_ref[...]
    sin = sin_ref[...]
    lane = lax.broadcasted_iota(jnp.int32, cos.shape, 1)
    first = (lane % 32) < 16
    for cb in range(qkd // LANES):
        sl = slice(cb * LANES, (cb + 1) * LANES)
        tb = t[:, sl]
        sw = jnp.where(first, pltpu.roll(tb, LANES - 16, 1), pltpu.roll(tb, 16, 1))
        r = (tb * cos + sw * sin) * invf[:, sl]
        if cb < qd // LANES:
            q_ref[:, sl] = r.astype(BF16)
        else:
            p = cb - qd // LANES
            a, b = _dup_halves(r)
            kk_ref[:, (2 * p) * LANES:(2 * p + 1) * LANES] = a.astype(BF16)
            kk_ref[:, (2 * p + 1) * LANES:(2 * p + 2) * LANES] = b.astype(BF16)
    lo64 = lane < HEAD_DIM
    for p in range(N_KV_HEADS * HEAD_DIM // LANES):
        r = qkv[:, qkd + p * LANES:qkd + (p + 1) * LANES]
        r64 = pltpu.roll(r, HEAD_DIM, 1)
        blocks = (jnp.where(lo64, r, 1.0), jnp.where(lo64, 1.0, r64), jnp.where(lo64, r64, 1.0), jnp.where(lo64, 1.0, r))
        for i, blk in enumerate(blocks):
            vv_ref[:, (4 * p + i) * LANES:(4 * p + i + 1) * LANES] = blk.astype(BF16)


def _qkv0(xin, mods, per_batch, g1, w, gain, gsum, gbc, cos, sin, tm):
    b, t, d = xin.shape
    kw = N_KV_HEADS * LANES
    vw = N_KV_HEADS * 2 * LANES
    return pl.pallas_call(
        _qkv0_kernel,
        grid=(b, t // tm),
        in_specs=[
            _tok_spec(tm, d),
            _mod_spec(0, per_batch),
            _mod_spec(1, per_batch),
            _const2((1, d)),
            _const2(w.shape),
            _const2(gain.shape),
            _const2(gsum.shape),
            _const2(gbc.shape),
            pl.BlockSpec((tm, LANES), lambda b, j: (j, 0)),
            pl.BlockSpec((tm, LANES), lambda b, j: (j, 0)),
        ],
        out_specs=[_tok_spec(tm, d), _tok_spec(tm, kw), _tok_spec(tm, vw)],
        out_shape=[
            jax.ShapeDtypeStruct((b, t, d), BF16),
            jax.ShapeDtypeStruct((b, t, kw), BF16),
            jax.ShapeDtypeStruct((b, t, vw), BF16),
        ],
        compiler_params=_params(2),
        name="attn_qkv_proj",
    )(xin, mods, mods, g1, w, gain, gsum, gbc, cos, sin)


def _band_masks(j, nb, heads):
    stacked = (heads * ATTN_BLOCK, ATTN_BLOCK)
    row = lax.broadcasted_iota(jnp.int32, stacked, 0) % ATTN_BLOCK
    col = lax.broadcasted_iota(jnp.int32, stacked, 1)
    return (col >= row) & (j > 0), (col <= row) & (j < nb - 1)


def _sink_values():
    vrow = lax.broadcasted_iota(jnp.int32, (LANES, 2 * LANES), 0)
    vcol = lax.broadcasted_iota(jnp.int32, (LANES, 2 * LANES), 1)
    return jnp.where((vrow == 0) & (vcol >= HEAD_DIM) & (vcol < 3 * HEAD_DIM), 1.0, 0.0).astype(BF16)


def _attn_probs(sink_ref, q_ref, key_refs, pr, pairs, masks):
    tq = q_ref.shape[0]
    g = pr // (N_HEADS // N_KV_HEADS // 2)
    gs = slice(g * LANES, (g + 1) * LANES)
    lo = lax.broadcasted_iota(jnp.int32, (tq, LANES), 1) < HEAD_DIM
    zero = jnp.zeros((tq, LANES), BF16)
    stack = []
    for p in range(pr, pr + pairs):
        qb = q_ref[:, p * LANES:(p + 1) * LANES]
        stack += [jnp.where(lo, qb, zero), jnp.where(lo, zero, qb)]
    keys = jnp.concatenate([r[:, gs] for r in key_refs], axis=0) if len(key_refs) > 1 else key_refs[0][:, gs]
    s = _dot_nt(jnp.concatenate(stack, axis=0), keys)
    parts = [s[:, i * LANES:(i + 1) * LANES] for i in range(s.shape[1] // LANES)]
    if masks is not None:
        ia = key_refs[0].shape[0] // LANES
        parts[ia] = jnp.where(masks[0], parts[ia], NEG_INF)
        parts[ia + 2] = jnp.where(masks[1], parts[ia + 2], NEG_INF)
    parts.append(jnp.concatenate([jnp.full((tq, LANES), sink_ref[2 * pr + i], F32) for i in range(2 * pairs)], axis=0))
    mx = parts[0]
    for part in parts[1:]:
        mx = jnp.maximum(mx, part)
    m = jnp.max(mx, axis=1, keepdims=True)
    return jnp.concatenate([jnp.exp(part - m).astype(BF16) for part in parts], axis=1)


def _attn_weighted(p, val_refs, o_ref, pr, pairs):
    tq = o_ref.shape[0]
    g = pr // (N_HEADS // N_KV_HEADS // 2)
    gv = slice(g * 2 * LANES, (g + 1) * 2 * LANES)
    lo = lax.broadcasted_iota(jnp.int32, (tq, LANES), 1) < HEAD_DIM
    vals = jnp.concatenate([r[:, gv] for r in val_refs] + [_sink_values()], axis=0)
    o = _dot(p, vals)
    for i in range(pairs):
        even = o[2 * i * tq:(2 * i + 1) * tq, :LANES]
        odd = o[(2 * i + 1) * tq:(2 * i + 2) * tq, LANES:]
        even = even / pltpu.roll(even, HEAD_DIM, 1)
        odd = odd / pltpu.roll(odd, HEAD_DIM, 1)
        o_ref[:, (pr + i) * LANES:(pr + i + 1) * LANES] = jnp.where(lo, even, odd).astype(BF16)


def _attn_ctx_kernel(sink_ref, q_ref, kc_ref, vc_ref, o_ref, *, pairs):
    for pr in range(0, N_HEADS // 2, pairs):
        _attn_weighted(_attn_probs(sink_ref, q_ref, [kc_ref], pr, pairs, None), [vc_ref], o_ref, pr, pairs)


def _attn_band_kernel(sink_ref, q_ref, kc_ref, ka_ref, kb_ref, kn_ref, vc_ref, va_ref, vb_ref, vn_ref, o_ref,
                      *, nb, pairs):
    masks = _band_masks(pl.program_id(1), nb, 2 * pairs)
    for pr in range(0, N_HEADS // 2, pairs):
        p = _attn_probs(sink_ref, q_ref, [kc_ref, ka_ref, kb_ref, kn_ref], pr, pairs, masks)
        _attn_weighted(p, [vc_ref, va_ref, vb_ref, vn_ref], o_ref, pr, pairs)


def _attn_x(sink, q, kk, vv, kkc, vvc):
    b, s, d = q.shape
    l = kkc.shape[1]
    kw, vw = kk.shape[2], vv.shape[2]
    nb = s // ATTN_BLOCK
    blk = lambda w, f: pl.BlockSpec((None, ATTN_BLOCK, w), f)
    prev = lambda b, j: (b, jnp.maximum(j - 1, 0), 0)
    cur = lambda b, j: (b, j, 0)
    nxt = lambda b, j: (b, jnp.minimum(j + 1, nb - 1), 0)
    ctx = lambda w: pl.BlockSpec((None, l, w), lambda b, j: (b, 0, 0))
    return pl.pallas_call(
        functools.partial(_attn_band_kernel, nb=nb, pairs=ATTN_PAIRS_X),
        grid=(b, nb),
        in_specs=[
            pl.BlockSpec(memory_space=pltpu.SMEM),
            blk(d, cur),
            ctx(kw), blk(kw, prev), blk(kw, cur), blk(kw, nxt),
            ctx(vw), blk(vw, prev), blk(vw, cur), blk(vw, nxt),
        ],
        out_specs=blk(d, cur),
        out_shape=jax.ShapeDtypeStruct((b, s, d), BF16),
        compiler_params=_params(2),
        name="attn_latent",
    )(sink, q, kkc, kk, kk, kk, vvc, vv, vv, vv)


def _attn_c(sink, qc, kkc, vvc):
    b, l, d = qc.shape
    return pl.pallas_call(
        functools.partial(_attn_ctx_kernel, pairs=ATTN_PAIRS_C),
        grid=(b,),
        in_specs=[
            pl.BlockSpec(memory_space=pltpu.SMEM),
            pl.BlockSpec((None, l, d), lambda b: (b, 0, 0)),
            pl.BlockSpec((None, l, kkc.shape[2]), lambda b: (b, 0, 0)),
            pl.BlockSpec((None, l, vvc.shape[2]), lambda b: (b, 0, 0)),
        ],
        out_specs=pl.BlockSpec((None, l, d), lambda b: (b, 0, 0)),
        out_shape=jax.ShapeDtypeStruct((b, l, d), BF16),
        compiler_params=_params(1),
        name="attn_context",
    )(sink, qc, kkc, vvc)


def _ffn_tail(x1, shf_ref, scf_ref, gtf_ref, g2_ref, win_ref, wout_ref, out_ref):
    h2 = (_rms(x1, g2_ref[...]) * (1.0 + scf_ref[...]) + shf_ref[...]).astype(BF16)
    acc = jnp.zeros(x1.shape, F32)
    for lo, hi in FF_CHUNKS:
        gate = _dot(h2, win_ref[:, lo:hi])
        up = _dot(h2, win_ref[:, D_FF + lo:D_FF + hi])
        act = (_silu(gate) * up).astype(BF16)
        acc = acc + _dot(act, wout_ref[lo:hi, :])
    out_ref[...] = x1 + gtf_ref[...] * acc


def _post0_kernel(x_ref, o_ref, gtm_ref, shf_ref, scf_ref, gtf_ref, g2_ref, wo_ref, win_ref, wout_ref, out_ref):
    x1 = x_ref[...] + gtm_ref[...] * _dot(o_ref[...], wo_ref[...])
    _ffn_tail(x1, shf_ref, scf_ref, gtf_ref, g2_ref, win_ref, wout_ref, out_ref)


def _post0(xin, o, mods, per_batch, g2, wo, win, wout, tm):
    b, t, d = xin.shape
    return pl.pallas_call(
        _post0_kernel,
        grid=(b, t // tm),
        in_specs=[
            _tok_spec(tm, d), _tok_spec(tm, d),
            _mod_spec(2, per_batch), _mod_spec(3, per_batch), _mod_spec(4, per_batch), _mod_spec(5, per_batch),
            _const2((1, d)), _const2(wo.shape), _const2(win.shape), _const2(wout.shape),
        ],
        out_specs=_tok_spec(tm, d),
        out_shape=jax.ShapeDtypeStruct((b, t, d), F32),
        compiler_params=_params(2),
        name="attn_out_ffn",
    )(xin, o, mods, mods, mods, mods, g2, wo, win, wout)


def _post1_kernel(x_ref, y_ref, shm_ref, scm_ref, gtm_ref, shf_ref, scf_ref, gtf_ref, g1_ref, g2_ref, gn_ref,
                  wg_ref, wo_ref, win_ref, wout_ref, out_ref):
    x = x_ref[...]
    h1 = (_rms(x, g1_ref[...]) * (1.0 + scm_ref[...]) + shm_ref[...]).astype(BF16)
    normed = []
    for h in range(RET_HEADS):
        o = y_ref[:, h * RET_V_DIM:(h + 1) * RET_V_DIM].astype(F32)
        d = o - jnp.mean(o, axis=-1, keepdims=True)
        normed.append(d * lax.rsqrt(jnp.mean(d * d, axis=-1, keepdims=True) + EPS))
    y = jnp.concatenate(normed, axis=1) * gn_ref[...]
    gated = (_silu(_dot(h1, wg_ref[...])) * y).astype(BF16)
    x1 = x + gtm_ref[...] * _dot(gated, wo_ref[...])
    _ffn_tail(x1, shf_ref, scf_ref, gtf_ref, g2_ref, win_ref, wout_ref, out_ref)


def _post1(xin, y, mods, g1, g2, gn, wg, wo, win, wout, tm):
    b, t, d = xin.shape
    return pl.pallas_call(
        _post1_kernel,
        grid=(b, t // tm),
        in_specs=[
            _tok_spec(tm, d), _tok_spec(tm, y.shape[2]),
            *[_mod_spec(k, True) for k in range(6)],
            _const2((1, d)), _const2((1, d)), _const2(gn.shape),
            _const2(wg.shape), _const2(wo.shape), _const2(win.shape), _const2(wout.shape),
        ],
        out_specs=_tok_spec(tm, d),
        out_shape=jax.ShapeDtypeStruct((b, t, d), F32),
        compiler_params=_params(2),
        name="ret_out_ffn",
    )(xin, y, mods, mods, mods, mods, mods, mods, g1, g2, gn, wg, wo, win, wout)


def _proj1_kernel(x_ref, sh_ref, sc_ref, g1_ref, w_ref, cos_ref, sin_ref, q_ref, k_ref, v_ref):
    qk_w = 2 * RET_HEADS * RET_QK_DIM
    hb = (_rms(x_ref[...], g1_ref[...]) * (1.0 + sc_ref[...]) + sh_ref[...]).astype(BF16)
    qk = _dot(hb, w_ref[:, :qk_w])
    k_scale = RET_QK_DIM ** -0.5
    for cb in range(qk_w // LANES):
        ts = slice((cb % 2) * LANES, (cb % 2 + 1) * LANES)
        tb = qk[:, cb * LANES:(cb + 1) * LANES]
        r = tb * cos_ref[:, ts] + pltpu.roll(tb, LANES // 2, 1) * sin_ref[:, ts]
        if cb < qk_w // (2 * LANES):
            q_ref[:, cb * LANES:(cb + 1) * LANES] = r.astype(BF16)
        else:
            cc = cb - qk_w // (2 * LANES)
            k_ref[:, cc * LANES:(cc + 1) * LANES] = (r * k_scale).astype(BF16)
    v_ref[...] = _dot(hb, w_ref[:, qk_w:]).astype(BF16)


def _proj1(xin, mods, per_batch, g1, w, cos, sin, tm):
    b, t, d = xin.shape
    qw = RET_HEADS * RET_QK_DIM
    return pl.pallas_call(
        _proj1_kernel,
        grid=(b, t // tm),
        in_specs=[
            _tok_spec(tm, d),
            _mod_spec(0, per_batch), _mod_spec(1, per_batch),
            _const2((1, d)), _const2(w.shape),
            pl.BlockSpec((tm, 2 * LANES), lambda b, j: (j, 0)),
            pl.BlockSpec((tm, 2 * LANES), lambda b, j: (j, 0)),
        ],
        out_specs=[_tok_spec(tm, qw), _tok_spec(tm, qw), _tok_spec(tm, RET_VWIDTH)],
        out_shape=[
            jax.ShapeDtypeStruct((b, t, qw), BF16),
            jax.ShapeDtypeStruct((b, t, qw), BF16),
            jax.ShapeDtypeStruct((b, t, RET_VWIDTH), BF16),
        ],
        compiler_params=_params(2),
        name="ret_qkv_proj",
    )(xin, mods, mods, g1, w, cos, sin)


def _ret_kernel(dl_ref, q_ref, k_ref, v_ref, kc_ref, vc_ref, y_ref,
                o_scr, sf_scr, sb_scr, comb_scr, qdf_scr, qdb_scr, kdf_scr, kdb_scr, cdf_scr, cdb_scr,
                ctxf_scr, ctxb_scr, *, nc):
    hd = pl.program_id(0)
    c_len = RET_CHUNK
    l_ctx = kc_ref.shape[0]

    @pl.when(pl.program_id(1) == 0)
    def _decay_tables():
        def log_decay(direction, shp):
            return -jnp.log(1.0 + jnp.exp(-jnp.full(shp, dl_ref[direction, hd], F32)))

        shape = (c_len, c_len)
        n = lax.broadcasted_iota(jnp.int32, shape, 0).astype(F32)
        m = lax.broadcasted_iota(jnp.int32, shape, 1).astype(F32)
        comb_scr[...] = (jnp.where(n >= m, jnp.exp(log_decay(0, shape) * jnp.maximum(n - m, 0.0)), 0.0)
                         + jnp.where(m >= n, jnp.exp(log_decay(1, shape) * jnp.maximum(m - n, 0.0)), 0.0))
        nk = lax.broadcasted_iota(jnp.int32, (c_len, RET_QK_DIM), 0).astype(F32)
        nv = lax.broadcasted_iota(jnp.int32, (c_len, RET_V_DIM), 0).astype(F32)
        qdf_scr[...] = jnp.exp(log_decay(0, nv.shape) * (nv + 1.0))
        kdf_scr[...] = jnp.exp(log_decay(0, nk.shape) * (c_len - 1.0 - nk))
        cdf_scr[...] = jnp.exp(log_decay(0, (1, RET_V_DIM)) * float(c_len))
        qdb_scr[...] = jnp.exp(log_decay(1, nv.shape) * (c_len - nv))
        kdb_scr[...] = jnp.exp(log_decay(1, nk.shape) * nk)
        cdb_scr[...] = jnp.exp(log_decay(1, (1, RET_V_DIM)) * float(c_len))
        tt = lax.broadcasted_iota(jnp.int32, (l_ctx, RET_QK_DIM), 0).astype(F32)
        ctxf_scr[...] = jnp.exp(log_decay(0, tt.shape) * (l_ctx - 1.0 - tt))
        ctxb_scr[...] = jnp.exp(log_decay(1, tt.shape) * tt)

    kc = kc_ref[...].astype(F32)
    vc = vc_ref[...]
    sf_scr[...] = _dot_tn((kc * ctxf_scr[...]).astype(BF16), vc)
    sb_scr[...] = _dot_tn((kc * ctxb_scr[...]).astype(BF16), vc)

    def chunk(c):
        sl = pl.ds(pl.multiple_of(c * c_len, c_len), c_len)
        return sl, q_ref[sl, :], k_ref[sl, :], v_ref[sl, :]

    def forward(c):
        sl, qc, kx, vx = chunk(c)
        st = sf_scr[...]
        inner = _dot((_dot_nt(qc, kx) * comb_scr[...]).astype(BF16), vx)
        out = inner + _dot(qc, st.astype(BF16)) * qdf_scr[...]
        sf_scr[...] = st * cdf_scr[...] + _dot_tn((kx.astype(F32) * kdf_scr[...]).astype(BF16), vx)
        return sl, out

    def backward(c):
        sl, qc, kx, vx = chunk(c)
        st = sb_scr[...]
        out = _dot(qc, st.astype(BF16)) * qdb_scr[...]
        sb_scr[...] = st * cdb_scr[...] + _dot_tn((kx.astype(F32) * kdb_scr[...]).astype(BF16), vx)
        return sl, out

    def first_visit(i, carry):
        sl, out = forward(i)
        o_scr[sl, :] = out
        sl, out = backward(nc - 1 - i)
        o_scr[sl, :] = out
        return carry

    def finish(sl, out):
        y_ref[sl, :] = (o_scr[sl, :] + out).astype(BF16)

    def second_visit(i, carry):
        finish(*forward(i))
        finish(*backward(nc - 1 - i))
        return carry

    unroll = min(RET_UNROLL, nc // 2)
    lax.fori_loop(0, nc // 2, first_visit, 0, unroll=unroll)
    lax.fori_loop(nc // 2, nc, second_visit, 0, unroll=unroll)


def _retention(dl, q, k, v, kc, vc):
    b, s, _ = q.shape
    l = kc.shape[1]
    nc = s // RET_CHUNK
    assert nc % 2 == 0
    hspec = lambda t, w: pl.BlockSpec((None, t, w), lambda h, b: (b, 0, h))
    return pl.pallas_call(
        functools.partial(_ret_kernel, nc=nc),
        grid=(RET_HEADS, b),
        in_specs=[
            pl.BlockSpec(memory_space=pltpu.SMEM),
            hspec(s, RET_QK_DIM), hspec(s, RET_QK_DIM), hspec(s, RET_V_DIM),
            hspec(l, RET_QK_DIM), hspec(l, RET_V_DIM),
        ],
        out_specs=hspec(s, RET_V_DIM),
        out_shape=jax.ShapeDtypeStruct((b, s, RET_VWIDTH), BF16),
        scratch_shapes=[
            pltpu.VMEM((s, RET_V_DIM), F32),
            pltpu.VMEM((RET_QK_DIM, RET_V_DIM), F32),
            pltpu.VMEM((RET_QK_DIM, RET_V_DIM), F32),
            pltpu.VMEM((RET_CHUNK, RET_CHUNK), F32),
            pltpu.VMEM((RET_CHUNK, RET_V_DIM), F32),
            pltpu.VMEM((RET_CHUNK, RET_V_DIM), F32),
            pltpu.VMEM((RET_CHUNK, RET_QK_DIM), F32),
            pltpu.VMEM((RET_CHUNK, RET_QK_DIM), F32),
            pltpu.VMEM((1, RET_V_DIM), F32),
            pltpu.VMEM((1, RET_V_DIM), F32),
            pltpu.VMEM((l, RET_QK_DIM), F32),
            pltpu.VMEM((l, RET_QK_DIM), F32),
        ],
        compiler_params=_params(2),
        name="retention",
    )(dl, q, k, v, kc, vc)


def _rope_tables(n, head_dim):
    rows = n // GRID_W
    row = jnp.broadcast_to(jnp.arange(rows, dtype=jnp.int32)[:, None], (rows, GRID_W)).reshape(n)
    col = jnp.broadcast_to(jnp.arange(GRID_W, dtype=jnp.int32)[None, :], (rows, GRID_W)).reshape(n)
    axis_dim = head_dim // 2
    inv = ROPE_BASE ** (-jnp.arange(0, axis_dim, 2, dtype=F32) / axis_dim)
    ang_r = row.astype(F32)[:, None] * inv
    ang_c = col.astype(F32)[:, None] * inv
    cos = jnp.concatenate([jnp.cos(ang_r)] * 2 + [jnp.cos(ang_c)] * 2, axis=-1)
    sin = jnp.concatenate([-jnp.sin(ang_r), jnp.sin(ang_r), -jnp.sin(ang_c), jnp.sin(ang_c)], axis=-1)
    return cos, sin


def kernel(x, c, ctx, c_ctx, ada_w, ada_b, norm1_g, norm2_g, ffn_w_in, ffn_w_out, attn_w_qkv, attn_q_norm,
           attn_k_norm, attn_sink, attn_w_o, ret_w_qkvg, ret_decay_logit, ret_gn_g, ret_w_o):
    b, s, d = x.shape
    l = ctx.shape[1]
    qd = N_HEADS * HEAD_DIM
    kvd = N_KV_HEADS * HEAD_DIM
    tm_x = min(TOKEN_TILE, s)
    tm_c = min(TOKEN_TILE, l)

    rows = -(-(b + 1) // 8) * 8
    cin = jnp.concatenate([c, c_ctx[None, :], jnp.zeros((rows - b - 1, d), F32)], axis=0)
    mods = _mods(cin, ada_w, ada_b)
    mx0 = mods[0, :b].reshape(b, 1, 6 * d)
    mc0 = mods[0, b:b + 1].reshape(1, 1, 6 * d)
    mx1 = mods[1, :b].reshape(b, 1, 6 * d)
    mc1 = mods[1, b:b + 1].reshape(1, 1, 6 * d)

    w_qkv = attn_w_qkv[0].astype(BF16)
    scale = HEAD_DIM ** -0.5
    gain = jnp.concatenate([jnp.tile(attn_q_norm[0] * scale, N_HEADS), jnp.tile(attn_k_norm[0], N_KV_HEADS)])[None, :]
    n_norm_heads = N_HEADS + N_KV_HEADS
    head_of = jnp.arange(qd + kvd) // HEAD_DIM
    gsum = ((head_of[:, None] == jnp.arange(LANES)[None, :]).astype(F32) / HEAD_DIM).astype(BF16)
    gbc_half = (jnp.arange(LANES)[:, None] == head_of[None, :]).astype(BF16)
    gbc = jnp.concatenate([gbc_half, gbc_half], axis=0)
    del n_norm_heads
    cos64, sin64 = _rope_tables(s, HEAD_DIM)
    cos_x = jnp.tile(cos64, (1, LANES // HEAD_DIM))
    sin_x = jnp.tile(sin64, (1, LANES // HEAD_DIM))
    cos_c = jnp.ones((l, LANES), F32)
    sin_c = jnp.zeros((l, LANES), F32)
    g1 = norm1_g[0][None, :]
    g2 = norm2_g[0][None, :]
    q_x, kk_x, vv_x = _qkv0(x, mx0, True, g1, w_qkv, gain, gsum, gbc, cos_x, sin_x, tm_x)
    q_c, kk_c, vv_c = _qkv0(ctx, mc0, False, g1, w_qkv, gain, gsum, gbc, cos_c, sin_c, tm_c)
    sink = attn_sink[0].astype(F32)
    o_x = _attn_x(sink, q_x, kk_x, vv_x, kk_c, vv_c)
    o_c = _attn_c(sink, q_c, kk_c, vv_c)
    w_o = attn_w_o[0].astype(BF16)
    w_in0 = ffn_w_in[0].astype(BF16)
    w_out0 = ffn_w_out[0].astype(BF16)
    x1 = _post0(x, o_x, mx0, True, g2, w_o, w_in0, w_out0, tm_x)
    y_ctx = _post0(ctx, o_c, mc0, False, g2, w_o, w_in0, w_out0, tm_c)

    qk_w = 2 * RET_HEADS * RET_QK_DIM
    w_all = ret_w_qkvg[0].astype(BF16)
    w_qkv1 = w_all[:, :qk_w + RET_VWIDTH]
    w_g = w_all[:, qk_w + RET_VWIDTH:]
    cos256, sin256 = _rope_tables(s, RET_QK_DIM)
    g1 = norm1_g[1][None, :]
    g2 = norm2_g[1][None, :]
    q1, k1, v1 = _proj1(x1, mx1, True, g1, w_qkv1, cos256, sin256, tm_x)
    ones_c = jnp.ones((l, 2 * LANES), F32)
    _, k1c, v1c = _proj1(y_ctx, mc1, False, g1, w_qkv1, ones_c, jnp.zeros_like(ones_c), tm_c)
    y = _retention(ret_decay_logit[0].astype(F32), q1, k1, v1, k1c, v1c)
    return _post1(x1, y, mx1, g1, g2, ret_gn_g[0][None, :], w_g, ret_w_o[0].astype(BF16),
                  ffn_w_in[1].astype(BF16), ffn_w_out[1].astype(BF16), tm_x)
```

```python
import functools

import jax
import jax.numpy as jnp
from jax import lax
from jax.experimental import pallas as pl
from jax.experimental.pallas import tpu as pltpu

F32 = jnp.float32
BF16 = jnp.bfloat16

D_MODEL = 1024
GRID_W = 64
HEAD_DIM = 64
N_HEADS = D_MODEL // HEAD_DIM
N_KV_HEADS = N_HEADS // 4
WINDOW = 128
ATTN_BLOCK = 128
RET_HEADS = 4
RET_QK_DIM = 256
RET_V_DIM = 512
RET_VWIDTH = 2 * D_MODEL
RET_CHUNK = 256
ATTN_PAIRS_X = 2
ATTN_PAIRS_C = 2
ATTN_LOOKAHEAD = 2
RET_UNROLL = 4
D_FF = 2816
MXU_TILE = 256
FF_CHUNKS = ((0, 6 * MXU_TILE), (6 * MXU_TILE, D_FF))
ROW_PARTS = 2
ROPE_BASE = 10000.0
EPS = 1e-6
NEG_INF = -1e30
LANES = 128

VMEM_LIMIT = 56 * 1024 * 1024
TOKEN_TILE = 512


def _dot(a, b):
    return jnp.dot(a, b, preferred_element_type=F32)


def _dot_nt(a, b):
    return lax.dot_general(a, b, (((1,), (1,)), ((), ())), preferred_element_type=F32)


def _dot_tn(a, b):
    return lax.dot_general(a, b, (((0,), (0,)), ((), ())), preferred_element_type=F32)


def _rms(x, g):
    return x * lax.rsqrt(jnp.mean(x * x, axis=-1, keepdims=True) + EPS) * g


def _silu(x):
    return x * jax.nn.sigmoid(x)


def _params(n_axes):
    return pltpu.CompilerParams(dimension_semantics=("arbitrary",) * n_axes, vmem_limit_bytes=VMEM_LIMIT)


def _const2(shape):
    return pl.BlockSpec(shape, lambda b, j: (0,) * len(shape), pipeline_mode=pl.Buffered(1))


def _mod_spec(k, per_batch):
    if per_batch:
        return pl.BlockSpec((None, 1, D_MODEL), lambda b, j, k=k: (b, 0, k))
    return pl.BlockSpec((None, 1, D_MODEL), lambda b, j, k=k: (0, 0, k))


def _tok_spec(tm, width):
    return pl.BlockSpec((None, tm, width), lambda b, j: (b, j, 0))


def _mods_kernel(c_ref, w_ref, b_ref, o_ref):
    a = _silu(c_ref[...]).astype(BF16)
    o_ref[...] = _dot(a, w_ref[...].astype(BF16)) + b_ref[...]


def _mods(cin, ada_w, ada_b):
    depth, d, n = ada_w.shape
    rows = cin.shape[0]
    tn = 1536
    return pl.pallas_call(
        _mods_kernel,
        grid=(depth, n // tn),
        in_specs=[
            pl.BlockSpec((rows, d), lambda i, j: (0, 0)),
            pl.BlockSpec((None, d, tn), lambda i, j: (i, 0, j)),
            pl.BlockSpec((None, 1, tn), lambda i, j: (i, 0, j)),
        ],
        out_specs=pl.BlockSpec((None, rows, tn), lambda i, j: (i, 0, j)),
        out_shape=jax.ShapeDtypeStruct((depth, rows, n), F32),
        compiler_params=_params(2),
        name="adaln_mods",
    )(cin, ada_w, ada_b.reshape(depth, 1, n))


def _dup_halves(r):
    lane = lax.broadcasted_iota(jnp.int32, r.shape, 1)
    lo = lane < HEAD_DIM
    r64 = pltpu.roll(r, HEAD_DIM, 1)
    return jnp.where(lo, r, r64), jnp.where(lo, r64, r)


def _qkv0_kernel(x_ref, sh_ref, sc_ref, g1_ref, w_ref, gain_ref, gsum_ref, gbc_ref, cos_ref, sin_ref,
                 q_ref, kk_ref, vt_ref):
    qd = N_HEADS * HEAD_DIM
    qkd = qd + N_KV_HEADS * HEAD_DIM
    h = _rms(x_ref[...], g1_ref[...]) * (1.0 + sc_ref[...]) + sh_ref[...]
    qkv = _dot(h.astype(BF16), w_ref[...])
    qk = qkv[:, :qkd]
    ms = _dot((qk * qk).astype(BF16), gsum_ref[...])
    inv = lax.rsqrt(ms + EPS)
    hi = inv.astype(BF16)
    lo = (inv - hi.astype(F32)).astype(BF16)
    invf = _dot(jnp.concatenate([hi, lo], axis=1), gbc_ref[...])
    t = qk * gain_ref[...]
    cos = cos_ref[...]
    sin = sin_ref[...]
    lane = lax.broadcasted_iota(jnp.int32, cos.shape, 1)
    first = (lane % 32) < 16
    for cb in range(qkd // LANES):
        sl = slice(cb * LANES, (cb + 1) * LANES)
        tb = t[:, sl]
        sw = jnp.where(first, pltpu.roll(tb, LANES - 16, 1), pltpu.roll(tb, 16, 1))
        r = (tb * cos + sw * sin) * invf[:, sl]
        if cb < qd // LANES:
            q_ref[:, sl] = r.astype(BF16)
        else:
            p = cb - qd // LANES
            a, b = _dup_halves(r)
            kk_ref[:, (2 * p) * LANES:(2 * p + 1) * LANES] = a.astype(BF16)
            kk_ref[:, (2 * p + 1) * LANES:(2 * p + 2) * LANES] = b.astype(BF16)
    tm = qkv.shape[0]
    vt = qkv[:, qkd:].T
    row = lax.broadcasted_iota(jnp.int32, (LANES - HEAD_DIM, tm), 0)
    tail = jnp.where(row == 0, 1.0, 0.0).astype(BF16)
    for g in range(N_KV_HEADS):
        vt_ref[g * LANES:g * LANES + HEAD_DIM, :] = vt[g * HEAD_DIM:(g + 1) * HEAD_DIM, :].astype(BF16)
        vt_ref[g * LANES + HEAD_DIM:(g + 1) * LANES, :] = tail


def _qkv0(xin, mods, per_batch, g1, w, gain, gsum, gbc, cos, sin, tm):
    b, t, d = xin.shape
    kw = N_KV_HEADS * LANES
    return pl.pallas_call(
        _qkv0_kernel,
        grid=(b, t // tm),
        in_specs=[
            _tok_spec(tm, d),
            _mod_spec(0, per_batch),
            _mod_spec(1, per_batch),
            _const2((1, d)),
            _const2(w.shape),
            _const2(gain.shape),
            _const2(gsum.shape),
            _const2(gbc.shape),
            pl.BlockSpec((tm, LANES), lambda b, j: (j, 0)),
            pl.BlockSpec((tm, LANES), lambda b, j: (j, 0)),
        ],
        out_specs=[_tok_spec(tm, d), _tok_spec(tm, kw), pl.BlockSpec((None, kw, tm), lambda b, j: (b, 0, j))],
        out_shape=[
            jax.ShapeDtypeStruct((b, t, d), BF16),
            jax.ShapeDtypeStruct((b, t, kw), BF16),
            jax.ShapeDtypeStruct((b, kw, t), BF16),
        ],
        compiler_params=_params(2),
        name="attn_qkv_proj",
    )(xin, mods, mods, g1, w, gain, gsum, gbc, cos, sin)


def _band_masks(j, nb, heads):
    shape = (ATTN_BLOCK, heads * ATTN_BLOCK)
    key = lax.broadcasted_iota(jnp.int32, shape, 0)
    qry = lax.broadcasted_iota(jnp.int32, shape, 1) % ATTN_BLOCK
    return (key >= qry) & (j > 0), (key <= qry) & (j < nb - 1)


def _attn_scores(sink_ref, q_ref, key_refs, pr, pairs, masks):
    tq = q_ref.shape[0]
    g = pr // (N_HEADS // N_KV_HEADS // 2)
    gs = slice(g * LANES, (g + 1) * LANES)
    lo = lax.broadcasted_iota(jnp.int32, (tq, LANES), 1) < HEAD_DIM
    zero = jnp.zeros((tq, LANES), BF16)
    stack = []
    for p in range(pr, pr + pairs):
        qb = q_ref[:, p * LANES:(p + 1) * LANES]
        stack += [jnp.where(lo, qb, zero), jnp.where(lo, zero, qb)]
    keys = jnp.concatenate([r[:, gs] for r in key_refs], axis=0) if len(key_refs) > 1 else key_refs[0][:, gs]
    st = _dot_nt(keys, jnp.concatenate(stack, axis=0))
    parts = [st[i * LANES:(i + 1) * LANES, :] for i in range(st.shape[0] // LANES)]
    if masks is not None:
        ia = key_refs[0].shape[0] // LANES
        parts[ia] = jnp.where(masks[0], parts[ia], NEG_INF)
        parts[ia + 2] = jnp.where(masks[1], parts[ia + 2], NEG_INF)
    sink = jnp.concatenate([jnp.full((1, tq), sink_ref[2 * pr + i], F32) for i in range(2 * pairs)], axis=1)
    mx = parts[0]
    for part in parts[1:]:
        mx = jnp.maximum(mx, part)
    m = jnp.maximum(jnp.max(mx, axis=0, keepdims=True), sink)
    return jnp.concatenate([jnp.exp(part - m).astype(BF16) for part in parts], axis=0), jnp.exp(sink - m)


def _attn_values(probs, val_refs, o_ref, pr, pairs):
    pt, p_sink = probs
    tq = o_ref.shape[0]
    g = pr // (N_HEADS // N_KV_HEADS // 2)
    gs = slice(g * LANES, (g + 1) * LANES)
    vt = jnp.concatenate([r[gs, :] for r in val_refs], axis=1) if len(val_refs) > 1 else val_refs[0][gs, :]
    ot = _dot(vt, pt)
    den = jnp.sum(ot[HEAD_DIM:HEAD_DIM + 8, :], axis=0, keepdims=True) + p_sink
    out = ot[:HEAD_DIM, :] / den
    for i in range(pairs):
        pair = jnp.concatenate([out[:, 2 * i * tq:(2 * i + 1) * tq], out[:, (2 * i + 1) * tq:(2 * i + 2) * tq]], axis=0)
        o_ref[:, (pr + i) * LANES:(pr + i + 1) * LANES] = pair.T.astype(BF16)


def _attn_all_heads(sink_ref, q_ref, key_refs, val_refs, o_ref, pairs, masks):
    groups = list(range(0, N_HEADS // 2, pairs))
    probs = {}
    for step in range(len(groups) + ATTN_LOOKAHEAD):
        if step < len(groups):
            probs[step] = _attn_scores(sink_ref, q_ref, key_refs, groups[step], pairs, masks)
        done = step - ATTN_LOOKAHEAD
        if done >= 0:
            _attn_values(probs.pop(done), val_refs, o_ref, groups[done], pairs)


def _attn_ctx_kernel(sink_ref, q_ref, kc_ref, vc_ref, o_ref, *, pairs):
    _attn_all_heads(sink_ref, q_ref, [kc_ref], [vc_ref], o_ref, pairs, None)


def _attn_band_kernel(sink_ref, q_ref, kc_ref, ka_ref, kb_ref, kn_ref, vc_ref, va_ref, vb_ref, vn_ref, o_ref,
                      *, nb, pairs):
    masks = _band_masks(pl.program_id(1), nb, 2 * pairs)
    _attn_all_heads(sink_ref, q_ref, [kc_ref, ka_ref, kb_ref, kn_ref], [vc_ref, va_ref, vb_ref, vn_ref], o_ref,
                    pairs, masks)


def _attn_x(sink, q, kk, vv, kkc, vvc):
    b, s, d = q.shape
    l = kkc.shape[1]
    kw = kk.shape[2]
    nb = s // ATTN_BLOCK
    prev = lambda j: jnp.maximum(j - 1, 0)
    nxt = lambda j: jnp.minimum(j + 1, nb - 1)
    blk = lambda w, f: pl.BlockSpec((None, ATTN_BLOCK, w), lambda b, j: (b, f(j), 0))
    blk_t = lambda f: pl.BlockSpec((None, kw, ATTN_BLOCK), lambda b, j: (b, 0, f(j)))
    cur = lambda j: j
    return pl.pallas_call(
        functools.partial(_attn_band_kernel, nb=nb, pairs=ATTN_PAIRS_X),
        grid=(b, nb),
        in_specs=[
            pl.BlockSpec(memory_space=pltpu.SMEM),
            blk(d, cur),
            pl.BlockSpec((None, l, kw), lambda b, j: (b, 0, 0)), blk(kw, prev), blk(kw, cur), blk(kw, nxt),
            pl.BlockSpec((None, kw, l), lambda b, j: (b, 0, 0)), blk_t(prev), blk_t(cur), blk_t(nxt),
        ],
        out_specs=blk(d, cur),
        out_shape=jax.ShapeDtypeStruct((b, s, d), BF16),
        compiler_params=_params(2),
        name="attn_latent",
    )(sink, q, kkc, kk, kk, kk, vvc, vv, vv, vv)


def _attn_c(sink, qc, kkc, vvc):
    b, l, d = qc.shape
    return pl.pallas_call(
        functools.partial(_attn_ctx_kernel, pairs=ATTN_PAIRS_C),
        grid=(b,),
        in_specs=[
            pl.BlockSpec(memory_space=pltpu.SMEM),
            pl.BlockSpec((None, l, d), lambda b: (b, 0, 0)),
            pl.BlockSpec((None, l, kkc.shape[2]), lambda b: (b, 0, 0)),
            pl.BlockSpec((None, vvc.shape[1], l), lambda b: (b, 0, 0)),
        ],
        out_specs=pl.BlockSpec((None, l, d), lambda b: (b, 0, 0)),
        out_shape=jax.ShapeDtypeStruct((b, l, d), BF16),
        compiler_params=_params(1),
        name="attn_context",
    )(sink, qc, kkc, vvc)


def _row_parts(ref):
    tm = ref.shape[0]
    n = ROW_PARTS if tm % (ROW_PARTS * 8) == 0 else 1
    return [slice(i * tm // n, (i + 1) * tm // n) for i in range(n)]


def _ffn_tail(x1s, parts, shf_ref, scf_ref, gtf_ref, g2_ref, win_ref, wout_ref, out_ref):
    h2s = [(_rms(x1, g2_ref[...]) * (1.0 + scf_ref[...]) + shf_ref[...]).astype(BF16) for x1 in x1s]
    accs = [None] * len(x1s)
    hidden = {}
    for step in range(len(FF_CHUNKS) + 1):
        if step < len(FF_CHUNKS):
            lo, hi = FF_CHUNKS[step]
            hidden[step] = [(_dot(h2, win_ref[:, lo:hi]), _dot(h2, win_ref[:, D_FF + lo:D_FF + hi])) for h2 in h2s]
        if step > 0:
            lo, hi = FF_CHUNKS[step - 1]
            for i, (gate, up) in enumerate(hidden.pop(step - 1)):
                down = _dot((_silu(gate) * up).astype(BF16), wout_ref[lo:hi, :])
                accs[i] = down if accs[i] is None else accs[i] + down
    for rows, x1, acc in zip(parts, x1s, accs):
        out_ref[rows, :] = x1 + gtf_ref[...] * acc


def _post0_kernel(x_ref, o_ref, gtm_ref, shf_ref, scf_ref, gtf_ref, g2_ref, wo_ref, win_ref, wout_ref, out_ref):
    parts = _row_parts(x_ref)
    mixed = [_dot(o_ref[rows, :], wo_ref[...]) for rows in parts]
    x1s = [x_ref[rows, :] + gtm_ref[...] * a for rows, a in zip(parts, mixed)]
    _ffn_tail(x1s, parts, shf_ref, scf_ref, gtf_ref, g2_ref, win_ref, wout_ref, out_ref)


def _post0(xin, o, mods, per_batch, g2, wo, win, wout, tm):
    b, t, d = xin.shape
    return pl.pallas_call(
        _post0_kernel,
        grid=(b, t // tm),
        in_specs=[
            _tok_spec(tm, d), _tok_spec(tm, d),
            _mod_spec(2, per_batch), _mod_spec(3, per_batch), _mod_spec(4, per_batch), _mod_spec(5, per_batch),
            _const2((1, d)), _const2(wo.shape), _const2(win.shape), _const2(wout.shape),
        ],
        out_specs=_tok_spec(tm, d),
        out_shape=jax.ShapeDtypeStruct((b, t, d), F32),
        compiler_params=_params(2),
        name="attn_out_ffn",
    )(xin, o, mods, mods, mods, mods, g2, wo, win, wout)


def _post1_kernel(x_ref, y_ref, shm_ref, scm_ref, gtm_ref, shf_ref, scf_ref, gtf_ref, g1_ref, g2_ref, gn_ref,
                  wg_ref, wo_ref, win_ref, wout_ref, out_ref):
    parts = _row_parts(x_ref)
    xs = [x_ref[rows, :] for rows in parts]
    h1s = [(_rms(x, g1_ref[...]) * (1.0 + scm_ref[...]) + shm_ref[...]).astype(BF16) for x in xs]
    gates = [_dot(h1, wg_ref[...]) for h1 in h1s]
    gated = []
    for rows, gate in zip(parts, gates):
        normed = []
        for h in range(RET_HEADS):
            o = y_ref[rows, h * RET_V_DIM:(h + 1) * RET_V_DIM].astype(F32)
            d = o - jnp.mean(o, axis=-1, keepdims=True)
            normed.append(d * lax.rsqrt(jnp.mean(d * d, axis=-1, keepdims=True) + EPS))
        gated.append((_silu(gate) * (jnp.concatenate(normed, axis=1) * gn_ref[...])).astype(BF16))
    mixed = [_dot(g, wo_ref[...]) for g in gated]
    x1s = [x + gtm_ref[...] * a for x, a in zip(xs, mixed)]
    _ffn_tail(x1s, parts, shf_ref, scf_ref, gtf_ref, g2_ref, win_ref, wout_ref, out_ref)


def _post1(xin, y, mods, g1, g2, gn, wg, wo, win, wout, tm):
    b, t, d = xin.shape
    return pl.pallas_call(
        _post1_kernel,
        grid=(b, t // tm),
        in_specs=[
            _tok_spec(tm, d), _tok_spec(tm, y.shape[2]),
            *[_mod_spec(k, True) for k in range(6)],
            _const2((1, d)), _const2((1, d)), _const2(gn.shape),
            pl.BlockSpec((d, RET_VWIDTH), lambda b, j: (0, wg.shape[1] // RET_VWIDTH - 1), pipeline_mode=pl.Buffered(1)),
            _const2(wo.shape), _const2(win.shape), _const2(wout.shape),
        ],
        out_specs=_tok_spec(tm, d),
        out_shape=jax.ShapeDtypeStruct((b, t, d), F32),
        compiler_params=_params(2),
        name="ret_out_ffn",
    )(xin, y, mods, mods, mods, mods, mods, mods, g1, g2, gn, wg, wo, win, wout)


def _proj1_kernel(x_ref, sh_ref, sc_ref, g1_ref, w_ref, cos_ref, sin_ref, q_ref, k_ref, v_ref):
    qk_w = 2 * RET_HEADS * RET_QK_DIM
    hb = (_rms(x_ref[...], g1_ref[...]) * (1.0 + sc_ref[...]) + sh_ref[...]).astype(BF16)
    qk = _dot(hb, w_ref[:, :qk_w])
    k_scale = RET_QK_DIM ** -0.5
    for cb in range(qk_w // LANES):
        ts = slice((cb % 2) * LANES, (cb % 2 + 1) * LANES)
        tb = qk[:, cb * LANES:(cb + 1) * LANES]
        r = tb * cos_ref[:, ts] + pltpu.roll(tb, LANES // 2, 1) * sin_ref[:, ts]
        if cb < qk_w // (2 * LANES):
            q_ref[:, cb * LANES:(cb + 1) * LANES] = r.astype(BF16)
        else:
            cc = cb - qk_w // (2 * LANES)
            k_ref[:, cc * LANES:(cc + 1) * LANES] = (r * k_scale).astype(BF16)
    v_ref[...] = _dot(hb, w_ref[:, qk_w:]).astype(BF16)


def _proj1(xin, mods, per_batch, g1, w, cos, sin, tm):
    b, t, d = xin.shape
    qw = RET_HEADS * RET_QK_DIM
    return pl.pallas_call(
        _proj1_kernel,
        grid=(b, t // tm),
        in_specs=[
            _tok_spec(tm, d),
            _mod_spec(0, per_batch), _mod_spec(1, per_batch),
            _const2((1, d)),
            pl.BlockSpec((d, 2 * qw + RET_VWIDTH), lambda b, j: (0, 0), pipeline_mode=pl.Buffered(1)),
            pl.BlockSpec((tm, 2 * LANES), lambda b, j: (j, 0)),
            pl.BlockSpec((tm, 2 * LANES), lambda b, j: (j, 0)),
        ],
        out_specs=[_tok_spec(tm, qw), _tok_spec(tm, qw), _tok_spec(tm, RET_VWIDTH)],
        out_shape=[
            jax.ShapeDtypeStruct((b, t, qw), BF16),
            jax.ShapeDtypeStruct((b, t, qw), BF16),
            jax.ShapeDtypeStruct((b, t, RET_VWIDTH), BF16),
        ],
        compiler_params=_params(2),
        name="ret_qkv_proj",
    )(xin, mods, mods, g1, w, cos, sin)


def _ret_kernel(dl_ref, q_ref, k_ref, v_ref, kc_ref, vc_ref, y_ref,
                o_scr, sf_scr, sb_scr, comb_scr, qdf_scr, qdb_scr, kdf_scr, kdb_scr, cdf_scr, cdb_scr,
                ctxf_scr, ctxb_scr, *, nc):
    hd = pl.program_id(0)
    c_len = RET_CHUNK
    l_ctx = kc_ref.shape[0]

    @pl.when(pl.program_id(1) == 0)
    def _decay_tables():
        def log_decay(direction, shp):
            return -jnp.log(1.0 + jnp.exp(-jnp.full(shp, dl_ref[direction, hd], F32)))

        shape = (c_len, c_len)
        n = lax.broadcasted_iota(jnp.int32, shape, 0).astype(F32)
        m = lax.broadcasted_iota(jnp.int32, shape, 1).astype(F32)
        comb_scr[...] = (jnp.where(n >= m, jnp.exp(log_decay(0, shape) * jnp.maximum(n - m, 0.0)), 0.0)
                         + jnp.where(m >= n, jnp.exp(log_decay(1, shape) * jnp.maximum(m - n, 0.0)), 0.0))
        nk = lax.broadcasted_iota(jnp.int32, (c_len, RET_QK_DIM), 0).astype(F32)
        nv = lax.broadcasted_iota(jnp.int32, (c_len, RET_V_DIM), 0).astype(F32)
        qdf_scr[...] = jnp.exp(log_decay(0, nv.shape) * (nv + 1.0))
        kdf_scr[...] = jnp.exp(log_decay(0, nk.shape) * (c_len - 1.0 - nk))
        cdf_scr[...] = jnp.exp(log_decay(0, (1, RET_V_DIM)) * float(c_len))
        qdb_scr[...] = jnp.exp(log_decay(1, nv.shape) * (c_len - nv))
        kdb_scr[...] = jnp.exp(log_decay(1, nk.shape) * nk)
        cdb_scr[...] = jnp.exp(log_decay(1, (1, RET_V_DIM)) * float(c_len))
        tt = lax.broadcasted_iota(jnp.int32, (l_ctx, RET_QK_DIM), 0).astype(F32)
        ctxf_scr[...] = jnp.exp(log_decay(0, tt.shape) * (l_ctx - 1.0 - tt))
        ctxb_scr[...] = jnp.exp(log_decay(1, tt.shape) * tt)

    kc = kc_ref[...].astype(F32)
    vc = vc_ref[...]
    sf_scr[...] = _dot_tn((kc * ctxf_scr[...]).astype(BF16), vc)
    sb_scr[...] = _dot_tn((kc * ctxb_scr[...]).astype(BF16), vc)

    def chunk(c):
        sl = pl.ds(pl.multiple_of(c * c_len, c_len), c_len)
        return sl, q_ref[sl, :], k_ref[sl, :], v_ref[sl, :]

    def forward(c):
        sl, qc, kx, vx = chunk(c)
        st = sf_scr[...]
        inner = _dot((_dot_nt(qc, kx) * comb_scr[...]).astype(BF16), vx)
        out = inner + _dot(qc, st.astype(BF16)) * qdf_scr[...]
        sf_scr[...] = st * cdf_scr[...] + _dot_tn((kx.astype(F32) * kdf_scr[...]).astype(BF16), vx)
        return sl, out

    def backward(c):
        sl, qc, kx, vx = chunk(c)
        st = sb_scr[...]
        out = _dot(qc, st.astype(BF16)) * qdb_scr[...]
        sb_scr[...] = st * cdb_scr[...] + _dot_tn((kx.astype(F32) * kdb_scr[...]).astype(BF16), vx)
        return sl, out

    def first_visit(i, carry):
        sl, out = forward(i)
        o_scr[sl, :] = out
        sl, out = backward(nc - 1 - i)
        o_scr[sl, :] = out
        return carry

    def finish(sl, out):
        y_ref[sl, :] = (o_scr[sl, :] + out).astype(BF16)

    def second_visit(i, carry):
        finish(*forward(i))
        finish(*backward(nc - 1 - i))
        return carry

    unroll = min(RET_UNROLL, nc // 2)
    lax.fori_loop(0, nc // 2, first_visit, 0, unroll=unroll)
    lax.fori_loop(nc // 2, nc, second_visit, 0, unroll=unroll)


def _retention(dl, q, k, v, kc, vc):
    b, s, _ = q.shape
    l = kc.shape[1]
    nc = s // RET_CHUNK
    assert nc % 2 == 0
    hspec = lambda t, w: pl.BlockSpec((None, t, w), lambda h, b: (b, 0, h))
    return pl.pallas_call(
        functools.partial(_ret_kernel, nc=nc),
        grid=(RET_HEADS, b),
        in_specs=[
            pl.BlockSpec(memory_space=pltpu.SMEM),
            hspec(s, RET_QK_DIM), hspec(s, RET_QK_DIM), hspec(s, RET_V_DIM),
            hspec(l, RET_QK_DIM), hspec(l, RET_V_DIM),
        ],
        out_specs=hspec(s, RET_V_DIM),
        out_shape=jax.ShapeDtypeStruct((b, s, RET_VWIDTH), BF16),
        scratch_shapes=[
            pltpu.VMEM((s, RET_V_DIM), F32),
            pltpu.VMEM((RET_QK_DIM, RET_V_DIM), F32),
            pltpu.VMEM((RET_QK_DIM, RET_V_DIM), F32),
            pltpu.VMEM((RET_CHUNK, RET_CHUNK), F32),
            pltpu.VMEM((RET_CHUNK, RET_V_DIM), F32),
            pltpu.VMEM((RET_CHUNK, RET_V_DIM), F32),
            pltpu.VMEM((RET_CHUNK, RET_QK_DIM), F32),
            pltpu.VMEM((RET_CHUNK, RET_QK_DIM), F32),
            pltpu.VMEM((1, RET_V_DIM), F32),
            pltpu.VMEM((1, RET_V_DIM), F32),
            pltpu.VMEM((l, RET_QK_DIM), F32),
            pltpu.VMEM((l, RET_QK_DIM), F32),
        ],
        compiler_params=_params(2),
        name="retention",
    )(dl, q, k, v, kc, vc)


def _rope_tables(n, head_dim):
    rows = n // GRID_W
    row = jnp.broadcast_to(jnp.arange(rows, dtype=jnp.int32)[:, None], (rows, GRID_W)).reshape(n)
    col = jnp.broadcast_to(jnp.arange(GRID_W, dtype=jnp.int32)[None, :], (rows, GRID_W)).reshape(n)
    axis_dim = head_dim // 2
    inv = ROPE_BASE ** (-jnp.arange(0, axis_dim, 2, dtype=F32) / axis_dim)
    ang_r = row.astype(F32)[:, None] * inv
    ang_c = col.astype(F32)[:, None] * inv
    cos = jnp.concatenate([jnp.cos(ang_r)] * 2 + [jnp.cos(ang_c)] * 2, axis=-1)
    sin = jnp.concatenate([-jnp.sin(ang_r), jnp.sin(ang_r), -jnp.sin(ang_c), jnp.sin(ang_c)], axis=-1)
    return cos, sin


def kernel(x, c, ctx, c_ctx, ada_w, ada_b, norm1_g, norm2_g, ffn_w_in, ffn_w_out, attn_w_qkv, attn_q_norm,
           attn_k_norm, attn_sink, attn_w_o, ret_w_qkvg, ret_decay_logit, ret_gn_g, ret_w_o):
    b, s, d = x.shape
    l = ctx.shape[1]
    qd = N_HEADS * HEAD_DIM
    kvd = N_KV_HEADS * HEAD_DIM
    tm_x = min(TOKEN_TILE, s)
    tm_c = min(TOKEN_TILE, l)

    rows = -(-(b + 1) // 8) * 8
    cin = jnp.concatenate([c, c_ctx[None, :], jnp.zeros((rows - b - 1, d), F32)], axis=0)
    mods = _mods(cin, ada_w, ada_b)
    mx0 = mods[0, :b].reshape(b, 1, 6 * d)
    mc0 = mods[0, b:b + 1].reshape(1, 1, 6 * d)
    mx1 = mods[1, :b].reshape(b, 1, 6 * d)
    mc1 = mods[1, b:b + 1].reshape(1, 1, 6 * d)

    w_qkv = attn_w_qkv[0].astype(BF16)
    scale = HEAD_DIM ** -0.5
    gain = jnp.concatenate([jnp.tile(attn_q_norm[0] * scale, N_HEADS), jnp.tile(attn_k_norm[0], N_KV_HEADS)])[None, :]
    n_norm_heads = N_HEADS + N_KV_HEADS
    head_of = jnp.arange(qd + kvd) // HEAD_DIM
    gsum = ((head_of[:, None] == jnp.arange(LANES)[None, :]).astype(F32) / HEAD_DIM).astype(BF16)
    gbc_half = (jnp.arange(LANES)[:, None] == head_of[None, :]).astype(BF16)
    gbc = jnp.concatenate([gbc_half, gbc_half], axis=0)
    del n_norm_heads
    cos64, sin64 = _rope_tables(s, HEAD_DIM)
    cos_x = jnp.tile(cos64, (1, LANES // HEAD_DIM))
    sin_x = jnp.tile(sin64, (1, LANES // HEAD_DIM))
    cos_c = jnp.ones((l, LANES), F32)
    sin_c = jnp.zeros((l, LANES), F32)
    g1 = norm1_g[0][None, :]
    g2 = norm2_g[0][None, :]
    q_x, kk_x, vv_x = _qkv0(x, mx0, True, g1, w_qkv, gain, gsum, gbc, cos_x, sin_x, tm_x)
    q_c, kk_c, vv_c = _qkv0(ctx, mc0, False, g1, w_qkv, gain, gsum, gbc, cos_c, sin_c, tm_c)
    sink = attn_sink[0].astype(F32)
    o_x = _attn_x(sink, q_x, kk_x, vv_x, kk_c, vv_c)
    o_c = _attn_c(sink, q_c, kk_c, vv_c)
    w_o = attn_w_o[0].astype(BF16)
    w_in0 = ffn_w_in[0].astype(BF16)
    w_out0 = ffn_w_out[0].astype(BF16)
    x1 = _post0(x, o_x, mx0, True, g2, w_o, w_in0, w_out0, tm_x)
    y_ctx = _post0(ctx, o_c, mc0, False, g2, w_o, w_in0, w_out0, tm_c)

    w_qkvg = ret_w_qkvg[0].astype(BF16)
    cos256, sin256 = _rope_tables(s, RET_QK_DIM)
    g1 = norm1_g[1][None, :]
    g2 = norm2_g[1][None, :]
    q1, k1, v1 = _proj1(x1, mx1, True, g1, w_qkvg, cos256, sin256, tm_x)
    ones_c = jnp.ones((l, 2 * LANES), F32)
    _, k1c, v1c = _proj1(y_ctx, mc1, False, g1, w_qkvg, ones_c, jnp.zeros_like(ones_c), tm_c)
    y = _retention(ret_decay_logit[0].astype(F32), q1, k1, v1, k1c, v1c)
    return _post1(x1, y, mx1, g1, g2, ret_gn_g[0][None, :], w_qkvg, ret_w_o[0].astype(BF16),
                  ffn_w_in[1].astype(BF16), ffn_w_out[1].astype(BF16), tm_x)
```

```python
import functools

import jax
import jax.numpy as jnp
from jax import lax
from jax.experimental import pallas as pl
from jax.experimental.pallas import tpu as pltpu

F32 = jnp.float32
BF16 = jnp.bfloat16

D_MODEL = 1024
GRID_W = 64
HEAD_DIM = 64
N_HEADS = D_MODEL // HEAD_DIM
N_KV_HEADS = N_HEADS // 4
WINDOW = 128
ATTN_BLOCK = 128
RET_HEADS = 4
RET_QK_DIM = 256
RET_V_DIM = 512
RET_VWIDTH = 2 * D_MODEL
RET_CHUNK = 256
ATTN_PAIRS_X = 2
ATTN_PAIRS_C = 2
ATTN_LOOKAHEAD = 2
RET_LOOKAHEAD = 2
D_FF = 2816
MXU_TILE = 256
FF_CHUNKS = ((0, 6 * MXU_TILE), (6 * MXU_TILE, D_FF))
ROW_PARTS = 2
ROPE_BASE = 10000.0
EPS = 1e-6
NEG_INF = -1e30
LOG2E = 1.4426950408889634
LANES = 128

VMEM_LIMIT = 56 * 1024 * 1024
TOKEN_TILE = 512


def _dot(a, b):
    return jnp.dot(a, b, preferred_element_type=F32)


def _dot_nt(a, b):
    return lax.dot_general(a, b, (((1,), (1,)), ((), ())), preferred_element_type=F32)


def _dot_tn(a, b):
    return lax.dot_general(a, b, (((0,), (0,)), ((), ())), preferred_element_type=F32)


def _rms(x, g):
    return x * lax.rsqrt(jnp.mean(x * x, axis=-1, keepdims=True) + EPS) * g


def _silu(x):
    return x * jax.nn.sigmoid(x)


def _params(n_axes):
    return pltpu.CompilerParams(dimension_semantics=("arbitrary",) * n_axes, vmem_limit_bytes=VMEM_LIMIT)


def _const2(shape):
    return pl.BlockSpec(shape, lambda b, j: (0,) * len(shape), pipeline_mode=pl.Buffered(1))


def _mod_spec(k, per_batch):
    if per_batch:
        return pl.BlockSpec((None, 1, D_MODEL), lambda b, j, k=k: (b, 0, k))
    return pl.BlockSpec((None, 1, D_MODEL), lambda b, j, k=k: (0, 0, k))


def _tok_spec(tm, width):
    return pl.BlockSpec((None, tm, width), lambda b, j: (b, j, 0))


def _row_parts(ref, n):
    tm = ref.shape[0]
    n = n if tm % (n * ATTN_BLOCK) == 0 else 1
    return [slice(i * tm // n, (i + 1) * tm // n) for i in range(n)]


def _mods_kernel(c_ref, w_ref, b_ref, o_ref):
    a = _silu(c_ref[...]).astype(BF16)
    o_ref[...] = _dot(a, w_ref[...].astype(BF16)) + b_ref[...]


def _mods(cin, ada_w, ada_b):
    depth, d, n = ada_w.shape
    rows = cin.shape[0]
    tn = 1536
    return pl.pallas_call(
        _mods_kernel,
        grid=(depth, n // tn),
        in_specs=[
            pl.BlockSpec((rows, d), lambda i, j: (0, 0)),
            pl.BlockSpec((None, d, tn), lambda i, j: (i, 0, j)),
            pl.BlockSpec((None, 1, tn), lambda i, j: (i, 0, j)),
        ],
        out_specs=pl.BlockSpec((None, rows, tn), lambda i, j: (i, 0, j)),
        out_shape=jax.ShapeDtypeStruct((depth, rows, n), F32),
        compiler_params=_params(2),
        name="adaln_mods",
    )(cin, ada_w, ada_b.reshape(depth, 1, n))


def _dup_halves(r):
    lane = lax.broadcasted_iota(jnp.int32, r.shape, 1)
    lo = lane < HEAD_DIM
    r64 = pltpu.roll(r, HEAD_DIM, 1)
    return jnp.where(lo, r, r64), jnp.where(lo, r64, r)


def _qkv0_kernel(x_ref, sh_ref, sc_ref, g1_ref, w_ref, gain_ref, gsum_ref, gbc_ref, cos_ref, sin_ref,
                 q_ref, kk_ref, vt_ref):
    qd = N_HEADS * HEAD_DIM
    qkd = qd + N_KV_HEADS * HEAD_DIM
    parts = _row_parts(x_ref, ROW_PARTS)
    hs = [_rms(x_ref[rows, :], g1_ref[...]) * (1.0 + sc_ref[...]) + sh_ref[...] for rows in parts]
    qkvs = [_dot(h.astype(BF16), w_ref[...]) for h in hs]
    mss = [_dot((qkv[:, :qkd] * qkv[:, :qkd]).astype(BF16), gsum_ref[...]) for qkv in qkvs]
    invfs = []
    for ms in mss:
        inv = lax.rsqrt(ms + EPS)
        hi = inv.astype(BF16)
        lo = (inv - hi.astype(F32)).astype(BF16)
        invfs.append(_dot(jnp.concatenate([hi, lo], axis=1), gbc_ref[...]))
    row = lax.broadcasted_iota(jnp.int32, (LANES - HEAD_DIM, ATTN_BLOCK), 0)
    tail = jnp.where(row == 0, 1.0, 0.0).astype(BF16)
    for rows, qkv, invf in zip(parts, qkvs, invfs):
        t = qkv[:, :qkd] * gain_ref[...]
        cos = cos_ref[rows, :]
        sin = sin_ref[rows, :]
        lane = lax.broadcasted_iota(jnp.int32, cos.shape, 1)
        first = (lane % 32) < 16
        for cb in range(qkd // LANES):
            sl = slice(cb * LANES, (cb + 1) * LANES)
            tb = t[:, sl]
            sw = jnp.where(first, pltpu.roll(tb, LANES - 16, 1), pltpu.roll(tb, 16, 1))
            r = (tb * cos + sw * sin) * invf[:, sl]
            if cb < qd // LANES:
                q_ref[rows, sl] = r.astype(BF16)
            else:
                p = cb - qd // LANES
                a, b = _dup_halves(r)
                kk_ref[rows, (2 * p) * LANES:(2 * p + 1) * LANES] = a.astype(BF16)
                kk_ref[rows, (2 * p + 1) * LANES:(2 * p + 2) * LANES] = b.astype(BF16)
        vt = qkv[:, qkd:].T
        for blk in range(vt.shape[1] // ATTN_BLOCK):
            cols = slice(blk * ATTN_BLOCK, (blk + 1) * ATTN_BLOCK)
            slab = rows.start // ATTN_BLOCK + blk
            for g in range(N_KV_HEADS):
                vt_ref[slab, g * LANES:g * LANES + HEAD_DIM, :] = vt[g * HEAD_DIM:(g + 1) * HEAD_DIM, cols].astype(BF16)
                vt_ref[slab, g * LANES + HEAD_DIM:(g + 1) * LANES, :] = tail


def _qkv0(xin, mods, per_batch, g1, w, gain, gsum, gbc, cos, sin, tm):
    b, t, d = xin.shape
    kw = N_KV_HEADS * LANES
    return pl.pallas_call(
        _qkv0_kernel,
        grid=(b, t // tm),
        in_specs=[
            _tok_spec(tm, d),
            _mod_spec(0, per_batch),
            _mod_spec(1, per_batch),
            _const2((1, d)),
            _const2(w.shape),
            _const2(gain.shape),
            _const2(gsum.shape),
            _const2(gbc.shape),
            pl.BlockSpec((tm, LANES), lambda b, j: (j, 0)),
            pl.BlockSpec((tm, LANES), lambda b, j: (j, 0)),
        ],
        out_specs=[_tok_spec(tm, d), _tok_spec(tm, kw),
                   pl.BlockSpec((None, tm // ATTN_BLOCK, kw, ATTN_BLOCK), lambda b, j: (b, j, 0, 0))],
        out_shape=[
            jax.ShapeDtypeStruct((b, t, d), BF16),
            jax.ShapeDtypeStruct((b, t, kw), BF16),
            jax.ShapeDtypeStruct((b, t // ATTN_BLOCK, kw, ATTN_BLOCK), BF16),
        ],
        compiler_params=_params(2),
        name="attn_qkv_proj",
    )(xin, mods, mods, g1, w, gain, gsum, gbc, cos, sin)


def _band_masks(j, nb, heads):
    shape = (ATTN_BLOCK, heads * ATTN_BLOCK)
    key = lax.broadcasted_iota(jnp.int32, shape, 0)
    qry = lax.broadcasted_iota(jnp.int32, shape, 1) % ATTN_BLOCK
    return (key >= qry) & (j > 0), (key <= qry) & (j < nb - 1)


def _attn_scores(sink_ref, q_ref, key_refs, pr, pairs, masks):
    tq = q_ref.shape[0]
    g = pr // (N_HEADS // N_KV_HEADS // 2)
    gs = slice(g * LANES, (g + 1) * LANES)
    lo = lax.broadcasted_iota(jnp.int32, (tq, LANES), 1) < HEAD_DIM
    zero = jnp.zeros((tq, LANES), BF16)
    stack = []
    for p in range(pr, pr + pairs):
        qb = q_ref[:, p * LANES:(p + 1) * LANES]
        stack += [jnp.where(lo, qb, zero), jnp.where(lo, zero, qb)]
    keys = jnp.concatenate([r[:, gs] for r in key_refs], axis=0) if len(key_refs) > 1 else key_refs[0][:, gs]
    st = _dot_nt(keys, jnp.concatenate(stack, axis=0))
    parts = [st[i * LANES:(i + 1) * LANES, :] for i in range(st.shape[0] // LANES)]
    if masks is not None:
        ia = key_refs[0].shape[0] // LANES
        parts[ia] = jnp.where(masks[0], parts[ia], NEG_INF)
        parts[ia + 2] = jnp.where(masks[1], parts[ia + 2], NEG_INF)
    sink = jnp.concatenate([jnp.full((1, tq), sink_ref[2 * pr + i] * LOG2E, F32) for i in range(2 * pairs)], axis=1)
    mx = parts[0]
    for part in parts[1:]:
        mx = jnp.maximum(mx, part)
    m = jnp.maximum(jnp.max(mx, axis=0, keepdims=True), sink)
    return jnp.concatenate([jnp.exp2(part - m).astype(BF16) for part in parts], axis=0), jnp.exp2(sink - m)


def _attn_values(probs, val_refs, o_ref, pr, pairs):
    pt, p_sink = probs
    tq = o_ref.shape[0]
    g = pr // (N_HEADS // N_KV_HEADS // 2)
    gs = slice(g * LANES, (g + 1) * LANES)
    vt = jnp.concatenate([r[i, gs, :] for r in val_refs for i in range(r.shape[0])], axis=1)
    ot = _dot(vt, pt)
    den = jnp.sum(ot[HEAD_DIM:HEAD_DIM + 8, :], axis=0, keepdims=True) + p_sink
    out = ot[:HEAD_DIM, :] / den
    for i in range(pairs):
        pair = jnp.concatenate([out[:, 2 * i * tq:(2 * i + 1) * tq], out[:, (2 * i + 1) * tq:(2 * i + 2) * tq]], axis=0)
        o_ref[:, (pr + i) * LANES:(pr + i + 1) * LANES] = pair.T.astype(BF16)


def _attn_all_heads(sink_ref, q_ref, key_refs, val_refs, o_ref, pairs, masks):
    groups = list(range(0, N_HEADS // 2, pairs))
    probs = {}
    for step in range(len(groups) + ATTN_LOOKAHEAD):
        if step < len(groups):
            probs[step] = _attn_scores(sink_ref, q_ref, key_refs, groups[step], pairs, masks)
        done = step - ATTN_LOOKAHEAD
        if done >= 0:
            _attn_values(probs.pop(done), val_refs, o_ref, groups[done], pairs)


def _attn_ctx_kernel(sink_ref, q_ref, kc_ref, vc_ref, o_ref, *, pairs):
    _attn_all_heads(sink_ref, q_ref, [kc_ref], [vc_ref], o_ref, pairs, None)


def _attn_band_kernel(sink_ref, q_ref, kc_ref, ka_ref, kb_ref, kn_ref, vc_ref, va_ref, vb_ref, vn_ref, o_ref,
                      *, nb, pairs):
    masks = _band_masks(pl.program_id(1), nb, 2 * pairs)
    _attn_all_heads(sink_ref, q_ref, [kc_ref, ka_ref, kb_ref, kn_ref], [vc_ref, va_ref, vb_ref, vn_ref], o_ref,
                    pairs, masks)


def _attn_x(sink, q, kk, vv, kkc, vvc):
    b, s, d = q.shape
    l = kkc.shape[1]
    kw = kk.shape[2]
    nb = s // ATTN_BLOCK
    prev = lambda j: jnp.maximum(j - 1, 0)
    nxt = lambda j: jnp.minimum(j + 1, nb - 1)
    blk = lambda w, f: pl.BlockSpec((None, ATTN_BLOCK, w), lambda b, j: (b, f(j), 0))
    blk_t = lambda f: pl.BlockSpec((None, 1, kw, ATTN_BLOCK), lambda b, j: (b, f(j), 0, 0))
    cur = lambda j: j
    return pl.pallas_call(
        functools.partial(_attn_band_kernel, nb=nb, pairs=ATTN_PAIRS_X),
        grid=(b, nb),
        in_specs=[
            pl.BlockSpec(memory_space=pltpu.SMEM),
            blk(d, cur),
            pl.BlockSpec((None, l, kw), lambda b, j: (b, 0, 0)), blk(kw, prev), blk(kw, cur), blk(kw, nxt),
            pl.BlockSpec((None,) + vvc.shape[1:], lambda b, j: (b, 0, 0, 0)), blk_t(prev), blk_t(cur), blk_t(nxt),
        ],
        out_specs=blk(d, cur),
        out_shape=jax.ShapeDtypeStruct((b, s, d), BF16),
        compiler_params=_params(2),
        name="attn_latent",
    )(sink, q, kkc, kk, kk, kk, vvc, vv, vv, vv)


def _attn_c(sink, qc, kkc, vvc):
    b, l, d = qc.shape
    return pl.pallas_call(
        functools.partial(_attn_ctx_kernel, pairs=ATTN_PAIRS_C),
        grid=(b,),
        in_specs=[
            pl.BlockSpec(memory_space=pltpu.SMEM),
            pl.BlockSpec((None, l, d), lambda b: (b, 0, 0)),
            pl.BlockSpec((None, l, kkc.shape[2]), lambda b: (b, 0, 0)),
            pl.BlockSpec((None,) + vvc.shape[1:], lambda b: (b, 0, 0, 0)),
        ],
        out_specs=pl.BlockSpec((None, l, d), lambda b: (b, 0, 0)),
        out_shape=jax.ShapeDtypeStruct((b, l, d), BF16),
        compiler_params=_params(1),
        name="attn_context",
    )(sink, qc, kkc, vvc)


def _ffn_tail(x1s, parts, shf_ref, scf_ref, gtf_ref, g2_ref, win_ref, wout_ref, out_ref):
    h2s = [(_rms(x1, g2_ref[...]) * (1.0 + scf_ref[...]) + shf_ref[...]).astype(BF16) for x1 in x1s]
    accs = [None] * len(x1s)
    hidden = {}
    for step in range(len(FF_CHUNKS) + 1):
        if step < len(FF_CHUNKS):
            lo, hi = FF_CHUNKS[step]
            hidden[step] = [(_dot(h2, win_ref[:, lo:hi]), _dot(h2, win_ref[:, D_FF + lo:D_FF + hi])) for h2 in h2s]
        if step > 0:
            lo, hi = FF_CHUNKS[step - 1]
            for i, (gate, up) in enumerate(hidden.pop(step - 1)):
                down = _dot((_silu(gate) * up).astype(BF16), wout_ref[lo:hi, :])
                accs[i] = down if accs[i] is None else accs[i] + down
    for rows, x1, acc in zip(parts, x1s, accs):
        out_ref[rows, :] = x1 + gtf_ref[...] * acc


def _post0_kernel(x_ref, o_ref, gtm_ref, shf_ref, scf_ref, gtf_ref, g2_ref, wo_ref, win_ref, wout_ref, out_ref):
    parts = _row_parts(x_ref, ROW_PARTS)
    mixed = [_dot(o_ref[rows, :], wo_ref[...]) for rows in parts]
    x1s = [x_ref[rows, :] + gtm_ref[...] * a for rows, a in zip(parts, mixed)]
    _ffn_tail(x1s, parts, shf_ref, scf_ref, gtf_ref, g2_ref, win_ref, wout_ref, out_ref)


def _post0(xin, o, mods, per_batch, g2, wo, win, wout, tm):
    b, t, d = xin.shape
    return pl.pallas_call(
        _post0_kernel,
        grid=(b, t // tm),
        in_specs=[
            _tok_spec(tm, d), _tok_spec(tm, d),
            _mod_spec(2, per_batch), _mod_spec(3, per_batch), _mod_spec(4, per_batch), _mod_spec(5, per_batch),
            _const2((1, d)), _const2(wo.shape), _const2(win.shape), _const2(wout.shape),
        ],
        out_specs=_tok_spec(tm, d),
        out_shape=jax.ShapeDtypeStruct((b, t, d), F32),
        compiler_params=_params(2),
        name="attn_out_ffn",
    )(xin, o, mods, mods, mods, mods, g2, wo, win, wout)


def _post1_kernel(x_ref, y_ref, shm_ref, scm_ref, gtm_ref, shf_ref, scf_ref, gtf_ref, g1_ref, g2_ref, gn_ref,
                  wg_ref, wo_ref, win_ref, wout_ref, out_ref):
    parts = _row_parts(x_ref, ROW_PARTS)
    xs = [x_ref[rows, :] for rows in parts]
    h1s = [(_rms(x, g1_ref[...]) * (1.0 + scm_ref[...]) + shm_ref[...]).astype(BF16) for x in xs]
    gates = [_dot(h1, wg_ref[...]) for h1 in h1s]
    gated = []
    for rows, gate in zip(parts, gates):
        normed = []
        for h in range(RET_HEADS):
            o = y_ref[rows, h * RET_V_DIM:(h + 1) * RET_V_DIM].astype(F32)
            d = o - jnp.mean(o, axis=-1, keepdims=True)
            normed.append(d * lax.rsqrt(jnp.mean(d * d, axis=-1, keepdims=True) + EPS))
        gated.append((_silu(gate) * (jnp.concatenate(normed, axis=1) * gn_ref[...])).astype(BF16))
    mixed = [_dot(g, wo_ref[...]) for g in gated]
    x1s = [x + gtm_ref[...] * a for x, a in zip(xs, mixed)]
    _ffn_tail(x1s, parts, shf_ref, scf_ref, gtf_ref, g2_ref, win_ref, wout_ref, out_ref)


def _post1(xin, y, mods, g1, g2, gn, wg, wo, win, wout, tm):
    b, t, d = xin.shape
    return pl.pallas_call(
        _post1_kernel,
        grid=(b, t // tm),
        in_specs=[
            _tok_spec(tm, d), _tok_spec(tm, y.shape[2]),
            *[_mod_spec(k, True) for k in range(6)],
            _const2((1, d)), _const2((1, d)), _const2(gn.shape),
            pl.BlockSpec((d, RET_VWIDTH), lambda b, j: (0, wg.shape[1] // RET_VWIDTH - 1), pipeline_mode=pl.Buffered(1)),
            _const2(wo.shape), _const2(win.shape), _const2(wout.shape),
        ],
        out_specs=_tok_spec(tm, d),
        out_shape=jax.ShapeDtypeStruct((b, t, d), F32),
        compiler_params=_params(2),
        name="ret_out_ffn",
    )(xin, y, mods, mods, mods, mods, mods, mods, g1, g2, gn, wg, wo, win, wout)


def _proj1_kernel(x_ref, sh_ref, sc_ref, g1_ref, w_ref, cos_ref, sin_ref, q_ref, k_ref, v_ref):
    qk_w = 2 * RET_HEADS * RET_QK_DIM
    hb = (_rms(x_ref[...], g1_ref[...]) * (1.0 + sc_ref[...]) + sh_ref[...]).astype(BF16)
    qk = _dot(hb, w_ref[:, :qk_w])
    k_scale = RET_QK_DIM ** -0.5
    for cb in range(qk_w // LANES):
        ts = slice((cb % 2) * LANES, (cb % 2 + 1) * LANES)
        tb = qk[:, cb * LANES:(cb + 1) * LANES]
        r = tb * cos_ref[:, ts] + pltpu.roll(tb, LANES // 2, 1) * sin_ref[:, ts]
        if cb < qk_w // (2 * LANES):
            q_ref[:, cb * LANES:(cb + 1) * LANES] = r.astype(BF16)
        else:
            cc = cb - qk_w // (2 * LANES)
            k_ref[:, cc * LANES:(cc + 1) * LANES] = (r * k_scale).astype(BF16)
    v_ref[...] = _dot(hb, w_ref[:, qk_w:]).astype(BF16)


def _proj1(xin, mods, per_batch, g1, w, cos, sin, tm):
    b, t, d = xin.shape
    qw = RET_HEADS * RET_QK_DIM
    return pl.pallas_call(
        _proj1_kernel,
        grid=(b, t // tm),
        in_specs=[
            _tok_spec(tm, d),
            _mod_spec(0, per_batch), _mod_spec(1, per_batch),
            _const2((1, d)),
            pl.BlockSpec((d, 2 * qw + RET_VWIDTH), lambda b, j: (0, 0), pipeline_mode=pl.Buffered(1)),
            pl.BlockSpec((tm, 2 * LANES), lambda b, j: (j, 0)),
            pl.BlockSpec((tm, 2 * LANES), lambda b, j: (j, 0)),
        ],
        out_specs=[_tok_spec(tm, qw), _tok_spec(tm, qw), _tok_spec(tm, RET_VWIDTH)],
        out_shape=[
            jax.ShapeDtypeStruct((b, t, qw), BF16),
            jax.ShapeDtypeStruct((b, t, qw), BF16),
            jax.ShapeDtypeStruct((b, t, RET_VWIDTH), BF16),
        ],
        compiler_params=_params(2),
        name="ret_qkv_proj",
    )(xin, mods, mods, g1, w, cos, sin)


def _ret_kernel(dl_ref, q_ref, k_ref, v_ref, kc_ref, vc_ref, y_ref,
                o_scr, sf_scr, sb_scr, comb_scr, qdf_scr, qdb_scr, kdf_scr, kdb_scr, cdf_scr, cdb_scr,
                ctxf_scr, ctxb_scr, *, nc):
    hd = pl.program_id(0)
    c_len = RET_CHUNK
    l_ctx = kc_ref.shape[0]

    @pl.when(pl.program_id(1) == 0)
    def _decay_tables():
        def log_decay(direction, shp):
            return -jnp.log(1.0 + jnp.exp(-jnp.full(shp, dl_ref[direction, hd], F32)))

        shape = (c_len, c_len)
        n = lax.broadcasted_iota(jnp.int32, shape, 0).astype(F32)
        m = lax.broadcasted_iota(jnp.int32, shape, 1).astype(F32)
        comb_scr[...] = (jnp.where(n >= m, jnp.exp(log_decay(0, shape) * jnp.maximum(n - m, 0.0)), 0.0)
                         + jnp.where(m >= n, jnp.exp(log_decay(1, shape) * jnp.maximum(m - n, 0.0)), 0.0))
        nk = lax.broadcasted_iota(jnp.int32, (c_len, RET_QK_DIM), 0).astype(F32)
        nv = lax.broadcasted_iota(jnp.int32, (c_len, RET_V_DIM), 0).astype(F32)
        qdf_scr[...] = jnp.exp(log_decay(0, nv.shape) * (nv + 1.0))
        kdf_scr[...] = jnp.exp(log_decay(0, nk.shape) * (c_len - 1.0 - nk))
        cdf_scr[...] = jnp.exp(log_decay(0, (1, RET_V_DIM)) * float(c_len))
        qdb_scr[...] = jnp.exp(log_decay(1, nv.shape) * (c_len - nv))
        kdb_scr[...] = jnp.exp(log_decay(1, nk.shape) * nk)
        cdb_scr[...] = jnp.exp(log_decay(1, (1, RET_V_DIM)) * float(c_len))
        tt = lax.broadcasted_iota(jnp.int32, (l_ctx, RET_QK_DIM), 0).astype(F32)
        ctxf_scr[...] = jnp.exp(log_decay(0, tt.shape) * (l_ctx - 1.0 - tt))
        ctxb_scr[...] = jnp.exp(log_decay(1, tt.shape) * tt)

    kc = kc_ref[...].astype(F32)
    vc = vc_ref[...]
    sf_scr[...] = _dot_tn((kc * ctxf_scr[...]).astype(BF16), vc)
    sb_scr[...] = _dot_tn((kc * ctxb_scr[...]).astype(BF16), vc)

    def rows(c):
        return slice(c * c_len, (c + 1) * c_len)

    def masked_scores(c):
        return (_dot_nt(q_ref[rows(c), :], k_ref[rows(c), :]) * comb_scr[...]).astype(BF16)

    def forward(c, scores):
        qc, kx, vx = q_ref[rows(c), :], k_ref[rows(c), :], v_ref[rows(c), :]
        st = sf_scr[...]
        out = _dot(scores, vx) + _dot(qc, st.astype(BF16)) * qdf_scr[...]
        sf_scr[...] = st * cdf_scr[...] + _dot_tn((kx.astype(F32) * kdf_scr[...]).astype(BF16), vx)
        return out

    def backward(c):
        qc, kx, vx = q_ref[rows(c), :], k_ref[rows(c), :], v_ref[rows(c), :]
        st = sb_scr[...]
        out = _dot(qc, st.astype(BF16)) * qdb_scr[...]
        sb_scr[...] = st * cdb_scr[...] + _dot_tn((kx.astype(F32) * kdb_scr[...]).astype(BF16), vx)
        return out

    scores = {}
    for step in range(nc + RET_LOOKAHEAD):
        if step < nc:
            scores[step] = masked_scores(step)
        i = step - RET_LOOKAHEAD
        if i < 0:
            continue
        for c, out in ((i, forward(i, scores.pop(i))), (nc - 1 - i, backward(nc - 1 - i))):
            if i < nc // 2:
                o_scr[rows(c), :] = out
            else:
                y_ref[rows(c), :] = (o_scr[rows(c), :] + out).astype(BF16)


def _retention(dl, q, k, v, kc, vc):
    b, s, _ = q.shape
    l = kc.shape[1]
    nc = s // RET_CHUNK
    assert nc % 2 == 0
    hspec = lambda t, w: pl.BlockSpec((None, t, w), lambda h, b: (b, 0, h))
    return pl.pallas_call(
        functools.partial(_ret_kernel, nc=nc),
        grid=(RET_HEADS, b),
        in_specs=[
            pl.BlockSpec(memory_space=pltpu.SMEM),
            hspec(s, RET_QK_DIM), hspec(s, RET_QK_DIM), hspec(s, RET_V_DIM),
            hspec(l, RET_QK_DIM), hspec(l, RET_V_DIM),
        ],
        out_specs=hspec(s, RET_V_DIM),
        out_shape=jax.ShapeDtypeStruct((b, s, RET_VWIDTH), BF16),
        scratch_shapes=[
            pltpu.VMEM((s, RET_V_DIM), F32),
            pltpu.VMEM((RET_QK_DIM, RET_V_DIM), F32),
            pltpu.VMEM((RET_QK_DIM, RET_V_DIM), F32),
            pltpu.VMEM((RET_CHUNK, RET_CHUNK), F32),
            pltpu.VMEM((RET_CHUNK, RET_V_DIM), F32),
            pltpu.VMEM((RET_CHUNK, RET_V_DIM), F32),
            pltpu.VMEM((RET_CHUNK, RET_QK_DIM), F32),
            pltpu.VMEM((RET_CHUNK, RET_QK_DIM), F32),
            pltpu.VMEM((1, RET_V_DIM), F32),
            pltpu.VMEM((1, RET_V_DIM), F32),
            pltpu.VMEM((l, RET_QK_DIM), F32),
            pltpu.VMEM((l, RET_QK_DIM), F32),
        ],
        compiler_params=_params(2),
        name="retention",
    )(dl, q, k, v, kc, vc)


def _rope_tables(n, head_dim):
    rows = n // GRID_W
    row = jnp.broadcast_to(jnp.arange(rows, dtype=jnp.int32)[:, None], (rows, GRID_W)).reshape(n)
    col = jnp.broadcast_to(jnp.arange(GRID_W, dtype=jnp.int32)[None, :], (rows, GRID_W)).reshape(n)
    axis_dim = head_dim // 2
    inv = ROPE_BASE ** (-jnp.arange(0, axis_dim, 2, dtype=F32) / axis_dim)
    ang_r = row.astype(F32)[:, None] * inv
    ang_c = col.astype(F32)[:, None] * inv
    cos = jnp.concatenate([jnp.cos(ang_r)] * 2 + [jnp.cos(ang_c)] * 2, axis=-1)
    sin = jnp.concatenate([-jnp.sin(ang_r), jnp.sin(ang_r), -jnp.sin(ang_c), jnp.sin(ang_c)], axis=-1)
    return cos, sin


def kernel(x, c, ctx, c_ctx, ada_w, ada_b, norm1_g, norm2_g, ffn_w_in, ffn_w_out, attn_w_qkv, attn_q_norm,
           attn_k_norm, attn_sink, attn_w_o, ret_w_qkvg, ret_decay_logit, ret_gn_g, ret_w_o):
    b, s, d = x.shape
    l = ctx.shape[1]
    qd = N_HEADS * HEAD_DIM
    kvd = N_KV_HEADS * HEAD_DIM
    tm_x = min(TOKEN_TILE, s)
    tm_c = min(TOKEN_TILE, l)

    rows = -(-(b + 1) // 8) * 8
    cin = jnp.concatenate([c, c_ctx[None, :], jnp.zeros((rows - b - 1, d), F32)], axis=0)
    mods = _mods(cin, ada_w, ada_b)
    mx0 = mods[0, :b].reshape(b, 1, 6 * d)
    mc0 = mods[0, b:b + 1].reshape(1, 1, 6 * d)
    mx1 = mods[1, :b].reshape(b, 1, 6 * d)
    mc1 = mods[1, b:b + 1].reshape(1, 1, 6 * d)

    w_qkv = attn_w_qkv[0].astype(BF16)
    scale = HEAD_DIM ** -0.5 * LOG2E
    gain = jnp.concatenate([jnp.tile(attn_q_norm[0] * scale, N_HEADS), jnp.tile(attn_k_norm[0], N_KV_HEADS)])[None, :]
    n_norm_heads = N_HEADS + N_KV_HEADS
    head_of = jnp.arange(qd + kvd) // HEAD_DIM
    gsum = ((head_of[:, None] == jnp.arange(LANES)[None, :]).astype(F32) / HEAD_DIM).astype(BF16)
    gbc_half = (jnp.arange(LANES)[:, None] == head_of[None, :]).astype(BF16)
    gbc = jnp.concatenate([gbc_half, gbc_half], axis=0)
    del n_norm_heads
    cos64, sin64 = _rope_tables(s, HEAD_DIM)
    cos_x = jnp.tile(cos64, (1, LANES // HEAD_DIM))
    sin_x = jnp.tile(sin64, (1, LANES // HEAD_DIM))
    cos_c = jnp.ones((l, LANES), F32)
    sin_c = jnp.zeros((l, LANES), F32)
    g1 = norm1_g[0][None, :]
    g2 = norm2_g[0][None, :]
    q_x, kk_x, vv_x = _qkv0(x, mx0, True, g1, w_qkv, gain, gsum, gbc, cos_x, sin_x, tm_x)
    q_c, kk_c, vv_c = _qkv0(ctx, mc0, False, g1, w_qkv, gain, gsum, gbc, cos_c, sin_c, tm_c)
    sink = attn_sink[0].astype(F32)
    o_x = _attn_x(sink, q_x, kk_x, vv_x, kk_c, vv_c)
    o_c = _attn_c(sink, q_c, kk_c, vv_c)
    w_o = attn_w_o[0].astype(BF16)
    w_in0 = ffn_w_in[0].astype(BF16)
    w_out0 = ffn_w_out[0].astype(BF16)
    x1 = _post0(x, o_x, mx0, True, g2, w_o, w_in0, w_out0, tm_x)
    y_ctx = _post0(ctx, o_c, mc0, False, g2, w_o, w_in0, w_out0, tm_c)

    w_qkvg = ret_w_qkvg[0].astype(BF16)
    cos256, sin256 = _rope_tables(s, RET_QK_DIM)
    g1 = norm1_g[1][None, :]
    g2 = norm2_g[1][None, :]
    q1, k1, v1 = _proj1(x1, mx1, True, g1, w_qkvg, cos256, sin256, tm_x)
    ones_c = jnp.ones((l, 2 * LANES), F32)
    _, k1c, v1c = _proj1(y_ctx, mc1, False, g1, w_qkvg, ones_c, jnp.zeros_like(ones_c), tm_c)
    y = _retention(ret_decay_logit[0].astype(F32), q1, k1, v1, k1c, v1c)
    return _post1(x1, y, mx1, g1, g2, ret_gn_g[0][None, :], w_qkvg, ret_w_o[0].astype(BF16),
                  ffn_w_in[1].astype(BF16), ffn_w_out[1].astype(BF16), tm_x)
```

```python
import functools

import jax
import jax.numpy as jnp
from jax import lax
from jax.experimental import pallas as pl
from jax.experimental.pallas import tpu as pltpu

F32 = jnp.float32
BF16 = jnp.bfloat16

D_MODEL = 1024
GRID_W = 64
HEAD_DIM = 64
N_HEADS = D_MODEL // HEAD_DIM
N_KV_HEADS = N_HEADS // 4
WINDOW = 128
ATTN_BLOCK = 128
RET_HEADS = 4
RET_QK_DIM = 256
RET_V_DIM = 512
RET_VWIDTH = 2 * D_MODEL
RET_CHUNK = 256
ATTN_PAIRS_X = 2
ATTN_PAIRS_C = 2
ATTN_LOOKAHEAD = 2
RET_LOOKAHEAD = 2
D_FF = 2816
MXU_TILE = 256
FF_CHUNKS = ((0, 6 * MXU_TILE), (6 * MXU_TILE, D_FF))
ROW_PARTS = 2
ROPE_BASE = 10000.0
EPS = 1e-6
NEG_INF = -1e30
LOG2E = 1.4426950408889634
LANES = 128

VMEM_LIMIT = 56 * 1024 * 1024
TOKEN_TILE = 512


def _dot(a, b):
    return jnp.dot(a, b, preferred_element_type=F32)


def _dot_nt(a, b):
    return lax.dot_general(a, b, (((1,), (1,)), ((), ())), preferred_element_type=F32)


def _dot_tn(a, b):
    return lax.dot_general(a, b, (((0,), (0,)), ((), ())), preferred_element_type=F32)


def _rms(x, g):
    return x * lax.rsqrt(jnp.mean(x * x, axis=-1, keepdims=True) + EPS) * g


def _silu(x):
    return x * jax.nn.sigmoid(x)


def _params(n_axes):
    return pltpu.CompilerParams(dimension_semantics=("arbitrary",) * n_axes, vmem_limit_bytes=VMEM_LIMIT)


def _const2(shape):
    return pl.BlockSpec(shape, lambda b, j: (0,) * len(shape), pipeline_mode=pl.Buffered(1))


def _mod_spec(k, per_batch):
    if per_batch:
        return pl.BlockSpec((None, 1, D_MODEL), lambda b, j, k=k: (b, 0, k))
    return pl.BlockSpec((None, 1, D_MODEL), lambda b, j, k=k: (0, 0, k))


def _tok_spec(tm, width):
    return pl.BlockSpec((None, tm, width), lambda b, j: (b, j, 0))


def _row_parts(ref, n):
    tm = ref.shape[0]
    n = n if tm % (n * ATTN_BLOCK) == 0 else 1
    return [slice(i * tm // n, (i + 1) * tm // n) for i in range(n)]


def _mods_kernel(c_ref, w_ref, b_ref, o_ref):
    a = _silu(c_ref[...]).astype(BF16)
    o_ref[...] = _dot(a, w_ref[...].astype(BF16)) + b_ref[...]


def _mods(cin, ada_w, ada_b):
    depth, d, n = ada_w.shape
    rows = cin.shape[0]
    tn = 1536
    return pl.pallas_call(
        _mods_kernel,
        grid=(depth, n // tn),
        in_specs=[
            pl.BlockSpec((rows, d), lambda i, j: (0, 0)),
            pl.BlockSpec((None, d, tn), lambda i, j: (i, 0, j)),
            pl.BlockSpec((None, 1, tn), lambda i, j: (i, 0, j)),
        ],
        out_specs=pl.BlockSpec((None, rows, tn), lambda i, j: (i, 0, j)),
        out_shape=jax.ShapeDtypeStruct((depth, rows, n), F32),
        compiler_params=_params(2),
        name="adaln_mods",
    )(cin, ada_w, ada_b.reshape(depth, 1, n))


def _dup_halves(r):
    lane = lax.broadcasted_iota(jnp.int32, r.shape, 1)
    lo = lane < HEAD_DIM
    r64 = pltpu.roll(r, HEAD_DIM, 1)
    return jnp.where(lo, r, r64), jnp.where(lo, r64, r)


def _qkv0_kernel(x_ref, sh_ref, sc_ref, g1_ref, w_ref, gain_ref, gsum_ref, gbc_ref, cos_ref, sin_ref,
                 q_ref, kk_ref, vt_ref):
    qd = N_HEADS * HEAD_DIM
    qkd = qd + N_KV_HEADS * HEAD_DIM
    parts = _row_parts(x_ref, ROW_PARTS)
    hs = [_rms(x_ref[rows, :], g1_ref[...]) * (1.0 + sc_ref[...]) + sh_ref[...] for rows in parts]
    qkvs = [_dot(h.astype(BF16), w_ref[...]) for h in hs]
    mss = [_dot((qkv[:, :qkd] * qkv[:, :qkd]).astype(BF16), gsum_ref[...]) for qkv in qkvs]
    invfs = []
    for ms in mss:
        inv = lax.rsqrt(ms + EPS)
        hi = inv.astype(BF16)
        lo = (inv - hi.astype(F32)).astype(BF16)
        invfs.append(_dot(jnp.concatenate([hi, lo], axis=1), gbc_ref[...]))
    row = lax.broadcasted_iota(jnp.int32, (LANES - HEAD_DIM, ATTN_BLOCK), 0)
    tail = jnp.where(row == 0, 1.0, 0.0).astype(BF16)
    for rows, qkv, invf in zip(parts, qkvs, invfs):
        t = qkv[:, :qkd] * gain_ref[...]
        cos = cos_ref[rows, :]
        sin = sin_ref[rows, :]
        lane = lax.broadcasted_iota(jnp.int32, cos.shape, 1)
        first = (lane % 32) < 16
        for cb in range(qkd // LANES):
            sl = slice(cb * LANES, (cb + 1) * LANES)
            tb = t[:, sl]
            sw = jnp.where(first, pltpu.roll(tb, LANES - 16, 1), pltpu.roll(tb, 16, 1))
            r = (tb * cos + sw * sin) * invf[:, sl]
            if cb < qd // LANES:
                q_ref[rows, sl] = r.astype(BF16)
            else:
                p = cb - qd // LANES
                a, b = _dup_halves(r)
                kk_ref[rows, (2 * p) * LANES:(2 * p + 1) * LANES] = a.astype(BF16)
                kk_ref[rows, (2 * p + 1) * LANES:(2 * p + 2) * LANES] = b.astype(BF16)
        vt = qkv[:, qkd:].T
        for blk in range(vt.shape[1] // ATTN_BLOCK):
            cols = slice(blk * ATTN_BLOCK, (blk + 1) * ATTN_BLOCK)
            slab = rows.start // ATTN_BLOCK + blk
            for g in range(N_KV_HEADS):
                vt_ref[slab, g * LANES:g * LANES + HEAD_DIM, :] = vt[g * HEAD_DIM:(g + 1) * HEAD_DIM, cols].astype(BF16)
                vt_ref[slab, g * LANES + HEAD_DIM:(g + 1) * LANES, :] = tail


def _qkv0(xin, mods, per_batch, g1, w, gain, gsum, gbc, cos, sin, tm):
    b, t, d = xin.shape
    kw = N_KV_HEADS * LANES
    return pl.pallas_call(
        _qkv0_kernel,
        grid=(b, t // tm),
        in_specs=[
            _tok_spec(tm, d),
            _mod_spec(0, per_batch),
            _mod_spec(1, per_batch),
            _const2((1, d)),
            _const2(w.shape),
            _const2(gain.shape),
            _const2(gsum.shape),
            _const2(gbc.shape),
            pl.BlockSpec((tm, LANES), lambda b, j: (j, 0)),
            pl.BlockSpec((tm, LANES), lambda b, j: (j, 0)),
        ],
        out_specs=[_tok_spec(tm, d), _tok_spec(tm, kw),
                   pl.BlockSpec((None, tm // ATTN_BLOCK, kw, ATTN_BLOCK), lambda b, j: (b, j, 0, 0))],
        out_shape=[
            jax.ShapeDtypeStruct((b, t, d), BF16),
            jax.ShapeDtypeStruct((b, t, kw), BF16),
            jax.ShapeDtypeStruct((b, t // ATTN_BLOCK, kw, ATTN_BLOCK), BF16),
        ],
        compiler_params=_params(2),
        name="attn_qkv_proj",
    )(xin, mods, mods, g1, w, gain, gsum, gbc, cos, sin)


def _band_masks(j, nb, heads):
    shape = (ATTN_BLOCK, heads * ATTN_BLOCK)
    key = lax.broadcasted_iota(jnp.int32, shape, 0)
    qry = lax.broadcasted_iota(jnp.int32, shape, 1) % ATTN_BLOCK
    return (key >= qry) & (j > 0), (key <= qry) & (j < nb - 1)


def _attn_scores(sink_ref, q_ref, key_refs, pr, pairs, masks):
    tq = q_ref.shape[0]
    g = pr // (N_HEADS // N_KV_HEADS // 2)
    gs = slice(g * LANES, (g + 1) * LANES)
    lo = lax.broadcasted_iota(jnp.int32, (tq, LANES), 1) < HEAD_DIM
    zero = jnp.zeros((tq, LANES), BF16)
    stack = []
    for p in range(pr, pr + pairs):
        qb = q_ref[:, p * LANES:(p + 1) * LANES]
        stack += [jnp.where(lo, qb, zero), jnp.where(lo, zero, qb)]
    keys = jnp.concatenate([r[:, gs] for r in key_refs], axis=0) if len(key_refs) > 1 else key_refs[0][:, gs]
    st = _dot_nt(keys, jnp.concatenate(stack, axis=0))
    parts = [st[i * LANES:(i + 1) * LANES, :] for i in range(st.shape[0] // LANES)]
    if masks is not None:
        ia = key_refs[0].shape[0] // LANES
        parts[ia] = jnp.where(masks[0], parts[ia], NEG_INF)
        parts[ia + 2] = jnp.where(masks[1], parts[ia + 2], NEG_INF)
    sink = jnp.concatenate([jnp.full((1, tq), sink_ref[2 * pr + i] * LOG2E, F32) for i in range(2 * pairs)], axis=1)
    mx = parts[0]
    for part in parts[1:]:
        mx = jnp.maximum(mx, part)
    m = jnp.maximum(jnp.max(mx, axis=0, keepdims=True), sink)
    return jnp.concatenate([jnp.exp2(part - m).astype(BF16) for part in parts], axis=0), jnp.exp2(sink - m)


def _attn_values(probs, val_refs, o_ref, pr, pairs):
    pt, p_sink = probs
    tq = o_ref.shape[0]
    g = pr // (N_HEADS // N_KV_HEADS // 2)
    gs = slice(g * LANES, (g + 1) * LANES)
    vt = jnp.concatenate([r[i, gs, :] for r in val_refs for i in range(r.shape[0])], axis=1)
    ot = _dot(vt, pt)
    den = jnp.sum(ot[HEAD_DIM:HEAD_DIM + 8, :], axis=0, keepdims=True) + p_sink
    out = ot[:HEAD_DIM, :] / den
    for i in range(pairs):
        pair = jnp.concatenate([out[:, 2 * i * tq:(2 * i + 1) * tq], out[:, (2 * i + 1) * tq:(2 * i + 2) * tq]], axis=0)
        o_ref[:, (pr + i) * LANES:(pr + i + 1) * LANES] = pair.T.astype(BF16)


def _attn_all_heads(sink_ref, q_ref, key_refs, val_refs, o_ref, pairs, masks):
    groups = list(range(0, N_HEADS // 2, pairs))
    probs = {}
    for step in range(len(groups) + ATTN_LOOKAHEAD):
        if step < len(groups):
            probs[step] = _attn_scores(sink_ref, q_ref, key_refs, groups[step], pairs, masks)
        done = step - ATTN_LOOKAHEAD
        if done >= 0:
            _attn_values(probs.pop(done), val_refs, o_ref, groups[done], pairs)


def _attn_ctx_kernel(sink_ref, q_ref, kc_ref, vc_ref, o_ref, *, pairs):
    _attn_all_heads(sink_ref, q_ref, [kc_ref], [vc_ref], o_ref, pairs, None)


def _attn_band_kernel(sink_ref, q_ref, kc_ref, k_ref, vc_ref, vt_ref, o_ref, *, nb, pairs):
    j = pl.program_id(1)
    masks = _band_masks(j, nb, 2 * pairs)
    band = [jnp.maximum(j - 1, 0), j, jnp.minimum(j + 1, nb - 1)]
    key_refs = [kc_ref] + [k_ref.at[pl.ds(pl.multiple_of(blk * ATTN_BLOCK, ATTN_BLOCK), ATTN_BLOCK), :]
                           for blk in band]
    val_refs = [vc_ref] + [vt_ref.at[pl.ds(blk, 1)] for blk in band]
    _attn_all_heads(sink_ref, q_ref, key_refs, val_refs, o_ref, pairs, masks)


def _attn_x(sink, q, kk, vv, kkc, vvc):
    b, s, d = q.shape
    nb = s // ATTN_BLOCK
    per_batch = lambda a: pl.BlockSpec((None,) + a.shape[1:], lambda b, j: (b,) + (0,) * (a.ndim - 1))
    blk = pl.BlockSpec((None, ATTN_BLOCK, d), lambda b, j: (b, j, 0))
    return pl.pallas_call(
        functools.partial(_attn_band_kernel, nb=nb, pairs=ATTN_PAIRS_X),
        grid=(b, nb),
        in_specs=[pl.BlockSpec(memory_space=pltpu.SMEM), blk,
                  per_batch(kkc), per_batch(kk), per_batch(vvc), per_batch(vv)],
        out_specs=blk,
        out_shape=jax.ShapeDtypeStruct((b, s, d), BF16),
        compiler_params=_params(2),
        name="attn_latent",
    )(sink, q, kkc, kk, vvc, vv)


def _attn_c(sink, qc, kkc, vvc):
    b, l, d = qc.shape
    return pl.pallas_call(
        functools.partial(_attn_ctx_kernel, pairs=ATTN_PAIRS_C),
        grid=(b,),
        in_specs=[
            pl.BlockSpec(memory_space=pltpu.SMEM),
            pl.BlockSpec((None, l, d), lambda b: (b, 0, 0)),
            pl.BlockSpec((None, l, kkc.shape[2]), lambda b: (b, 0, 0)),
            pl.BlockSpec((None,) + vvc.shape[1:], lambda b: (b, 0, 0, 0)),
        ],
        out_specs=pl.BlockSpec((None, l, d), lambda b: (b, 0, 0)),
        out_shape=jax.ShapeDtypeStruct((b, l, d), BF16),
        compiler_params=_params(1),
        name="attn_context",
    )(sink, qc, kkc, vvc)


def _ffn_tail(x1s, parts, shf_ref, scf_ref, gtf_ref, g2_ref, win_ref, wout_ref, out_ref):
    h2s = [(_rms(x1, g2_ref[...]) * (1.0 + scf_ref[...]) + shf_ref[...]).astype(BF16) for x1 in x1s]
    accs = [None] * len(x1s)
    hidden = {}
    for step in range(len(FF_CHUNKS) + 1):
        if step < len(FF_CHUNKS):
            lo, hi = FF_CHUNKS[step]
            hidden[step] = [(_dot(h2, win_ref[:, lo:hi]), _dot(h2, win_ref[:, D_FF + lo:D_FF + hi])) for h2 in h2s]
        if step > 0:
            lo, hi = FF_CHUNKS[step - 1]
            for i, (gate, up) in enumerate(hidden.pop(step - 1)):
                down = _dot((_silu(gate) * up).astype(BF16), wout_ref[lo:hi, :])
                accs[i] = down if accs[i] is None else accs[i] + down
    for rows, x1, acc in zip(parts, x1s, accs):
        out_ref[rows, :] = x1 + gtf_ref[...] * acc


def _post0_kernel(x_ref, o_ref, gtm_ref, shf_ref, scf_ref, gtf_ref, g2_ref, wo_ref, win_ref, wout_ref, out_ref):
    parts = _row_parts(x_ref, ROW_PARTS)
    mixed = [_dot(o_ref[rows, :], wo_ref[...]) for rows in parts]
    x1s = [x_ref[rows, :] + gtm_ref[...] * a for rows, a in zip(parts, mixed)]
    _ffn_tail(x1s, parts, shf_ref, scf_ref, gtf_ref, g2_ref, win_ref, wout_ref, out_ref)


def _post0(xin, o, mods, per_batch, g2, wo, win, wout, tm):
    b, t, d = xin.shape
    return pl.pallas_call(
        _post0_kernel,
        grid=(b, t // tm),
        in_specs=[
            _tok_spec(tm, d), _tok_spec(tm, d),
            _mod_spec(2, per_batch), _mod_spec(3, per_batch), _mod_spec(4, per_batch), _mod_spec(5, per_batch),
            _const2((1, d)), _const2(wo.shape), _const2(win.shape), _const2(wout.shape),
        ],
        out_specs=_tok_spec(tm, d),
        out_shape=jax.ShapeDtypeStruct((b, t, d), F32),
        compiler_params=_params(2),
        name="attn_out_ffn",
    )(xin, o, mods, mods, mods, mods, g2, wo, win, wout)


def _post1_kernel(x_ref, y_ref, shm_ref, scm_ref, gtm_ref, shf_ref, scf_ref, gtf_ref, g1_ref, g2_ref, gn_ref,
                  wg_ref, wo_ref, win_ref, wout_ref, out_ref):
    parts = _row_parts(x_ref, ROW_PARTS)
    xs = [x_ref[rows, :] for rows in parts]
    h1s = [(_rms(x, g1_ref[...]) * (1.0 + scm_ref[...]) + shm_ref[...]).astype(BF16) for x in xs]
    gates = [_dot(h1, wg_ref[...]) for h1 in h1s]
    gated = []
    for rows, gate in zip(parts, gates):
        normed = []
        for h in range(RET_HEADS):
            o = y_ref[rows, h * RET_V_DIM:(h + 1) * RET_V_DIM].astype(F32)
            d = o - jnp.mean(o, axis=-1, keepdims=True)
            normed.append(d * lax.rsqrt(jnp.mean(d * d, axis=-1, keepdims=True) + EPS))
        gated.append((_silu(gate) * (jnp.concatenate(normed, axis=1) * gn_ref[...])).astype(BF16))
    mixed = [_dot(g, wo_ref[...]) for g in gated]
    x1s = [x + gtm_ref[...] * a for x, a in zip(xs, mixed)]
    _ffn_tail(x1s, parts, shf_ref, scf_ref, gtf_ref, g2_ref, win_ref, wout_ref, out_ref)


def _post1(xin, y, mods, g1, g2, gn, wg, wo, win, wout, tm):
    b, t, d = xin.shape
    return pl.pallas_call(
        _post1_kernel,
        grid=(b, t // tm),
        in_specs=[
            _tok_spec(tm, d), _tok_spec(tm, y.shape[2]),
            *[_mod_spec(k, True) for k in range(6)],
            _const2((1, d)), _const2((1, d)), _const2(gn.shape),
            pl.BlockSpec((d, RET_VWIDTH), lambda b, j: (0, wg.shape[1] // RET_VWIDTH - 1), pipeline_mode=pl.Buffered(1)),
            _const2(wo.shape), _const2(win.shape), _const2(wout.shape),
        ],
        out_specs=_tok_spec(tm, d),
        out_shape=jax.ShapeDtypeStruct((b, t, d), F32),
        compiler_params=_params(2),
        name="ret_out_ffn",
    )(xin, y, mods, mods, mods, mods, mods, mods, g1, g2, gn, wg, wo, win, wout)


def _proj1_kernel(x_ref, sh_ref, sc_ref, g1_ref, w_ref, cos_ref, sin_ref, q_ref, k_ref, v_ref):
    qk_w = 2 * RET_HEADS * RET_QK_DIM
    hb = (_rms(x_ref[...], g1_ref[...]) * (1.0 + sc_ref[...]) + sh_ref[...]).astype(BF16)
    qk = _dot(hb, w_ref[:, :qk_w])
    k_scale = RET_QK_DIM ** -0.5
    for cb in range(qk_w // LANES):
        ts = slice((cb % 2) * LANES, (cb % 2 + 1) * LANES)
        tb = qk[:, cb * LANES:(cb + 1) * LANES]
        r = tb * cos_ref[:, ts] + pltpu.roll(tb, LANES // 2, 1) * sin_ref[:, ts]
        if cb < qk_w // (2 * LANES):
            q_ref[:, cb * LANES:(cb + 1) * LANES] = r.astype(BF16)
        else:
            cc = cb - qk_w // (2 * LANES)
            k_ref[:, cc * LANES:(cc + 1) * LANES] = (r * k_scale).astype(BF16)
    v_ref[...] = _dot(hb, w_ref[:, qk_w:]).astype(BF16)


def _proj1_ctx_kernel(x_ref, sh_ref, sc_ref, g1_ref, w_ref, k_ref, v_ref):
    qw = RET_HEADS * RET_QK_DIM
    hb = (_rms(x_ref[...], g1_ref[...]) * (1.0 + sc_ref[...]) + sh_ref[...]).astype(BF16)
    kv = _dot(hb, w_ref[:, qw:])
    k_ref[...] = (kv[:, :qw] * RET_QK_DIM ** -0.5).astype(BF16)
    v_ref[...] = kv[:, qw:].astype(BF16)


def _proj1(xin, mods, per_batch, g1, w, tables, tm):
    b, t, d = xin.shape
    qw = RET_HEADS * RET_QK_DIM
    specs = [
        _tok_spec(tm, d),
        _mod_spec(0, per_batch), _mod_spec(1, per_batch),
        _const2((1, d)),
        pl.BlockSpec((d, 2 * qw + RET_VWIDTH), lambda b, j: (0, 0), pipeline_mode=pl.Buffered(1)),
    ]
    outs = [(qw, _tok_spec(tm, qw)), (RET_VWIDTH, _tok_spec(tm, RET_VWIDTH))]
    if tables is not None:
        specs += [pl.BlockSpec((tm, 2 * LANES), lambda b, j: (j, 0))] * 2
        outs = [(qw, _tok_spec(tm, qw))] + outs
    return pl.pallas_call(
        _proj1_ctx_kernel if tables is None else _proj1_kernel,
        grid=(b, t // tm),
        in_specs=specs,
        out_specs=[spec for _, spec in outs],
        out_shape=[jax.ShapeDtypeStruct((b, t, width), BF16) for width, _ in outs],
        compiler_params=_params(2),
        name="ret_qkv_proj",
    )(xin, mods, mods, g1, w, *(tables or ()))


def _ret_kernel(dl_ref, q_ref, k_ref, v_ref, kc_ref, vc_ref, y_ref,
                o_scr, sf_scr, sb_scr, comb_scr, qdf_scr, qdb_scr, kdf_scr, kdb_scr, cdf_scr, cdb_scr,
                ctxf_scr, ctxb_scr, *, nc):
    hd = pl.program_id(0)
    c_len = RET_CHUNK
    l_ctx = kc_ref.shape[0]

    @pl.when(pl.program_id(1) == 0)
    def _decay_tables():
        def log_decay(direction, shp):
            logit = jnp.full(shp, dl_ref[direction, hd], F32)
            return jnp.minimum(logit, 0.0) - jnp.log1p(jnp.exp(-jnp.abs(logit)))

        shape = (c_len, c_len)
        n = lax.broadcasted_iota(jnp.int32, shape, 0).astype(F32)
        m = lax.broadcasted_iota(jnp.int32, shape, 1).astype(F32)
        comb_scr[...] = (jnp.where(n >= m, jnp.exp(log_decay(0, shape) * jnp.maximum(n - m, 0.0)), 0.0)
                         + jnp.where(m >= n, jnp.exp(log_decay(1, shape) * jnp.maximum(m - n, 0.0)), 0.0))
        nk = lax.broadcasted_iota(jnp.int32, (c_len, RET_QK_DIM), 0).astype(F32)
        nv = lax.broadcasted_iota(jnp.int32, (c_len, RET_V_DIM), 0).astype(F32)
        qdf_scr[...] = jnp.exp(log_decay(0, nv.shape) * (nv + 1.0))
        kdf_scr[...] = jnp.exp(log_decay(0, nk.shape) * (c_len - 1.0 - nk))
        cdf_scr[...] = jnp.exp(log_decay(0, (1, RET_V_DIM)) * float(c_len))
        qdb_scr[...] = jnp.exp(log_decay(1, nv.shape) * (c_len - nv))
        kdb_scr[...] = jnp.exp(log_decay(1, nk.shape) * nk)
        cdb_scr[...] = jnp.exp(log_decay(1, (1, RET_V_DIM)) * float(c_len))
        tt = lax.broadcasted_iota(jnp.int32, (l_ctx, RET_QK_DIM), 0).astype(F32)
        ctxf_scr[...] = jnp.exp(log_decay(0, tt.shape) * (l_ctx - 1.0 - tt))
        ctxb_scr[...] = jnp.exp(log_decay(1, tt.shape) * tt)

    kc = kc_ref[...].astype(F32)
    vc = vc_ref[...]
    sf_scr[...] = _dot_tn((kc * ctxf_scr[...]).astype(BF16), vc)
    sb_scr[...] = _dot_tn((kc * ctxb_scr[...]).astype(BF16), vc)

    def rows(c):
        return slice(c * c_len, (c + 1) * c_len)

    def masked_scores(c):
        return (_dot_nt(q_ref[rows(c), :], k_ref[rows(c), :]) * comb_scr[...]).astype(BF16)

    def forward(c, scores):
        qc, kx, vx = q_ref[rows(c), :], k_ref[rows(c), :], v_ref[rows(c), :]
        st = sf_scr[...]
        out = _dot(scores, vx) + _dot(qc, st.astype(BF16)) * qdf_scr[...]
        sf_scr[...] = st * cdf_scr[...] + _dot_tn((kx.astype(F32) * kdf_scr[...]).astype(BF16), vx)
        return out

    def backward(c):
        qc, kx, vx = q_ref[rows(c), :], k_ref[rows(c), :], v_ref[rows(c), :]
        st = sb_scr[...]
        out = _dot(qc, st.astype(BF16)) * qdb_scr[...]
        sb_scr[...] = st * cdb_scr[...] + _dot_tn((kx.astype(F32) * kdb_scr[...]).astype(BF16), vx)
        return out

    scores = {}
    for step in range(nc + RET_LOOKAHEAD):
        if step < nc:
            scores[step] = masked_scores(step)
        i = step - RET_LOOKAHEAD
        if i < 0:
            continue
        for c, out in ((i, forward(i, scores.pop(i))), (nc - 1 - i, backward(nc - 1 - i))):
            if i < nc // 2:
                o_scr[rows(c), :] = out
            else:
                y_ref[rows(c), :] = (o_scr[rows(c), :] + out).astype(BF16)


def _retention(dl, q, k, v, kc, vc):
    b, s, _ = q.shape
    l = kc.shape[1]
    nc = s // RET_CHUNK
    assert nc % 2 == 0
    hspec = lambda t, w: pl.BlockSpec((None, t, w), lambda h, b: (b, 0, h))
    return pl.pallas_call(
        functools.partial(_ret_kernel, nc=nc),
        grid=(RET_HEADS, b),
        in_specs=[
            pl.BlockSpec(memory_space=pltpu.SMEM),
            hspec(s, RET_QK_DIM), hspec(s, RET_QK_DIM), hspec(s, RET_V_DIM),
            hspec(l, RET_QK_DIM), hspec(l, RET_V_DIM),
        ],
        out_specs=hspec(s, RET_V_DIM),
        out_shape=jax.ShapeDtypeStruct((b, s, RET_VWIDTH), BF16),
        scratch_shapes=[
            pltpu.VMEM((s, RET_V_DIM), F32),
            pltpu.VMEM((RET_QK_DIM, RET_V_DIM), F32),
            pltpu.VMEM((RET_QK_DIM, RET_V_DIM), F32),
            pltpu.VMEM((RET_CHUNK, RET_CHUNK), F32),
            pltpu.VMEM((RET_CHUNK, RET_V_DIM), F32),
            pltpu.VMEM((RET_CHUNK, RET_V_DIM), F32),
            pltpu.VMEM((RET_CHUNK, RET_QK_DIM), F32),
            pltpu.VMEM((RET_CHUNK, RET_QK_DIM), F32),
            pltpu.VMEM((1, RET_V_DIM), F32),
            pltpu.VMEM((1, RET_V_DIM), F32),
            pltpu.VMEM((l, RET_QK_DIM), F32),
            pltpu.VMEM((l, RET_QK_DIM), F32),
        ],
        compiler_params=_params(2),
        name="retention",
    )(dl, q, k, v, kc, vc)


def _rope_tables(n, head_dim):
    rows = n // GRID_W
    row = jnp.broadcast_to(jnp.arange(rows, dtype=jnp.int32)[:, None], (rows, GRID_W)).reshape(n)
    col = jnp.broadcast_to(jnp.arange(GRID_W, dtype=jnp.int32)[None, :], (rows, GRID_W)).reshape(n)
    axis_dim = head_dim // 2
    inv = ROPE_BASE ** (-jnp.arange(0, axis_dim, 2, dtype=F32) / axis_dim)
    ang_r = row.astype(F32)[:, None] * inv
    ang_c = col.astype(F32)[:, None] * inv
    cos = jnp.concatenate([jnp.cos(ang_r)] * 2 + [jnp.cos(ang_c)] * 2, axis=-1)
    sin = jnp.concatenate([-jnp.sin(ang_r), jnp.sin(ang_r), -jnp.sin(ang_c), jnp.sin(ang_c)], axis=-1)
    return cos, sin


def kernel(x, c, ctx, c_ctx, ada_w, ada_b, norm1_g, norm2_g, ffn_w_in, ffn_w_out, attn_w_qkv, attn_q_norm,
           attn_k_norm, attn_sink, attn_w_o, ret_w_qkvg, ret_decay_logit, ret_gn_g, ret_w_o):
    b, s, d = x.shape
    l = ctx.shape[1]
    qd = N_HEADS * HEAD_DIM
    kvd = N_KV_HEADS * HEAD_DIM
    tm_x = min(TOKEN_TILE, s)
    tm_c = min(TOKEN_TILE, l)

    rows = -(-(b + 1) // 8) * 8
    cin = jnp.concatenate([c, c_ctx[None, :], jnp.zeros((rows - b - 1, d), F32)], axis=0)
    mods = _mods(cin, ada_w, ada_b)
    mx0 = mods[0, :b].reshape(b, 1, 6 * d)
    mc0 = mods[0, b:b + 1].reshape(1, 1, 6 * d)
    mx1 = mods[1, :b].reshape(b, 1, 6 * d)
    mc1 = mods[1, b:b + 1].reshape(1, 1, 6 * d)

    w_qkv = attn_w_qkv[0].astype(BF16)
    scale = HEAD_DIM ** -0.5 * LOG2E
    gain = jnp.concatenate([jnp.tile(attn_q_norm[0] * scale, N_HEADS), jnp.tile(attn_k_norm[0], N_KV_HEADS)])[None, :]
    n_norm_heads = N_HEADS + N_KV_HEADS
    head_of = jnp.arange(qd + kvd) // HEAD_DIM
    gsum = ((head_of[:, None] == jnp.arange(LANES)[None, :]).astype(F32) / HEAD_DIM).astype(BF16)
    gbc_half = (jnp.arange(LANES)[:, None] == head_of[None, :]).astype(BF16)
    gbc = jnp.concatenate([gbc_half, gbc_half], axis=0)
    del n_norm_heads
    cos64, sin64 = _rope_tables(s, HEAD_DIM)
    cos_x = jnp.tile(cos64, (1, LANES // HEAD_DIM))
    sin_x = jnp.tile(sin64, (1, LANES // HEAD_DIM))
    cos_c = jnp.ones((l, LANES), F32)
    sin_c = jnp.zeros((l, LANES), F32)
    g1 = norm1_g[0][None, :]
    g2 = norm2_g[0][None, :]
    q_x, kk_x, vv_x = _qkv0(x, mx0, True, g1, w_qkv, gain, gsum, gbc, cos_x, sin_x, tm_x)
    q_c, kk_c, vv_c = _qkv0(ctx, mc0, False, g1, w_qkv, gain, gsum, gbc, cos_c, sin_c, tm_c)
    sink = attn_sink[0].astype(F32)
    o_x = _attn_x(sink, q_x, kk_x, vv_x, kk_c, vv_c)
    o_c = _attn_c(sink, q_c, kk_c, vv_c)
    w_o = attn_w_o[0].astype(BF16)
    w_in0 = ffn_w_in[0].astype(BF16)
    w_out0 = ffn_w_out[0].astype(BF16)
    x1 = _post0(x, o_x, mx0, True, g2, w_o, w_in0, w_out0, tm_x)
    y_ctx = _post0(ctx, o_c, mc0, False, g2, w_o, w_in0, w_out0, tm_c)

    w_qkvg = ret_w_qkvg[0].astype(BF16)
    cos256, sin256 = _rope_tables(s, RET_QK_DIM)
    g1 = norm1_g[1][None, :]
    g2 = norm2_g[1][None, :]
    q1, k1, v1 = _proj1(x1, mx1, True, g1, w_qkvg, (cos256, sin256), tm_x)
    k1c, v1c = _proj1(y_ctx, mc1, False, g1, w_qkvg, None, tm_c)
    y = _retention(ret_decay_logit[0].astype(F32), q1, k1, v1, k1c, v1c)
    return _post1(x1, y, mx1, g1, g2, ret_gn_g[0][None, :], w_qkvg, ret_w_o[0].astype(BF16),
                  ffn_w_in[1].astype(BF16), ffn_w_out[1].astype(BF16), tm_x)
```

```python
import functools

import jax
import jax.numpy as jnp
from jax import lax
from jax.experimental import pallas as pl
from jax.experimental.pallas import tpu as pltpu

F32 = jnp.float32
BF16 = jnp.bfloat16

D_MODEL = 1024
GRID_W = 64
HEAD_DIM = 64
N_HEADS = D_MODEL // HEAD_DIM
N_KV_HEADS = N_HEADS // 4
WINDOW = 128
ATTN_BLOCK = 128
RET_HEADS = 4
RET_QK_DIM = 256
RET_V_DIM = 512
RET_VWIDTH = 2 * D_MODEL
RET_CHUNK = 256
ATTN_PAIRS_X = 2
ATTN_PAIRS_C = 2
ATTN_BLOCKS_PER_STEP = 4
ATTN_LOOKAHEAD = 2
RET_LOOKAHEAD = 2
D_FF = 2816
MXU_TILE = 256
FF_CHUNKS = ((0, 6 * MXU_TILE), (6 * MXU_TILE, D_FF))
ROW_PARTS = 2
ROPE_BASE = 10000.0
EPS = 1e-6
NEG_INF = -1e30
LOG2E = 1.4426950408889634
LANES = 128

VMEM_LIMIT = 56 * 1024 * 1024
TOKEN_TILE = 512


def _dot(a, b):
    return jnp.dot(a, b, preferred_element_type=F32)


def _dot_nt(a, b):
    return lax.dot_general(a, b, (((1,), (1,)), ((), ())), preferred_element_type=F32)


def _dot_tn(a, b):
    return lax.dot_general(a, b, (((0,), (0,)), ((), ())), preferred_element_type=F32)


def _rms(x, g):
    return x * lax.rsqrt(jnp.mean(x * x, axis=-1, keepdims=True) + EPS) * g


def _silu(x):
    return x * jax.nn.sigmoid(x)


def _params(n_axes):
    return pltpu.CompilerParams(dimension_semantics=("arbitrary",) * n_axes, vmem_limit_bytes=VMEM_LIMIT)


def _const2(shape):
    return pl.BlockSpec(shape, lambda b, j: (0,) * len(shape), pipeline_mode=pl.Buffered(1))


def _layer_spec(stacked, layer):
    return pl.BlockSpec((None,) + stacked.shape[1:], lambda b, j: (layer,) + (0,) * (stacked.ndim - 1),
                        pipeline_mode=pl.Buffered(1))


def _mod_spec(k, per_batch):
    if per_batch:
        return pl.BlockSpec((None, 1, D_MODEL), lambda b, j, k=k: (b, 0, k))
    return pl.BlockSpec((None, 1, D_MODEL), lambda b, j, k=k: (0, 0, k))


def _tok_spec(tm, width):
    return pl.BlockSpec((None, tm, width), lambda b, j: (b, j, 0))


def _row_parts(ref, n):
    tm = ref.shape[0]
    n = n if tm % (n * ATTN_BLOCK) == 0 else 1
    return [slice(i * tm // n, (i + 1) * tm // n) for i in range(n)]


def _mods_kernel(c_ref, w_ref, b_ref, o_ref):
    a = _silu(c_ref[...]).astype(BF16)
    o_ref[...] = _dot(a, w_ref[...].astype(BF16)) + b_ref[...]


def _mods(cin, ada_w, ada_b):
    depth, d, n = ada_w.shape
    rows = cin.shape[0]
    tn = 1536
    return pl.pallas_call(
        _mods_kernel,
        grid=(depth, n // tn),
        in_specs=[
            pl.BlockSpec((rows, d), lambda i, j: (0, 0)),
            pl.BlockSpec((None, d, tn), lambda i, j: (i, 0, j)),
            pl.BlockSpec((None, 1, tn), lambda i, j: (i, 0, j)),
        ],
        out_specs=pl.BlockSpec((None, rows, tn), lambda i, j: (i, 0, j)),
        out_shape=jax.ShapeDtypeStruct((depth, rows, n), F32),
        compiler_params=_params(2),
        name="adaln_mods",
    )(cin, ada_w, ada_b.reshape(depth, 1, n))


def _dup_halves(r):
    lane = lax.broadcasted_iota(jnp.int32, r.shape, 1)
    lo = lane < HEAD_DIM
    r64 = pltpu.roll(r, HEAD_DIM, 1)
    return jnp.where(lo, r, r64), jnp.where(lo, r64, r)


def _qkv0_kernel(x_ref, sh_ref, sc_ref, g1_ref, w_ref, gain_ref, gsum_ref, cos_ref, sin_ref,
                 q_ref, kk_ref, vt_ref):
    qd = N_HEADS * HEAD_DIM
    qkd = qd + N_KV_HEADS * HEAD_DIM
    parts = _row_parts(x_ref, ROW_PARTS)
    hs = [_rms(x_ref[rows, :], g1_ref[...]) * (1.0 + sc_ref[...]) + sh_ref[...] for rows in parts]
    qkvs = [_dot(h.astype(BF16), w_ref[...]) for h in hs]
    invfs = []
    for qkv in qkvs:
        sq = (qkv[:, :qkd] * qkv[:, :qkd]).astype(BF16)
        ms = jnp.concatenate([_dot(sq[:, c:c + MXU_TILE], gsum_ref[...]) for c in range(0, qkd, MXU_TILE)], axis=1)
        invfs.append(lax.rsqrt(ms + EPS))
    row = lax.broadcasted_iota(jnp.int32, (LANES - HEAD_DIM, ATTN_BLOCK), 0)
    tail = jnp.where(row == 0, 1.0, 0.0).astype(BF16)
    for rows, qkv, invf in zip(parts, qkvs, invfs):
        t = qkv[:, :qkd] * gain_ref[...]
        cos = cos_ref[rows, :]
        sin = sin_ref[rows, :]
        lane = lax.broadcasted_iota(jnp.int32, cos.shape, 1)
        first = (lane % 32) < 16
        for cb in range(qkd // LANES):
            sl = slice(cb * LANES, (cb + 1) * LANES)
            tb = t[:, sl]
            sw = jnp.where(first, pltpu.roll(tb, LANES - 16, 1), pltpu.roll(tb, 16, 1))
            r = (tb * cos + sw * sin) * invf[:, sl]
            if cb < qd // LANES:
                q_ref[rows, sl] = r.astype(BF16)
            else:
                p = cb - qd // LANES
                a, b = _dup_halves(r)
                kk_ref[rows, (2 * p) * LANES:(2 * p + 1) * LANES] = a.astype(BF16)
                kk_ref[rows, (2 * p + 1) * LANES:(2 * p + 2) * LANES] = b.astype(BF16)
        vt = qkv[:, qkd:].T
        for blk in range(vt.shape[1] // ATTN_BLOCK):
            cols = slice(blk * ATTN_BLOCK, (blk + 1) * ATTN_BLOCK)
            slab = rows.start // ATTN_BLOCK + blk
            for g in range(N_KV_HEADS):
                vt_ref[slab, g * LANES:g * LANES + HEAD_DIM, :] = vt[g * HEAD_DIM:(g + 1) * HEAD_DIM, cols].astype(BF16)
                vt_ref[slab, g * LANES + HEAD_DIM:(g + 1) * LANES, :] = tail


def _qkv0(xin, mods, per_batch, g1, w, gain, gsum, cos, sin, tm):
    b, t, d = xin.shape
    kw = N_KV_HEADS * LANES
    return pl.pallas_call(
        _qkv0_kernel,
        grid=(b, t // tm),
        in_specs=[
            _tok_spec(tm, d),
            _mod_spec(0, per_batch),
            _mod_spec(1, per_batch),
            _const2((1, d)),
            _const2(w.shape),
            _const2(gain.shape),
            _const2(gsum.shape),
            pl.BlockSpec((tm, LANES), lambda b, j: (j, 0)),
            pl.BlockSpec((tm, LANES), lambda b, j: (j, 0)),
        ],
        out_specs=[_tok_spec(tm, d), _tok_spec(tm, kw),
                   pl.BlockSpec((None, tm // ATTN_BLOCK, kw, ATTN_BLOCK), lambda b, j: (b, j, 0, 0))],
        out_shape=[
            jax.ShapeDtypeStruct((b, t, d), BF16),
            jax.ShapeDtypeStruct((b, t, kw), BF16),
            jax.ShapeDtypeStruct((b, t // ATTN_BLOCK, kw, ATTN_BLOCK), BF16),
        ],
        compiler_params=_params(2),
        name="attn_qkv_proj",
    )(xin, mods, mods, g1, w, gain, gsum, cos, sin)


def _band_masks(j, nb, heads):
    shape = (ATTN_BLOCK, heads * ATTN_BLOCK)
    key = lax.broadcasted_iota(jnp.int32, shape, 0)
    qry = lax.broadcasted_iota(jnp.int32, shape, 1) % ATTN_BLOCK
    return (key >= qry) & (j > 0), (key <= qry) & (j < nb - 1)


def _attn_scores(sink_ref, q_ref, key_refs, pr, pairs, masks):
    tq = q_ref.shape[0]
    g = pr // (N_HEADS // N_KV_HEADS // 2)
    gs = slice(g * LANES, (g + 1) * LANES)
    lo = lax.broadcasted_iota(jnp.int32, (tq, LANES), 1) < HEAD_DIM
    zero = jnp.zeros((tq, LANES), BF16)
    stack = []
    for p in range(pr, pr + pairs):
        qb = q_ref[:, p * LANES:(p + 1) * LANES]
        stack += [jnp.where(lo, qb, zero), jnp.where(lo, zero, qb)]
    keys = jnp.concatenate([r[:, gs] for r in key_refs], axis=0) if len(key_refs) > 1 else key_refs[0][:, gs]
    st = _dot_nt(keys, jnp.concatenate(stack, axis=0))
    parts = [st[i * LANES:(i + 1) * LANES, :] for i in range(st.shape[0] // LANES)]
    if masks is not None:
        ia = key_refs[0].shape[0] // LANES
        parts[ia] = jnp.where(masks[0], parts[ia], NEG_INF)
        parts[ia + 2] = jnp.where(masks[1], parts[ia + 2], NEG_INF)
    sink = jnp.concatenate([jnp.full((1, tq), sink_ref[2 * pr + i] * LOG2E, F32) for i in range(2 * pairs)], axis=1)
    mx = parts[0]
    for part in parts[1:]:
        mx = jnp.maximum(mx, part)
    m = jnp.maximum(jnp.max(mx, axis=0, keepdims=True), sink)
    return jnp.concatenate([jnp.exp2(part - m).astype(BF16) for part in parts], axis=0), jnp.exp2(sink - m)


def _attn_values(probs, val_refs, o_ref, pr, pairs):
    pt, p_sink = probs
    tq = o_ref.shape[0]
    g = pr // (N_HEADS // N_KV_HEADS // 2)
    gs = slice(g * LANES, (g + 1) * LANES)
    vt = jnp.concatenate([r[i, gs, :] for r in val_refs for i in range(r.shape[0])], axis=1)
    ot = _dot(vt, pt)
    den = jnp.sum(ot[HEAD_DIM:HEAD_DIM + 8, :], axis=0, keepdims=True) + p_sink
    out = ot[:HEAD_DIM, :] / den
    for i in range(pairs):
        pair = jnp.concatenate([out[:, 2 * i * tq:(2 * i + 1) * tq], out[:, (2 * i + 1) * tq:(2 * i + 2) * tq]], axis=0)
        o_ref[:, (pr + i) * LANES:(pr + i + 1) * LANES] = pair.T.astype(BF16)


def _attn_pipeline(sink_ref, tasks, pairs):
    items = [(task, g) for task in tasks for g in range(0, N_HEADS // 2, pairs)]
    probs = {}
    for step in range(len(items) + ATTN_LOOKAHEAD):
        if step < len(items):
            (q_ref, key_refs, _, _, masks), g = items[step]
            probs[step] = _attn_scores(sink_ref, q_ref, key_refs, g, pairs, masks)
        done = step - ATTN_LOOKAHEAD
        if done >= 0:
            (_, _, val_refs, o_ref, _), g = items[done]
            _attn_values(probs.pop(done), val_refs, o_ref, g, pairs)


def _attn_ctx_kernel(sink_ref, q_ref, kc_ref, vc_ref, o_ref, *, pairs):
    _attn_pipeline(sink_ref, [(q_ref, [kc_ref], [vc_ref], o_ref, None)], pairs)


def _attn_band_kernel(sink_ref, q_ref, kc_ref, k_ref, vc_ref, vt_ref, o_ref, *, nb, pairs):
    tasks = []
    for t in range(q_ref.shape[0] // ATTN_BLOCK):
        j = pl.program_id(1) * (q_ref.shape[0] // ATTN_BLOCK) + t
        rows = pl.ds(t * ATTN_BLOCK, ATTN_BLOCK)
        band = [jnp.maximum(j - 1, 0), j, jnp.minimum(j + 1, nb - 1)]
        key_refs = [kc_ref] + [k_ref.at[pl.ds(pl.multiple_of(blk * ATTN_BLOCK, ATTN_BLOCK), ATTN_BLOCK), :]
                               for blk in band]
        val_refs = [vc_ref] + [vt_ref.at[pl.ds(blk, 1)] for blk in band]
        tasks.append((q_ref.at[rows, :], key_refs, val_refs, o_ref.at[rows, :], _band_masks(j, nb, 2 * pairs)))
    _attn_pipeline(sink_ref, tasks, pairs)


def _attn_x(sink, q, kk, vv, kkc, vvc):
    b, s, d = q.shape
    nb = s // ATTN_BLOCK
    rows = ATTN_BLOCKS_PER_STEP * ATTN_BLOCK
    per_batch = lambda a: pl.BlockSpec((None,) + a.shape[1:], lambda b, j: (b,) + (0,) * (a.ndim - 1))
    blk = pl.BlockSpec((None, rows, d), lambda b, j: (b, j, 0))
    return pl.pallas_call(
        functools.partial(_attn_band_kernel, nb=nb, pairs=ATTN_PAIRS_X),
        grid=(b, s // rows),
        in_specs=[pl.BlockSpec(memory_space=pltpu.SMEM), blk,
                  per_batch(kkc), per_batch(kk), per_batch(vvc), per_batch(vv)],
        out_specs=blk,
        out_shape=jax.ShapeDtypeStruct((b, s, d), BF16),
        compiler_params=_params(2),
        name="attn_latent",
    )(sink, q, kkc, kk, vvc, vv)


def _attn_c(sink, qc, kkc, vvc):
    b, l, d = qc.shape
    return pl.pallas_call(
        functools.partial(_attn_ctx_kernel, pairs=ATTN_PAIRS_C),
        grid=(b,),
        in_specs=[
            pl.BlockSpec(memory_space=pltpu.SMEM),
            pl.BlockSpec((None, l, d), lambda b: (b, 0, 0)),
            pl.BlockSpec((None, l, kkc.shape[2]), lambda b: (b, 0, 0)),
            pl.BlockSpec((None,) + vvc.shape[1:], lambda b: (b, 0, 0, 0)),
        ],
        out_specs=pl.BlockSpec((None, l, d), lambda b: (b, 0, 0)),
        out_shape=jax.ShapeDtypeStruct((b, l, d), BF16),
        compiler_params=_params(1),
        name="attn_context",
    )(sink, qc, kkc, vvc)


def _ffn_tail(x1s, parts, shf_ref, scf_ref, gtf_ref, g2_ref, win_ref, wout_ref, out_ref):
    h2s = [(_rms(x1, g2_ref[...]) * (1.0 + scf_ref[...]) + shf_ref[...]).astype(BF16) for x1 in x1s]
    accs = [None] * len(x1s)
    hidden = {}
    for step in range(len(FF_CHUNKS) + 1):
        if step < len(FF_CHUNKS):
            lo, hi = FF_CHUNKS[step]
            hidden[step] = [(_dot(h2, win_ref[:, lo:hi]), _dot(h2, win_ref[:, D_FF + lo:D_FF + hi])) for h2 in h2s]
        if step > 0:
            lo, hi = FF_CHUNKS[step - 1]
            for i, (gate, up) in enumerate(hidden.pop(step - 1)):
                down = _dot((_silu(gate) * up).astype(BF16), wout_ref[lo:hi, :])
                accs[i] = down if accs[i] is None else accs[i] + down
    for rows, x1, acc in zip(parts, x1s, accs):
        out_ref[rows, :] = x1 + gtf_ref[...] * acc


def _post0_kernel(x_ref, o_ref, gtm_ref, shf_ref, scf_ref, gtf_ref, g2_ref, wo_ref, win_ref, wout_ref, out_ref):
    parts = _row_parts(x_ref, ROW_PARTS)
    mixed = [_dot(o_ref[rows, :], wo_ref[...]) for rows in parts]
    x1s = [x_ref[rows, :] + gtm_ref[...] * a for rows, a in zip(parts, mixed)]
    _ffn_tail(x1s, parts, shf_ref, scf_ref, gtf_ref, g2_ref, win_ref, wout_ref, out_ref)


def _post0(xin, o, mods, per_batch, g2, wo, win, wout, layer, tm):
    b, t, d = xin.shape
    return pl.pallas_call(
        _post0_kernel,
        grid=(b, t // tm),
        in_specs=[
            _tok_spec(tm, d), _tok_spec(tm, d),
            _mod_spec(2, per_batch), _mod_spec(3, per_batch), _mod_spec(4, per_batch), _mod_spec(5, per_batch),
            _const2((1, d)), _const2(wo.shape), _layer_spec(win, layer), _layer_spec(wout, layer),
        ],
        out_specs=_tok_spec(tm, d),
        out_shape=jax.ShapeDtypeStruct((b, t, d), F32),
        compiler_params=_params(2),
        name="attn_out_ffn",
    )(xin, o, mods, mods, mods, mods, g2, wo, win, wout)


def _post1_kernel(x_ref, y_ref, shm_ref, scm_ref, gtm_ref, shf_ref, scf_ref, gtf_ref, g1_ref, g2_ref, gn_ref,
                  wg_ref, wo_ref, win_ref, wout_ref, out_ref):
    parts = _row_parts(x_ref, ROW_PARTS)
    xs = [x_ref[rows, :] for rows in parts]
    h1s = [(_rms(x, g1_ref[...]) * (1.0 + scm_ref[...]) + shm_ref[...]).astype(BF16) for x in xs]
    gates = [_dot(h1, wg_ref[...]) for h1 in h1s]
    gated = []
    for rows, gate in zip(parts, gates):
        normed = []
        for h in range(RET_HEADS):
            o = y_ref[rows, h * RET_V_DIM:(h + 1) * RET_V_DIM].astype(F32)
            d = o - jnp.mean(o, axis=-1, keepdims=True)
            normed.append(d * lax.rsqrt(jnp.mean(d * d, axis=-1, keepdims=True) + EPS))
        gated.append((_silu(gate) * (jnp.concatenate(normed, axis=1) * gn_ref[...])).astype(BF16))
    mixed = [_dot(g, wo_ref[...]) for g in gated]
    x1s = [x + gtm_ref[...] * a for x, a in zip(xs, mixed)]
    _ffn_tail(x1s, parts, shf_ref, scf_ref, gtf_ref, g2_ref, win_ref, wout_ref, out_ref)


def _post1(xin, y, mods, g1, g2, gn, wg, wo, win, wout, layer, tm):
    b, t, d = xin.shape
    return pl.pallas_call(
        _post1_kernel,
        grid=(b, t // tm),
        in_specs=[
            _tok_spec(tm, d), _tok_spec(tm, y.shape[2]),
            *[_mod_spec(k, True) for k in range(6)],
            _const2((1, d)), _const2((1, d)), _const2(gn.shape),
            pl.BlockSpec((d, RET_VWIDTH), lambda b, j: (0, wg.shape[1] // RET_VWIDTH - 1), pipeline_mode=pl.Buffered(1)),
            _const2(wo.shape), _layer_spec(win, layer), _layer_spec(wout, layer),
        ],
        out_specs=_tok_spec(tm, d),
        out_shape=jax.ShapeDtypeStruct((b, t, d), F32),
        compiler_params=_params(2),
        name="ret_out_ffn",
    )(xin, y, mods, mods, mods, mods, mods, mods, g1, g2, gn, wg, wo, win, wout)


def _proj1_kernel(x_ref, sh_ref, sc_ref, g1_ref, w_ref, cos_ref, sin_ref, q_ref, k_ref, v_ref):
    qk_w = 2 * RET_HEADS * RET_QK_DIM
    hb = (_rms(x_ref[...], g1_ref[...]) * (1.0 + sc_ref[...]) + sh_ref[...]).astype(BF16)
    qk = _dot(hb, w_ref[:, :qk_w])
    k_scale = RET_QK_DIM ** -0.5
    for cb in range(qk_w // LANES):
        ts = slice((cb % 2) * LANES, (cb % 2 + 1) * LANES)
        tb = qk[:, cb * LANES:(cb + 1) * LANES]
        r = tb * cos_ref[:, ts] + pltpu.roll(tb, LANES // 2, 1) * sin_ref[:, ts]
        if cb < qk_w // (2 * LANES):
            q_ref[:, cb * LANES:(cb + 1) * LANES] = r.astype(BF16)
        else:
            cc = cb - qk_w // (2 * LANES)
            k_ref[:, cc * LANES:(cc + 1) * LANES] = (r * k_scale).astype(BF16)
    v_ref[...] = _dot(hb, w_ref[:, qk_w:]).astype(BF16)


def _proj1_ctx_kernel(x_ref, sh_ref, sc_ref, g1_ref, w_ref, k_ref, v_ref):
    qw = RET_HEADS * RET_QK_DIM
    hb = (_rms(x_ref[...], g1_ref[...]) * (1.0 + sc_ref[...]) + sh_ref[...]).astype(BF16)
    kv = _dot(hb, w_ref[:, qw:])
    k_ref[...] = (kv[:, :qw] * RET_QK_DIM ** -0.5).astype(BF16)
    v_ref[...] = kv[:, qw:].astype(BF16)


def _proj1(xin, mods, per_batch, g1, w, tables, tm):
    b, t, d = xin.shape
    qw = RET_HEADS * RET_QK_DIM
    specs = [
        _tok_spec(tm, d),
        _mod_spec(0, per_batch), _mod_spec(1, per_batch),
        _const2((1, d)),
        pl.BlockSpec((d, 2 * qw + RET_VWIDTH), lambda b, j: (0, 0), pipeline_mode=pl.Buffered(1)),
    ]
    outs = [(qw, _tok_spec(tm, qw)), (RET_VWIDTH, _tok_spec(tm, RET_VWIDTH))]
    if tables is not None:
        specs += [pl.BlockSpec((tm, 2 * LANES), lambda b, j: (j, 0))] * 2
        outs = [(qw, _tok_spec(tm, qw))] + outs
    return pl.pallas_call(
        _proj1_ctx_kernel if tables is None else _proj1_kernel,
        grid=(b, t // tm),
        in_specs=specs,
        out_specs=[spec for _, spec in outs],
        out_shape=[jax.ShapeDtypeStruct((b, t, width), BF16) for width, _ in outs],
        compiler_params=_params(2),
        name="ret_qkv_proj",
    )(xin, mods, mods, g1, w, *(tables or ()))


def _ret_kernel(dl_ref, q_ref, k_ref, v_ref, kc_ref, vc_ref, y_ref,
                o_scr, sf_scr, sb_scr, comb_scr, qdf_scr, qdb_scr, kdf_scr, kdb_scr, cdf_scr, cdb_scr,
                ctxf_scr, ctxb_scr, *, nc):
    hd = pl.program_id(0)
    c_len = RET_CHUNK
    l_ctx = kc_ref.shape[0]

    @pl.when(pl.program_id(1) == 0)
    def _decay_tables():
        def log_decay(direction, shp):
            logit = jnp.full(shp, dl_ref[direction, hd], F32)
            return jnp.minimum(logit, 0.0) - jnp.log1p(jnp.exp(-jnp.abs(logit)))

        shape = (c_len, c_len)
        n = lax.broadcasted_iota(jnp.int32, shape, 0).astype(F32)
        m = lax.broadcasted_iota(jnp.int32, shape, 1).astype(F32)
        comb_scr[...] = (jnp.where(n >= m, jnp.exp(log_decay(0, shape) * jnp.maximum(n - m, 0.0)), 0.0)
                         + jnp.where(m >= n, jnp.exp(log_decay(1, shape) * jnp.maximum(m - n, 0.0)), 0.0))
        nk = lax.broadcasted_iota(jnp.int32, (c_len, RET_QK_DIM), 0).astype(F32)
        nv = lax.broadcasted_iota(jnp.int32, (c_len, RET_V_DIM), 0).astype(F32)
        qdf_scr[...] = jnp.exp(log_decay(0, nv.shape) * (nv + 1.0))
        kdf_scr[...] = jnp.exp(log_decay(0, nk.shape) * (c_len - 1.0 - nk))
        cdf_scr[...] = jnp.exp(log_decay(0, (1, RET_V_DIM)) * float(c_len))
        qdb_scr[...] = jnp.exp(log_decay(1, nv.shape) * (c_len - nv))
        kdb_scr[...] = jnp.exp(log_decay(1, nk.shape) * nk)
        cdb_scr[...] = jnp.exp(log_decay(1, (1, RET_V_DIM)) * float(c_len))
        tt = lax.broadcasted_iota(jnp.int32, (l_ctx, RET_QK_DIM), 0).astype(F32)
        ctxf_scr[...] = jnp.exp(log_decay(0, tt.shape) * (l_ctx - 1.0 - tt))
        ctxb_scr[...] = jnp.exp(log_decay(1, tt.shape) * tt)

    kc = kc_ref[...].astype(F32)
    vc = vc_ref[...]
    sf_scr[...] = _dot_tn((kc * ctxf_scr[...]).astype(BF16), vc)
    sb_scr[...] = _dot_tn((kc * ctxb_scr[...]).astype(BF16), vc)

    def rows(c):
        return slice(c * c_len, (c + 1) * c_len)

    def masked_scores(c):
        return (_dot_nt(q_ref[rows(c), :], k_ref[rows(c), :]) * comb_scr[...]).astype(BF16)

    def forward(c, scores):
        qc, kx, vx = q_ref[rows(c), :], k_ref[rows(c), :], v_ref[rows(c), :]
        st = sf_scr[...]
        out = _dot(scores, vx) + _dot(qc, st.astype(BF16)) * qdf_scr[...]
        sf_scr[...] = st * cdf_scr[...] + _dot_tn((kx.astype(F32) * kdf_scr[...]).astype(BF16), vx)
        return out

    def backward(c):
        qc, kx, vx = q_ref[rows(c), :], k_ref[rows(c), :], v_ref[rows(c), :]
        st = sb_scr[...]
        out = _dot(qc, st.astype(BF16)) * qdb_scr[...]
        sb_scr[...] = st * cdb_scr[...] + _dot_tn((kx.astype(F32) * kdb_scr[...]).astype(BF16), vx)
        return out

    scores = {}
    for step in range(nc + RET_LOOKAHEAD):
        if step < nc:
            scores[step] = masked_scores(step)
        i = step - RET_LOOKAHEAD
        if i < 0:
            continue
        for c, out in ((i, forward(i, scores.pop(i))), (nc - 1 - i, backward(nc - 1 - i))):
            if i < nc // 2:
                o_scr[rows(c), :] = out
            else:
                y_ref[rows(c), :] = (o_scr[rows(c), :] + out).astype(BF16)


def _retention(dl, q, k, v, kc, vc):
    b, s, _ = q.shape
    l = kc.shape[1]
    nc = s // RET_CHUNK
    assert nc % 2 == 0
    hspec = lambda t, w: pl.BlockSpec((None, t, w), lambda h, b: (b, 0, h))
    return pl.pallas_call(
        functools.partial(_ret_kernel, nc=nc),
        grid=(RET_HEADS, b),
        in_specs=[
            pl.BlockSpec(memory_space=pltpu.SMEM),
            hspec(s, RET_QK_DIM), hspec(s, RET_QK_DIM), hspec(s, RET_V_DIM),
            hspec(l, RET_QK_DIM), hspec(l, RET_V_DIM),
        ],
        out_specs=hspec(s, RET_V_DIM),
        out_shape=jax.ShapeDtypeStruct((b, s, RET_VWIDTH), BF16),
        scratch_shapes=[
            pltpu.VMEM((s, RET_V_DIM), F32),
            pltpu.VMEM((RET_QK_DIM, RET_V_DIM), F32),
            pltpu.VMEM((RET_QK_DIM, RET_V_DIM), F32),
            pltpu.VMEM((RET_CHUNK, RET_CHUNK), F32),
            pltpu.VMEM((RET_CHUNK, RET_V_DIM), F32),
            pltpu.VMEM((RET_CHUNK, RET_V_DIM), F32),
            pltpu.VMEM((RET_CHUNK, RET_QK_DIM), F32),
            pltpu.VMEM((RET_CHUNK, RET_QK_DIM), F32),
            pltpu.VMEM((1, RET_V_DIM), F32),
            pltpu.VMEM((1, RET_V_DIM), F32),
            pltpu.VMEM((l, RET_QK_DIM), F32),
            pltpu.VMEM((l, RET_QK_DIM), F32),
        ],
        compiler_params=_params(2),
        name="retention",
    )(dl, q, k, v, kc, vc)


def _rope_tables(n, head_dim):
    rows = n // GRID_W
    row = jnp.broadcast_to(jnp.arange(rows, dtype=jnp.int32)[:, None], (rows, GRID_W)).reshape(n)
    col = jnp.broadcast_to(jnp.arange(GRID_W, dtype=jnp.int32)[None, :], (rows, GRID_W)).reshape(n)
    axis_dim = head_dim // 2
    inv = ROPE_BASE ** (-jnp.arange(0, axis_dim, 2, dtype=F32) / axis_dim)
    ang_r = row.astype(F32)[:, None] * inv
    ang_c = col.astype(F32)[:, None] * inv
    cos = jnp.concatenate([jnp.cos(ang_r)] * 2 + [jnp.cos(ang_c)] * 2, axis=-1)
    sin = jnp.concatenate([-jnp.sin(ang_r), jnp.sin(ang_r), -jnp.sin(ang_c), jnp.sin(ang_c)], axis=-1)
    return cos, sin


def kernel(x, c, ctx, c_ctx, ada_w, ada_b, norm1_g, norm2_g, ffn_w_in, ffn_w_out, attn_w_qkv, attn_q_norm,
           attn_k_norm, attn_sink, attn_w_o, ret_w_qkvg, ret_decay_logit, ret_gn_g, ret_w_o):
    b, s, d = x.shape
    l = ctx.shape[1]
    tm_x = min(TOKEN_TILE, s)
    tm_c = min(TOKEN_TILE, l)

    rows = -(-(b + 1) // 8) * 8
    cin = jnp.concatenate([c, c_ctx[None, :], jnp.zeros((rows - b - 1, d), F32)], axis=0)
    mods = _mods(cin, ada_w, ada_b)
    mx0 = mods[0, :b].reshape(b, 1, 6 * d)
    mc0 = mods[0, b:b + 1].reshape(1, 1, 6 * d)
    mx1 = mods[1, :b].reshape(b, 1, 6 * d)
    mc1 = mods[1, b:b + 1].reshape(1, 1, 6 * d)

    w_qkv = attn_w_qkv[0].astype(BF16)
    scale = HEAD_DIM ** -0.5 * LOG2E
    gain = jnp.concatenate([jnp.tile(attn_q_norm[0] * scale, N_HEADS), jnp.tile(attn_k_norm[0], N_KV_HEADS)])[None, :]
    lane_head = jnp.arange(MXU_TILE) // HEAD_DIM
    gsum = ((lane_head[:, None] == lane_head[None, :]).astype(F32) / HEAD_DIM).astype(BF16)
    cos64, sin64 = _rope_tables(s, HEAD_DIM)
    cos_x = jnp.tile(cos64, (1, LANES // HEAD_DIM))
    sin_x = jnp.tile(sin64, (1, LANES // HEAD_DIM))
    cos_c = jnp.ones((l, LANES), F32)
    sin_c = jnp.zeros((l, LANES), F32)
    g1 = norm1_g[0][None, :]
    g2 = norm2_g[0][None, :]
    q_x, kk_x, vv_x = _qkv0(x, mx0, True, g1, w_qkv, gain, gsum, cos_x, sin_x, tm_x)
    q_c, kk_c, vv_c = _qkv0(ctx, mc0, False, g1, w_qkv, gain, gsum, cos_c, sin_c, tm_c)
    sink = attn_sink[0].astype(F32)
    o_x = _attn_x(sink, q_x, kk_x, vv_x, kk_c, vv_c)
    o_c = _attn_c(sink, q_c, kk_c, vv_c)
    w_o = attn_w_o[0].astype(BF16)
    w_in = ffn_w_in.astype(BF16)
    w_out = ffn_w_out.astype(BF16)
    x1 = _post0(x, o_x, mx0, True, g2, w_o, w_in, w_out, 0, tm_x)
    y_ctx = _post0(ctx, o_c, mc0, False, g2, w_o, w_in, w_out, 0, tm_c)

    w_qkvg = ret_w_qkvg[0].astype(BF16)
    cos256, sin256 = _rope_tables(s, RET_QK_DIM)
    g1 = norm1_g[1][None, :]
    g2 = norm2_g[1][None, :]
    q1, k1, v1 = _proj1(x1, mx1, True, g1, w_qkvg, (cos256, sin256), tm_x)
    k1c, v1c = _proj1(y_ctx, mc1, False, g1, w_qkvg, None, tm_c)
    y = _retention(ret_decay_logit[0].astype(F32), q1, k1, v1, k1c, v1c)
    return _post1(x1, y, mx1, g1, g2, ret_gn_g[0][None, :], w_qkvg, ret_w_o[0].astype(BF16),
                  w_in, w_out, 1, tm_x)
```

```python
import functools

import jax
import jax.numpy as jnp
from jax import lax
from jax.experimental import pallas as pl
from jax.experimental.pallas import tpu as pltpu

F32 = jnp.float32
BF16 = jnp.bfloat16

D_MODEL = 1024
GRID_W = 64
HEAD_DIM = 64
N_HEADS = D_MODEL // HEAD_DIM
N_KV_HEADS = N_HEADS // 4
WINDOW = 128
ATTN_BLOCK = 128
RET_HEADS = 4
RET_QK_DIM = 256
RET_V_DIM = 512
RET_VWIDTH = 2 * D_MODEL
RET_CHUNK = 256
ATTN_PAIRS_X = 2
ATTN_PAIRS_C = 2
ATTN_BLOCKS_PER_STEP = 4
ATTN_LOOKAHEAD = 2
RET_LOOKAHEAD = 2
D_FF = 2816
MXU_TILE = 256
FF_CHUNKS = ((0, 6 * MXU_TILE), (6 * MXU_TILE, D_FF))
ROW_PARTS = 2
ROPE_BASE = 10000.0
EPS = 1e-6
NEG_INF = -1e30
LOG2E = 1.4426950408889634
LANES = 128

VMEM_LIMIT = 56 * 1024 * 1024
FFN_TILE = 512
PROJ_TILE = 1024


def _dot(a, b):
    return jnp.dot(a, b, preferred_element_type=F32)


def _dot_nt(a, b):
    return lax.dot_general(a, b, (((1,), (1,)), ((), ())), preferred_element_type=F32)


def _dot_tn(a, b):
    return lax.dot_general(a, b, (((0,), (0,)), ((), ())), preferred_element_type=F32)


def _rms(x, g):
    return x * lax.rsqrt(jnp.mean(x * x, axis=-1, keepdims=True) + EPS) * g


def _silu(x):
    return x * jax.nn.sigmoid(x)


def _params(n_axes):
    return pltpu.CompilerParams(dimension_semantics=("arbitrary",) * n_axes, vmem_limit_bytes=VMEM_LIMIT)


def _const2(shape):
    return pl.BlockSpec(shape, lambda b, j: (0,) * len(shape), pipeline_mode=pl.Buffered(1))


def _layer_spec(stacked, layer):
    return pl.BlockSpec((None,) + stacked.shape[1:], lambda b, j: (layer,) + (0,) * (stacked.ndim - 1),
                        pipeline_mode=pl.Buffered(1))


def _mod_spec(k, per_batch):
    if per_batch:
        return pl.BlockSpec((None, 1, D_MODEL), lambda b, j, k=k: (b, 0, k))
    return pl.BlockSpec((None, 1, D_MODEL), lambda b, j, k=k: (0, 0, k))


def _tok_spec(tm, width):
    return pl.BlockSpec((None, tm, width), lambda b, j: (b, j, 0))


def _row_parts(ref, n):
    tm = ref.shape[0]
    n = n if tm % (n * ATTN_BLOCK) == 0 else 1
    return [slice(i * tm // n, (i + 1) * tm // n) for i in range(n)]


def _mods_kernel(c_ref, w_ref, b_ref, o_ref):
    a = _silu(c_ref[...]).astype(BF16)
    o_ref[...] = _dot(a, w_ref[...].astype(BF16)) + b_ref[...]


def _mods(cin, ada_w, ada_b):
    depth, d, n = ada_w.shape
    rows = cin.shape[0]
    tn = 1536
    return pl.pallas_call(
        _mods_kernel,
        grid=(depth, n // tn),
        in_specs=[
            pl.BlockSpec((rows, d), lambda i, j: (0, 0)),
            pl.BlockSpec((None, d, tn), lambda i, j: (i, 0, j)),
            pl.BlockSpec((None, 1, tn), lambda i, j: (i, 0, j)),
        ],
        out_specs=pl.BlockSpec((None, rows, tn), lambda i, j: (i, 0, j)),
        out_shape=jax.ShapeDtypeStruct((depth, rows, n), F32),
        compiler_params=_params(2),
        name="adaln_mods",
    )(cin, ada_w, ada_b.reshape(depth, 1, n))


def _dup_halves(r):
    lane = lax.broadcasted_iota(jnp.int32, r.shape, 1)
    lo = lane < HEAD_DIM
    r64 = pltpu.roll(r, HEAD_DIM, 1)
    return jnp.where(lo, r, r64), jnp.where(lo, r64, r)


def _qkv0_kernel(x_ref, sh_ref, sc_ref, g1_ref, w_ref, gain_ref, gsum_ref, cos_ref, sin_ref,
                 q_ref, kk_ref, vt_ref):
    qd = N_HEADS * HEAD_DIM
    qkd = qd + N_KV_HEADS * HEAD_DIM
    parts = _row_parts(x_ref, ROW_PARTS)
    hs = [_rms(x_ref[rows, :], g1_ref[...]) * (1.0 + sc_ref[...]) + sh_ref[...] for rows in parts]
    qkvs = [_dot(h.astype(BF16), w_ref[...]) for h in hs]
    invfs = []
    for qkv in qkvs:
        sq = (qkv[:, :qkd] * qkv[:, :qkd]).astype(BF16)
        ms = jnp.concatenate([_dot(sq[:, c:c + MXU_TILE], gsum_ref[...]) for c in range(0, qkd, MXU_TILE)], axis=1)
        invfs.append(lax.rsqrt(ms + EPS))
    row = lax.broadcasted_iota(jnp.int32, (LANES - HEAD_DIM, ATTN_BLOCK), 0)
    tail = jnp.where(row == 0, 1.0, 0.0).astype(BF16)
    for rows, qkv, invf in zip(parts, qkvs, invfs):
        t = qkv[:, :qkd] * gain_ref[...]
        cos = cos_ref[rows, :]
        sin = sin_ref[rows, :]
        lane = lax.broadcasted_iota(jnp.int32, cos.shape, 1)
        first = (lane % 32) < 16
        for cb in range(qkd // LANES):
            sl = slice(cb * LANES, (cb + 1) * LANES)
            tb = t[:, sl]
            sw = jnp.where(first, pltpu.roll(tb, LANES - 16, 1), pltpu.roll(tb, 16, 1))
            r = (tb * cos + sw * sin) * invf[:, sl]
            if cb < qd // LANES:
                q_ref[rows, sl] = r.astype(BF16)
            else:
                p = cb - qd // LANES
                a, b = _dup_halves(r)
                kk_ref[rows, (2 * p) * LANES:(2 * p + 1) * LANES] = a.astype(BF16)
                kk_ref[rows, (2 * p + 1) * LANES:(2 * p + 2) * LANES] = b.astype(BF16)
        vt = qkv[:, qkd:].T
        for blk in range(vt.shape[1] // ATTN_BLOCK):
            cols = slice(blk * ATTN_BLOCK, (blk + 1) * ATTN_BLOCK)
            slab = rows.start // ATTN_BLOCK + blk
            for g in range(N_KV_HEADS):
                vt_ref[slab, g * LANES:g * LANES + HEAD_DIM, :] = vt[g * HEAD_DIM:(g + 1) * HEAD_DIM, cols].astype(BF16)
                vt_ref[slab, g * LANES + HEAD_DIM:(g + 1) * LANES, :] = tail


def _qkv0(xin, mods, per_batch, g1, w, gain, gsum, cos, sin, tm):
    b, t, d = xin.shape
    kw = N_KV_HEADS * LANES
    return pl.pallas_call(
        _qkv0_kernel,
        grid=(b, t // tm),
        in_specs=[
            _tok_spec(tm, d),
            _mod_spec(0, per_batch),
            _mod_spec(1, per_batch),
            _const2((1, d)),
            _const2(w.shape),
            _const2(gain.shape),
            _const2(gsum.shape),
            pl.BlockSpec((tm, LANES), lambda b, j: (j, 0)),
            pl.BlockSpec((tm, LANES), lambda b, j: (j, 0)),
        ],
        out_specs=[_tok_spec(tm, d), _tok_spec(tm, kw),
                   pl.BlockSpec((None, tm // ATTN_BLOCK, kw, ATTN_BLOCK), lambda b, j: (b, j, 0, 0))],
        out_shape=[
            jax.ShapeDtypeStruct((b, t, d), BF16),
            jax.ShapeDtypeStruct((b, t, kw), BF16),
            jax.ShapeDtypeStruct((b, t // ATTN_BLOCK, kw, ATTN_BLOCK), BF16),
        ],
        compiler_params=_params(2),
        name="attn_qkv_proj",
    )(xin, mods, mods, g1, w, gain, gsum, cos, sin)


def _band_masks(j, nb, heads):
    shape = (ATTN_BLOCK, heads * ATTN_BLOCK)
    key = lax.broadcasted_iota(jnp.int32, shape, 0)
    qry = lax.broadcasted_iota(jnp.int32, shape, 1) % ATTN_BLOCK
    return (key >= qry) & (j > 0), (key <= qry) & (j < nb - 1)


def _attn_scores(sink_ref, q_ref, key_refs, pr, pairs, masks):
    tq = q_ref.shape[0]
    g = pr // (N_HEADS // N_KV_HEADS // 2)
    gs = slice(g * LANES, (g + 1) * LANES)
    lo = lax.broadcasted_iota(jnp.int32, (tq, LANES), 1) < HEAD_DIM
    zero = jnp.zeros((tq, LANES), BF16)
    stack = []
    for p in range(pr, pr + pairs):
        qb = q_ref[:, p * LANES:(p + 1) * LANES]
        stack += [jnp.where(lo, qb, zero), jnp.where(lo, zero, qb)]
    keys = jnp.concatenate([r[:, gs] for r in key_refs], axis=0) if len(key_refs) > 1 else key_refs[0][:, gs]
    st = _dot_nt(keys, jnp.concatenate(stack, axis=0))
    parts = [st[i * LANES:(i + 1) * LANES, :] for i in range(st.shape[0] // LANES)]
    if masks is not None:
        ia = key_refs[0].shape[0] // LANES
        parts[ia] = jnp.where(masks[0], parts[ia], NEG_INF)
        parts[ia + 2] = jnp.where(masks[1], parts[ia + 2], NEG_INF)
    sink = jnp.concatenate([jnp.full((1, tq), sink_ref[2 * pr + i] * LOG2E, F32) for i in range(2 * pairs)], axis=1)
    mx = parts[0]
    for part in parts[1:]:
        mx = jnp.maximum(mx, part)
    m = jnp.maximum(jnp.max(mx, axis=0, keepdims=True), sink)
    return jnp.concatenate([jnp.exp2(part - m).astype(BF16) for part in parts], axis=0), jnp.exp2(sink - m)


def _attn_values(probs, val_refs, o_ref, pr, pairs):
    pt, p_sink = probs
    tq = o_ref.shape[0]
    g = pr // (N_HEADS // N_KV_HEADS // 2)
    gs = slice(g * LANES, (g + 1) * LANES)
    vt = jnp.concatenate([r[i, gs, :] for r in val_refs for i in range(r.shape[0])], axis=1)
    ot = _dot(vt, pt)
    den = jnp.sum(ot[HEAD_DIM:HEAD_DIM + 8, :], axis=0, keepdims=True) + p_sink
    out = ot[:HEAD_DIM, :] / den
    for i in range(pairs):
        pair = jnp.concatenate([out[:, 2 * i * tq:(2 * i + 1) * tq], out[:, (2 * i + 1) * tq:(2 * i + 2) * tq]], axis=0)
        o_ref[:, (pr + i) * LANES:(pr + i + 1) * LANES] = pair.T.astype(BF16)


def _attn_pipeline(sink_ref, tasks, pairs):
    items = [(task, g) for task in tasks for g in range(0, N_HEADS // 2, pairs)]
    probs = {}
    for step in range(len(items) + ATTN_LOOKAHEAD):
        if step < len(items):
            (q_ref, key_refs, _, _, masks), g = items[step]
            probs[step] = _attn_scores(sink_ref, q_ref, key_refs, g, pairs, masks)
        done = step - ATTN_LOOKAHEAD
        if done >= 0:
            (_, _, val_refs, o_ref, _), g = items[done]
            _attn_values(probs.pop(done), val_refs, o_ref, g, pairs)


def _attn_ctx_kernel(sink_ref, q_ref, kc_ref, vc_ref, o_ref, *, pairs):
    _attn_pipeline(sink_ref, [(q_ref, [kc_ref], [vc_ref], o_ref, None)], pairs)


def _attn_band_kernel(sink_ref, q_ref, kc_ref, k_ref, vc_ref, vt_ref, o_ref, *, nb, pairs):
    tasks = []
    for t in range(q_ref.shape[0] // ATTN_BLOCK):
        j = pl.program_id(1) * (q_ref.shape[0] // ATTN_BLOCK) + t
        rows = pl.ds(t * ATTN_BLOCK, ATTN_BLOCK)
        band = [jnp.maximum(j - 1, 0), j, jnp.minimum(j + 1, nb - 1)]
        key_refs = [kc_ref] + [k_ref.at[pl.ds(pl.multiple_of(blk * ATTN_BLOCK, ATTN_BLOCK), ATTN_BLOCK), :]
                               for blk in band]
        val_refs = [vc_ref] + [vt_ref.at[pl.ds(blk, 1)] for blk in band]
        tasks.append((q_ref.at[rows, :], key_refs, val_refs, o_ref.at[rows, :], _band_masks(j, nb, 2 * pairs)))
    _attn_pipeline(sink_ref, tasks, pairs)


def _attn_x(sink, q, kk, vv, kkc, vvc):
    b, s, d = q.shape
    nb = s // ATTN_BLOCK
    rows = ATTN_BLOCKS_PER_STEP * ATTN_BLOCK
    per_batch = lambda a: pl.BlockSpec((None,) + a.shape[1:], lambda b, j: (b,) + (0,) * (a.ndim - 1))
    blk = pl.BlockSpec((None, rows, d), lambda b, j: (b, j, 0))
    return pl.pallas_call(
        functools.partial(_attn_band_kernel, nb=nb, pairs=ATTN_PAIRS_X),
        grid=(b, s // rows),
        in_specs=[pl.BlockSpec(memory_space=pltpu.SMEM), blk,
                  per_batch(kkc), per_batch(kk), per_batch(vvc), per_batch(vv)],
        out_specs=blk,
        out_shape=jax.ShapeDtypeStruct((b, s, d), BF16),
        compiler_params=_params(2),
        name="attn_latent",
    )(sink, q, kkc, kk, vvc, vv)


def _attn_c(sink, qc, kkc, vvc):
    b, l, d = qc.shape
    return pl.pallas_call(
        functools.partial(_attn_ctx_kernel, pairs=ATTN_PAIRS_C),
        grid=(b,),
        in_specs=[
            pl.BlockSpec(memory_space=pltpu.SMEM),
            pl.BlockSpec((None, l, d), lambda b: (b, 0, 0)),
            pl.BlockSpec((None, l, kkc.shape[2]), lambda b: (b, 0, 0)),
            pl.BlockSpec((None,) + vvc.shape[1:], lambda b: (b, 0, 0, 0)),
        ],
        out_specs=pl.BlockSpec((None, l, d), lambda b: (b, 0, 0)),
        out_shape=jax.ShapeDtypeStruct((b, l, d), BF16),
        compiler_params=_params(1),
        name="attn_context",
    )(sink, qc, kkc, vvc)


def _ffn_tail(x1s, parts, shf_ref, scf_ref, gtf_ref, g2_ref, win_ref, wout_ref, out_ref):
    h2s = [(_rms(x1, g2_ref[...]) * (1.0 + scf_ref[...]) + shf_ref[...]).astype(BF16) for x1 in x1s]
    accs = [None] * len(x1s)
    hidden = {}
    for step in range(len(FF_CHUNKS) + 1):
        if step < len(FF_CHUNKS):
            lo, hi = FF_CHUNKS[step]
            hidden[step] = [(_dot(h2, win_ref[:, lo:hi]), _dot(h2, win_ref[:, D_FF + lo:D_FF + hi])) for h2 in h2s]
        if step > 0:
            lo, hi = FF_CHUNKS[step - 1]
            for i, (gate, up) in enumerate(hidden.pop(step - 1)):
                down = _dot((_silu(gate) * up).astype(BF16), wout_ref[lo:hi, :])
                accs[i] = down if accs[i] is None else accs[i] + down
    for rows, x1, acc in zip(parts, x1s, accs):
        out_ref[rows, :] = x1 + gtf_ref[...] * acc


def _post0_kernel(x_ref, o_ref, gtm_ref, shf_ref, scf_ref, gtf_ref, g2_ref, wo_ref, win_ref, wout_ref, out_ref):
    parts = _row_parts(x_ref, ROW_PARTS)
    mixed = [_dot(o_ref[rows, :], wo_ref[...]) for rows in parts]
    x1s = [x_ref[rows, :] + gtm_ref[...] * a for rows, a in zip(parts, mixed)]
    _ffn_tail(x1s, parts, shf_ref, scf_ref, gtf_ref, g2_ref, win_ref, wout_ref, out_ref)


def _post0(xin, o, mods, per_batch, g2, wo, win, wout, layer, tm):
    b, t, d = xin.shape
    return pl.pallas_call(
        _post0_kernel,
        grid=(b, t // tm),
        in_specs=[
            _tok_spec(tm, d), _tok_spec(tm, d),
            _mod_spec(2, per_batch), _mod_spec(3, per_batch), _mod_spec(4, per_batch), _mod_spec(5, per_batch),
            _const2((1, d)), _const2(wo.shape), _layer_spec(win, layer), _layer_spec(wout, layer),
        ],
        out_specs=_tok_spec(tm, d),
        out_shape=jax.ShapeDtypeStruct((b, t, d), F32),
        compiler_params=_params(2),
        name="attn_out_ffn",
    )(xin, o, mods, mods, mods, mods, g2, wo, win, wout)


def _post1_kernel(x_ref, y_ref, shm_ref, scm_ref, gtm_ref, shf_ref, scf_ref, gtf_ref, g1_ref, g2_ref, gn_ref,
                  wg_ref, wo_ref, win_ref, wout_ref, out_ref):
    parts = _row_parts(x_ref, ROW_PARTS)
    xs = [x_ref[rows, :] for rows in parts]
    h1s = [(_rms(x, g1_ref[...]) * (1.0 + scm_ref[...]) + shm_ref[...]).astype(BF16) for x in xs]
    gates = [_dot(h1, wg_ref[...]) for h1 in h1s]
    gated = []
    for rows, gate in zip(parts, gates):
        normed = []
        for h in range(RET_HEADS):
            o = y_ref[rows, h * RET_V_DIM:(h + 1) * RET_V_DIM].astype(F32)
            d = o - jnp.mean(o, axis=-1, keepdims=True)
            normed.append(d * lax.rsqrt(jnp.mean(d * d, axis=-1, keepdims=True) + EPS))
        gated.append((_silu(gate) * (jnp.concatenate(normed, axis=1) * gn_ref[...])).astype(BF16))
    mixed = [_dot(g, wo_ref[...]) for g in gated]
    x1s = [x + gtm_ref[...] * a for x, a in zip(xs, mixed)]
    _ffn_tail(x1s, parts, shf_ref, scf_ref, gtf_ref, g2_ref, win_ref, wout_ref, out_ref)


def _post1(xin, y, mods, g1, g2, gn, wg, wo, win, wout, layer, tm):
    b, t, d = xin.shape
    return pl.pallas_call(
        _post1_kernel,
        grid=(b, t // tm),
        in_specs=[
            _tok_spec(tm, d), _tok_spec(tm, y.shape[2]),
            *[_mod_spec(k, True) for k in range(6)],
            _const2((1, d)), _const2((1, d)), _const2(gn.shape),
            pl.BlockSpec((d, RET_VWIDTH), lambda b, j: (0, wg.shape[1] // RET_VWIDTH - 1), pipeline_mode=pl.Buffered(1)),
            _const2(wo.shape), _layer_spec(win, layer), _layer_spec(wout, layer),
        ],
        out_specs=_tok_spec(tm, d),
        out_shape=jax.ShapeDtypeStruct((b, t, d), F32),
        compiler_params=_params(2),
        name="ret_out_ffn",
    )(xin, y, mods, mods, mods, mods, mods, mods, g1, g2, gn, wg, wo, win, wout)


def _proj1_kernel(x_ref, sh_ref, sc_ref, g1_ref, w_ref, cos_ref, sin_ref, q_ref, k_ref, v_ref):
    qk_w = 2 * RET_HEADS * RET_QK_DIM
    hb = (_rms(x_ref[...], g1_ref[...]) * (1.0 + sc_ref[...]) + sh_ref[...]).astype(BF16)
    qk = _dot(hb, w_ref[:, :qk_w])
    k_scale = RET_QK_DIM ** -0.5
    for cb in range(qk_w // LANES):
        ts = slice((cb % 2) * LANES, (cb % 2 + 1) * LANES)
        tb = qk[:, cb * LANES:(cb + 1) * LANES]
        r = tb * cos_ref[:, ts] + pltpu.roll(tb, LANES // 2, 1) * sin_ref[:, ts]
        if cb < qk_w // (2 * LANES):
            q_ref[:, cb * LANES:(cb + 1) * LANES] = r.astype(BF16)
        else:
            cc = cb - qk_w // (2 * LANES)
            k_ref[:, cc * LANES:(cc + 1) * LANES] = (r * k_scale).astype(BF16)
    v_ref[...] = _dot(hb, w_ref[:, qk_w:]).astype(BF16)


def _proj1_ctx_kernel(x_ref, sh_ref, sc_ref, g1_ref, w_ref, k_ref, v_ref):
    qw = RET_HEADS * RET_QK_DIM
    hb = (_rms(x_ref[...], g1_ref[...]) * (1.0 + sc_ref[...]) + sh_ref[...]).astype(BF16)
    kv = _dot(hb, w_ref[:, qw:])
    k_ref[...] = (kv[:, :qw] * RET_QK_DIM ** -0.5).astype(BF16)
    v_ref[...] = kv[:, qw:].astype(BF16)


def _proj1(xin, mods, per_batch, g1, w, tables, tm):
    b, t, d = xin.shape
    qw = RET_HEADS * RET_QK_DIM
    specs = [
        _tok_spec(tm, d),
        _mod_spec(0, per_batch), _mod_spec(1, per_batch),
        _const2((1, d)),
        pl.BlockSpec((d, 2 * qw + RET_VWIDTH), lambda b, j: (0, 0), pipeline_mode=pl.Buffered(1)),
    ]
    outs = [(qw, _tok_spec(tm, qw)), (RET_VWIDTH, _tok_spec(tm, RET_VWIDTH))]
    if tables is not None:
        specs += [pl.BlockSpec((tm, 2 * LANES), lambda b, j: (j, 0))] * 2
        outs = [(qw, _tok_spec(tm, qw))] + outs
    return pl.pallas_call(
        _proj1_ctx_kernel if tables is None else _proj1_kernel,
        grid=(b, t // tm),
        in_specs=specs,
        out_specs=[spec for _, spec in outs],
        out_shape=[jax.ShapeDtypeStruct((b, t, width), BF16) for width, _ in outs],
        compiler_params=_params(2),
        name="ret_qkv_proj",
    )(xin, mods, mods, g1, w, *(tables or ()))


def _ret_kernel(dl_ref, q_ref, k_ref, v_ref, kc_ref, vc_ref, y_ref,
                o_scr, sf_scr, sb_scr, comb_scr, qdf_scr, qdb_scr, kdf_scr, kdb_scr, cdf_scr, cdb_scr,
                ctxf_scr, ctxb_scr, *, nc):
    hd = pl.program_id(0)
    c_len = RET_CHUNK
    l_ctx = kc_ref.shape[0]

    @pl.when(pl.program_id(1) == 0)
    def _decay_tables():
        def log_decay(direction, shp):
            logit = jnp.full(shp, dl_ref[direction, hd], F32)
            return jnp.minimum(logit, 0.0) - jnp.log1p(jnp.exp(-jnp.abs(logit)))

        shape = (c_len, c_len)
        n = lax.broadcasted_iota(jnp.int32, shape, 0).astype(F32)
        m = lax.broadcasted_iota(jnp.int32, shape, 1).astype(F32)
        comb_scr[...] = (jnp.where(n >= m, jnp.exp(log_decay(0, shape) * jnp.maximum(n - m, 0.0)), 0.0)
                         + jnp.where(m >= n, jnp.exp(log_decay(1, shape) * jnp.maximum(m - n, 0.0)), 0.0))
        nk = lax.broadcasted_iota(jnp.int32, (c_len, RET_QK_DIM), 0).astype(F32)
        nv = lax.broadcasted_iota(jnp.int32, (c_len, RET_V_DIM), 0).astype(F32)
        qdf_scr[...] = jnp.exp(log_decay(0, nv.shape) * (nv + 1.0))
        kdf_scr[...] = jnp.exp(log_decay(0, nk.shape) * (c_len - 1.0 - nk))
        cdf_scr[...] = jnp.exp(log_decay(0, (1, RET_V_DIM)) * float(c_len))
        qdb_scr[...] = jnp.exp(log_decay(1, nv.shape) * (c_len - nv))
        kdb_scr[...] = jnp.exp(log_decay(1, nk.shape) * nk)
        cdb_scr[...] = jnp.exp(log_decay(1, (1, RET_V_DIM)) * float(c_len))
        tt = lax.broadcasted_iota(jnp.int32, (l_ctx, RET_QK_DIM), 0).astype(F32)
        ctxf_scr[...] = jnp.exp(log_decay(0, tt.shape) * (l_ctx - 1.0 - tt))
        ctxb_scr[...] = jnp.exp(log_decay(1, tt.shape) * tt)

    kc = kc_ref[...].astype(F32)
    vc = vc_ref[...]
    sf_scr[...] = _dot_tn((kc * ctxf_scr[...]).astype(BF16), vc)
    sb_scr[...] = _dot_tn((kc * ctxb_scr[...]).astype(BF16), vc)

    def rows(c):
        return slice(c * c_len, (c + 1) * c_len)

    def masked_scores(c):
        return (_dot_nt(q_ref[rows(c), :], k_ref[rows(c), :]) * comb_scr[...]).astype(BF16)

    def forward(c, scores):
        qc, kx, vx = q_ref[rows(c), :], k_ref[rows(c), :], v_ref[rows(c), :]
        st = sf_scr[...]
        out = _dot(scores, vx) + _dot(qc, st.astype(BF16)) * qdf_scr[...]
        sf_scr[...] = st * cdf_scr[...] + _dot_tn((kx.astype(F32) * kdf_scr[...]).astype(BF16), vx)
        return out

    def backward(c):
        qc, kx, vx = q_ref[rows(c), :], k_ref[rows(c), :], v_ref[rows(c), :]
        st = sb_scr[...]
        out = _dot(qc, st.astype(BF16)) * qdb_scr[...]
        sb_scr[...] = st * cdb_scr[...] + _dot_tn((kx.astype(F32) * kdb_scr[...]).astype(BF16), vx)
        return out

    scores = {}
    for step in range(nc + RET_LOOKAHEAD):
        if step < nc:
            scores[step] = masked_scores(step)
        i = step - RET_LOOKAHEAD
        if i < 0:
            continue
        for c, out in ((i, forward(i, scores.pop(i))), (nc - 1 - i, backward(nc - 1 - i))):
            if i < nc // 2:
                o_scr[rows(c), :] = out
            else:
                y_ref[rows(c), :] = (o_scr[rows(c), :] + out).astype(BF16)


def _retention(dl, q, k, v, kc, vc):
    b, s, _ = q.shape
    l = kc.shape[1]
    nc = s // RET_CHUNK
    assert nc % 2 == 0
    hspec = lambda t, w: pl.BlockSpec((None, t, w), lambda h, b: (b, 0, h))
    return pl.pallas_call(
        functools.partial(_ret_kernel, nc=nc),
        grid=(RET_HEADS, b),
        in_specs=[
            pl.BlockSpec(memory_space=pltpu.SMEM),
            hspec(s, RET_QK_DIM), hspec(s, RET_QK_DIM), hspec(s, RET_V_DIM),
            hspec(l, RET_QK_DIM), hspec(l, RET_V_DIM),
        ],
        out_specs=hspec(s, RET_V_DIM),
        out_shape=jax.ShapeDtypeStruct((b, s, RET_VWIDTH), BF16),
        scratch_shapes=[
            pltpu.VMEM((s, RET_V_DIM), F32),
            pltpu.VMEM((RET_QK_DIM, RET_V_DIM), F32),
            pltpu.VMEM((RET_QK_DIM, RET_V_DIM), F32),
            pltpu.VMEM((RET_CHUNK, RET_CHUNK), F32),
            pltpu.VMEM((RET_CHUNK, RET_V_DIM), F32),
            pltpu.VMEM((RET_CHUNK, RET_V_DIM), F32),
            pltpu.VMEM((RET_CHUNK, RET_QK_DIM), F32),
            pltpu.VMEM((RET_CHUNK, RET_QK_DIM), F32),
            pltpu.VMEM((1, RET_V_DIM), F32),
            pltpu.VMEM((1, RET_V_DIM), F32),
            pltpu.VMEM((l, RET_QK_DIM), F32),
            pltpu.VMEM((l, RET_QK_DIM), F32),
        ],
        compiler_params=_params(2),
        name="retention",
    )(dl, q, k, v, kc, vc)


def _rope_tables(n, head_dim):
    rows = n // GRID_W
    row = jnp.broadcast_to(jnp.arange(rows, dtype=jnp.int32)[:, None], (rows, GRID_W)).reshape(n)
    col = jnp.broadcast_to(jnp.arange(GRID_W, dtype=jnp.int32)[None, :], (rows, GRID_W)).reshape(n)
    axis_dim = head_dim // 2
    inv = ROPE_BASE ** (-jnp.arange(0, axis_dim, 2, dtype=F32) / axis_dim)
    ang_r = row.astype(F32)[:, None] * inv
    ang_c = col.astype(F32)[:, None] * inv
    cos = jnp.concatenate([jnp.cos(ang_r)] * 2 + [jnp.cos(ang_c)] * 2, axis=-1)
    sin = jnp.concatenate([-jnp.sin(ang_r), jnp.sin(ang_r), -jnp.sin(ang_c), jnp.sin(ang_c)], axis=-1)
    return cos, sin


def kernel(x, c, ctx, c_ctx, ada_w, ada_b, norm1_g, norm2_g, ffn_w_in, ffn_w_out, attn_w_qkv, attn_q_norm,
           attn_k_norm, attn_sink, attn_w_o, ret_w_qkvg, ret_decay_logit, ret_gn_g, ret_w_o):
    b, s, d = x.shape
    l = ctx.shape[1]
    tm_x = min(FFN_TILE, s)
    tm_c = min(FFN_TILE, l)
    tp_x = min(PROJ_TILE, s)
    tp_c = min(PROJ_TILE, l)

    rows = -(-(b + 1) // 8) * 8
    cin = jnp.concatenate([c, c_ctx[None, :], jnp.zeros((rows - b - 1, d), F32)], axis=0)
    mods = _mods(cin, ada_w, ada_b)
    mx0 = mods[0, :b].reshape(b, 1, 6 * d)
    mc0 = mods[0, b:b + 1].reshape(1, 1, 6 * d)
    mx1 = mods[1, :b].reshape(b, 1, 6 * d)
    mc1 = mods[1, b:b + 1].reshape(1, 1, 6 * d)

    w_qkv = attn_w_qkv[0].astype(BF16)
    scale = HEAD_DIM ** -0.5 * LOG2E
    gain = jnp.concatenate([jnp.tile(attn_q_norm[0] * scale, N_HEADS), jnp.tile(attn_k_norm[0], N_KV_HEADS)])[None, :]
    lane_head = jnp.arange(MXU_TILE) // HEAD_DIM
    gsum = ((lane_head[:, None] == lane_head[None, :]).astype(F32) / HEAD_DIM).astype(BF16)
    cos64, sin64 = _rope_tables(s, HEAD_DIM)
    cos_x = jnp.tile(cos64, (1, LANES // HEAD_DIM))
    sin_x = jnp.tile(sin64, (1, LANES // HEAD_DIM))
    cos_c = jnp.ones((l, LANES), F32)
    sin_c = jnp.zeros((l, LANES), F32)
    g1 = norm1_g[0][None, :]
    g2 = norm2_g[0][None, :]
    q_x, kk_x, vv_x = _qkv0(x, mx0, True, g1, w_qkv, gain, gsum, cos_x, sin_x, tp_x)
    q_c, kk_c, vv_c = _qkv0(ctx, mc0, False, g1, w_qkv, gain, gsum, cos_c, sin_c, tp_c)
    sink = attn_sink[0].astype(F32)
    o_x = _attn_x(sink, q_x, kk_x, vv_x, kk_c, vv_c)
    o_c = _attn_c(sink, q_c, kk_c, vv_c)
    w_o = attn_w_o[0].astype(BF16)
    w_in = ffn_w_in.astype(BF16)
    w_out = ffn_w_out.astype(BF16)
    x1 = _post0(x, o_x, mx0, True, g2, w_o, w_in, w_out, 0, tm_x)
    y_ctx = _post0(ctx, o_c, mc0, False, g2, w_o, w_in, w_out, 0, tm_c)

    w_qkvg = ret_w_qkvg[0].astype(BF16)
    cos256, sin256 = _rope_tables(s, RET_QK_DIM)
    g1 = norm1_g[1][None, :]
    g2 = norm2_g[1][None, :]
    q1, k1, v1 = _proj1(x1, mx1, True, g1, w_qkvg, (cos256, sin256), tp_x)
    k1c, v1c = _proj1(y_ctx, mc1, False, g1, w_qkvg, None, tp_c)
    y = _retention(ret_decay_logit[0].astype(F32), q1, k1, v1, k1c, v1c)
    return _post1(x1, y, mx1, g1, g2, ret_gn_g[0][None, :], w_qkvg, ret_w_o[0].astype(BF16),
                  w_in, w_out, 1, tm_x)
```

```python
import functools

import jax
import jax.numpy as jnp
from jax import lax
from jax.experimental import pallas as pl
from jax.experimental.pallas import tpu as pltpu

F32 = jnp.float32
BF16 = jnp.bfloat16

D_MODEL = 1024
GRID_W = 64
HEAD_DIM = 64
N_HEADS = D_MODEL // HEAD_DIM
N_KV_HEADS = N_HEADS // 4
WINDOW = 128
ATTN_BLOCK = 128
RET_HEADS = 4
RET_QK_DIM = 256
RET_V_DIM = 512
RET_VWIDTH = 2 * D_MODEL
RET_CHUNK = 256
ATTN_PAIRS_X = 2
ATTN_PAIRS_C = 2
ATTN_BLOCKS_PER_STEP = 4
ATTN_LOOKAHEAD = 2
RET_LOOKAHEAD = 2
D_FF = 2816
MXU_TILE = 256
FF_CHUNKS = ((0, 6 * MXU_TILE), (6 * MXU_TILE, D_FF))
ROW_PARTS = 2
ROPE_BASE = 10000.0
EPS = 1e-6
NEG_INF = -1e30
LOG2E = 1.4426950408889634
LANES = 128

VMEM_LIMIT = 56 * 1024 * 1024
FFN_TILE = 512
PROJ_TILE = 1024


def _dot(a, b):
    return jnp.dot(a, b, preferred_element_type=F32)


def _dot_nt(a, b):
    return lax.dot_general(a, b, (((1,), (1,)), ((), ())), preferred_element_type=F32)


def _dot_tn(a, b):
    return lax.dot_general(a, b, (((0,), (0,)), ((), ())), preferred_element_type=F32)


def _rms(x, g):
    return x * lax.rsqrt(jnp.mean(x * x, axis=-1, keepdims=True) + EPS) * g


def _silu(x):
    return x * jax.nn.sigmoid(x)


def _params(n_axes):
    return pltpu.CompilerParams(dimension_semantics=("arbitrary",) * n_axes, vmem_limit_bytes=VMEM_LIMIT)


def _const2(shape):
    return pl.BlockSpec(shape, lambda b, j: (0,) * len(shape), pipeline_mode=pl.Buffered(1))


def _layer_spec(stacked, layer):
    return pl.BlockSpec((None,) + stacked.shape[1:], lambda b, j: (layer,) + (0,) * (stacked.ndim - 1),
                        pipeline_mode=pl.Buffered(1))


def _mod_spec(k, per_batch):
    if per_batch:
        return pl.BlockSpec((None, 1, D_MODEL), lambda b, j, k=k: (b, 0, k))
    return pl.BlockSpec((None, 1, D_MODEL), lambda b, j, k=k: (0, 0, k))


def _tok_spec(tm, width):
    return pl.BlockSpec((None, tm, width), lambda b, j: (b, j, 0))


def _row_parts(ref, n):
    tm = ref.shape[0]
    n = n if tm % (n * ATTN_BLOCK) == 0 else 1
    return [slice(i * tm // n, (i + 1) * tm // n) for i in range(n)]


def _mods_kernel(c_ref, w_ref, b_ref, o_ref):
    a = _silu(c_ref[...]).astype(BF16)
    o_ref[...] = _dot(a, w_ref[...].astype(BF16)) + b_ref[...]


def _mods(cin, ada_w, ada_b):
    depth, d, n = ada_w.shape
    rows = cin.shape[0]
    tn = 1536
    return pl.pallas_call(
        _mods_kernel,
        grid=(depth, n // tn),
        in_specs=[
            pl.BlockSpec((rows, d), lambda i, j: (0, 0)),
            pl.BlockSpec((None, d, tn), lambda i, j: (i, 0, j)),
            pl.BlockSpec((None, 1, tn), lambda i, j: (i, 0, j)),
        ],
        out_specs=pl.BlockSpec((None, rows, tn), lambda i, j: (i, 0, j)),
        out_shape=jax.ShapeDtypeStruct((depth, rows, n), F32),
        compiler_params=_params(2),
        name="adaln_mods",
    )(cin, ada_w, ada_b.reshape(depth, 1, n))


def _dup_halves(r):
    lane = lax.broadcasted_iota(jnp.int32, r.shape, 1)
    lo = lane < HEAD_DIM
    r64 = pltpu.roll(r, HEAD_DIM, 1)
    return jnp.where(lo, r, r64), jnp.where(lo, r64, r)


def _qkv0_kernel(x_ref, sh_ref, sc_ref, g1_ref, w_ref, gain_ref, gsum_ref, cos_ref, sin_ref, *rest):
    n_side = (len(rest) - 3) // 2
    q_ref, kk_ref, vt_ref = rest[n_side:n_side + 3]
    for src_ref, dst_ref in zip(rest[:n_side], rest[n_side + 3:]):
        dst_ref[...] = src_ref[...].astype(BF16)
    qd = N_HEADS * HEAD_DIM
    qkd = qd + N_KV_HEADS * HEAD_DIM
    parts = _row_parts(x_ref, ROW_PARTS)
    hs = [_rms(x_ref[rows, :], g1_ref[...]) * (1.0 + sc_ref[...]) + sh_ref[...] for rows in parts]
    qkvs = [_dot(h.astype(BF16), w_ref[...]) for h in hs]
    invfs = []
    for qkv in qkvs:
        sq = (qkv[:, :qkd] * qkv[:, :qkd]).astype(BF16)
        ms = jnp.concatenate([_dot(sq[:, c:c + MXU_TILE], gsum_ref[...]) for c in range(0, qkd, MXU_TILE)], axis=1)
        invfs.append(lax.rsqrt(ms + EPS))
    row = lax.broadcasted_iota(jnp.int32, (LANES - HEAD_DIM, ATTN_BLOCK), 0)
    tail = jnp.where(row == 0, 1.0, 0.0).astype(BF16)
    for rows, qkv, invf in zip(parts, qkvs, invfs):
        t = qkv[:, :qkd] * gain_ref[...]
        cos = cos_ref[rows, :]
        sin = sin_ref[rows, :]
        lane = lax.broadcasted_iota(jnp.int32, cos.shape, 1)
        first = (lane % 32) < 16
        for cb in range(qkd // LANES):
            sl = slice(cb * LANES, (cb + 1) * LANES)
            tb = t[:, sl]
            sw = jnp.where(first, pltpu.roll(tb, LANES - 16, 1), pltpu.roll(tb, 16, 1))
            r = (tb * cos + sw * sin) * invf[:, sl]
            if cb < qd // LANES:
                q_ref[rows, sl] = r.astype(BF16)
            else:
                p = cb - qd // LANES
                a, b = _dup_halves(r)
                kk_ref[rows, (2 * p) * LANES:(2 * p + 1) * LANES] = a.astype(BF16)
                kk_ref[rows, (2 * p + 1) * LANES:(2 * p + 2) * LANES] = b.astype(BF16)
        vt = qkv[:, qkd:].T
        for blk in range(vt.shape[1] // ATTN_BLOCK):
            cols = slice(blk * ATTN_BLOCK, (blk + 1) * ATTN_BLOCK)
            slab = rows.start // ATTN_BLOCK + blk
            for g in range(N_KV_HEADS):
                vt_ref[slab, g * LANES:g * LANES + HEAD_DIM, :] = vt[g * HEAD_DIM:(g + 1) * HEAD_DIM, cols].astype(BF16)
                vt_ref[slab, g * LANES + HEAD_DIM:(g + 1) * LANES, :] = tail


def _row_blocks(rows, steps):
    return next(n for n in range(min(steps, rows // 16), 0, -1) if rows % n == 0 and (rows // n) % 16 == 0)


def _qkv0(xin, mods, per_batch, g1, w, gain, gsum, cos, sin, tm, side=()):
    b, t, d = xin.shape
    kw = N_KV_HEADS * LANES
    nj = t // tm
    flat = [a.reshape(-1, a.shape[-1]) for a in side]
    side_specs = []
    for a in flat:
        nblk = _row_blocks(a.shape[0], b * nj)
        side_specs.append(pl.BlockSpec((a.shape[0] // nblk, a.shape[1]),
                                       lambda b, j, nblk=nblk: (jnp.minimum(b * nj + j, nblk - 1), 0)))
    outs = pl.pallas_call(
        _qkv0_kernel,
        grid=(b, nj),
        in_specs=[
            _tok_spec(tm, d),
            _mod_spec(0, per_batch),
            _mod_spec(1, per_batch),
            _const2((1, d)),
            _const2(w.shape),
            _const2(gain.shape),
            _const2(gsum.shape),
            pl.BlockSpec((tm, LANES), lambda b, j: (j, 0)),
            pl.BlockSpec((tm, LANES), lambda b, j: (j, 0)),
            *side_specs,
        ],
        out_specs=[_tok_spec(tm, d), _tok_spec(tm, kw),
                   pl.BlockSpec((None, tm // ATTN_BLOCK, kw, ATTN_BLOCK), lambda b, j: (b, j, 0, 0)), *side_specs],
        out_shape=[
            jax.ShapeDtypeStruct((b, t, d), BF16),
            jax.ShapeDtypeStruct((b, t, kw), BF16),
            jax.ShapeDtypeStruct((b, t // ATTN_BLOCK, kw, ATTN_BLOCK), BF16),
            *[jax.ShapeDtypeStruct(a.shape, BF16) for a in flat],
        ],
        compiler_params=_params(2),
        name="attn_qkv_proj",
    )(xin, mods, mods, g1, w, gain, gsum, cos, sin, *flat)
    return list(outs[:3]) + [o.reshape(a.shape) for o, a in zip(outs[3:], side)]


def _band_masks(j, nb, heads):
    shape = (ATTN_BLOCK, heads * ATTN_BLOCK)
    key = lax.broadcasted_iota(jnp.int32, shape, 0)
    qry = lax.broadcasted_iota(jnp.int32, shape, 1) % ATTN_BLOCK
    return (key >= qry) & (j > 0), (key <= qry) & (j < nb - 1)


def _attn_scores(sink_ref, q_ref, key_refs, pr, pairs, masks):
    tq = q_ref.shape[0]
    g = pr // (N_HEADS // N_KV_HEADS // 2)
    gs = slice(g * LANES, (g + 1) * LANES)
    lo = lax.broadcasted_iota(jnp.int32, (tq, LANES), 1) < HEAD_DIM
    zero = jnp.zeros((tq, LANES), BF16)
    stack = []
    for p in range(pr, pr + pairs):
        qb = q_ref[:, p * LANES:(p + 1) * LANES]
        stack += [jnp.where(lo, qb, zero), jnp.where(lo, zero, qb)]
    keys = jnp.concatenate([r[:, gs] for r in key_refs], axis=0) if len(key_refs) > 1 else key_refs[0][:, gs]
    st = _dot_nt(keys, jnp.concatenate(stack, axis=0))
    parts = [st[i * LANES:(i + 1) * LANES, :] for i in range(st.shape[0] // LANES)]
    if masks is not None:
        ia = key_refs[0].shape[0] // LANES
        parts[ia] = jnp.where(masks[0], parts[ia], NEG_INF)
        parts[ia + 2] = jnp.where(masks[1], parts[ia + 2], NEG_INF)
    sink = jnp.concatenate([jnp.full((1, tq), sink_ref[2 * pr + i] * LOG2E, F32) for i in range(2 * pairs)], axis=1)
    mx = parts[0]
    for part in parts[1:]:
        mx = jnp.maximum(mx, part)
    m = jnp.maximum(jnp.max(mx, axis=0, keepdims=True), sink)
    return jnp.concatenate([jnp.exp2(part - m).astype(BF16) for part in parts], axis=0), jnp.exp2(sink - m)


def _attn_values(probs, val_refs, o_ref, pr, pairs):
    pt, p_sink = probs
    tq = o_ref.shape[0]
    g = pr // (N_HEADS // N_KV_HEADS // 2)
    gs = slice(g * LANES, (g + 1) * LANES)
    vt = jnp.concatenate([r[i, gs, :] for r in val_refs for i in range(r.shape[0])], axis=1)
    ot = _dot(vt, pt)
    den = jnp.sum(ot[HEAD_DIM:HEAD_DIM + 8, :], axis=0, keepdims=True) + p_sink
    out = ot[:HEAD_DIM, :] / den
    for i in range(pairs):
        pair = jnp.concatenate([out[:, 2 * i * tq:(2 * i + 1) * tq], out[:, (2 * i + 1) * tq:(2 * i + 2) * tq]], axis=0)
        o_ref[:, (pr + i) * LANES:(pr + i + 1) * LANES] = pair.T.astype(BF16)


def _attn_pipeline(sink_ref, tasks, pairs):
    items = [(task, g) for task in tasks for g in range(0, N_HEADS // 2, pairs)]
    probs = {}
    for step in range(len(items) + ATTN_LOOKAHEAD):
        if step < len(items):
            (q_ref, key_refs, _, _, masks), g = items[step]
            probs[step] = _attn_scores(sink_ref, q_ref, key_refs, g, pairs, masks)
        done = step - ATTN_LOOKAHEAD
        if done >= 0:
            (_, _, val_refs, o_ref, _), g = items[done]
            _attn_values(probs.pop(done), val_refs, o_ref, g, pairs)


def _attn_ctx_kernel(sink_ref, q_ref, kc_ref, vc_ref, o_ref, *, pairs):
    _attn_pipeline(sink_ref, [(q_ref, [kc_ref], [vc_ref], o_ref, None)], pairs)


def _attn_band_kernel(sink_ref, q_ref, kc_ref, k_ref, vc_ref, vt_ref, o_ref, *, nb, pairs):
    tasks = []
    for t in range(q_ref.shape[0] // ATTN_BLOCK):
        j = pl.program_id(1) * (q_ref.shape[0] // ATTN_BLOCK) + t
        rows = pl.ds(t * ATTN_BLOCK, ATTN_BLOCK)
        band = [jnp.maximum(j - 1, 0), j, jnp.minimum(j + 1, nb - 1)]
        key_refs = [kc_ref] + [k_ref.at[pl.ds(pl.multiple_of(blk * ATTN_BLOCK, ATTN_BLOCK), ATTN_BLOCK), :]
                               for blk in band]
        val_refs = [vc_ref] + [vt_ref.at[pl.ds(blk, 1)] for blk in band]
        tasks.append((q_ref.at[rows, :], key_refs, val_refs, o_ref.at[rows, :], _band_masks(j, nb, 2 * pairs)))
    _attn_pipeline(sink_ref, tasks, pairs)


def _attn_x(sink, q, kk, vv, kkc, vvc):
    b, s, d = q.shape
    nb = s // ATTN_BLOCK
    rows = ATTN_BLOCKS_PER_STEP * ATTN_BLOCK
    per_batch = lambda a: pl.BlockSpec((None,) + a.shape[1:], lambda b, j: (b,) + (0,) * (a.ndim - 1))
    blk = pl.BlockSpec((None, rows, d), lambda b, j: (b, j, 0))
    return pl.pallas_call(
        functools.partial(_attn_band_kernel, nb=nb, pairs=ATTN_PAIRS_X),
        grid=(b, s // rows),
        in_specs=[pl.BlockSpec(memory_space=pltpu.SMEM), blk,
                  per_batch(kkc), per_batch(kk), per_batch(vvc), per_batch(vv)],
        out_specs=blk,
        out_shape=jax.ShapeDtypeStruct((b, s, d), BF16),
        compiler_params=_params(2),
        name="attn_latent",
    )(sink, q, kkc, kk, vvc, vv)


def _attn_c(sink, qc, kkc, vvc):
    b, l, d = qc.shape
    return pl.pallas_call(
        functools.partial(_attn_ctx_kernel, pairs=ATTN_PAIRS_C),
        grid=(b,),
        in_specs=[
            pl.BlockSpec(memory_space=pltpu.SMEM),
            pl.BlockSpec((None, l, d), lambda b: (b, 0, 0)),
            pl.BlockSpec((None, l, kkc.shape[2]), lambda b: (b, 0, 0)),
            pl.BlockSpec((None,) + vvc.shape[1:], lambda b: (b, 0, 0, 0)),
        ],
        out_specs=pl.BlockSpec((None, l, d), lambda b: (b, 0, 0)),
        out_shape=jax.ShapeDtypeStruct((b, l, d), BF16),
        compiler_params=_params(1),
        name="attn_context",
    )(sink, qc, kkc, vvc)


def _ffn_tail(x1s, parts, shf_ref, scf_ref, gtf_ref, g2_ref, win_ref, wout_ref, out_ref):
    h2s = [(_rms(x1, g2_ref[...]) * (1.0 + scf_ref[...]) + shf_ref[...]).astype(BF16) for x1 in x1s]
    accs = [None] * len(x1s)
    hidden = {}
    for step in range(len(FF_CHUNKS) + 1):
        if step < len(FF_CHUNKS):
            lo, hi = FF_CHUNKS[step]
            hidden[step] = [(_dot(h2, win_ref[:, lo:hi]), _dot(h2, win_ref[:, D_FF + lo:D_FF + hi])) for h2 in h2s]
        if step > 0:
            lo, hi = FF_CHUNKS[step - 1]
            for i, (gate, up) in enumerate(hidden.pop(step - 1)):
                down = _dot((_silu(gate) * up).astype(BF16), wout_ref[lo:hi, :])
                accs[i] = down if accs[i] is None else accs[i] + down
    for rows, x1, acc in zip(parts, x1s, accs):
        out_ref[rows, :] = x1 + gtf_ref[...] * acc


def _post0_kernel(x_ref, o_ref, gtm_ref, shf_ref, scf_ref, gtf_ref, g2_ref, wo_ref, win_ref, wout_ref, out_ref):
    parts = _row_parts(x_ref, ROW_PARTS)
    mixed = [_dot(o_ref[rows, :], wo_ref[...]) for rows in parts]
    x1s = [x_ref[rows, :] + gtm_ref[...] * a for rows, a in zip(parts, mixed)]
    _ffn_tail(x1s, parts, shf_ref, scf_ref, gtf_ref, g2_ref, win_ref, wout_ref, out_ref)


def _post0(xin, o, mods, per_batch, g2, wo, win, wout, layer, tm):
    b, t, d = xin.shape
    return pl.pallas_call(
        _post0_kernel,
        grid=(b, t // tm),
        in_specs=[
            _tok_spec(tm, d), _tok_spec(tm, d),
            _mod_spec(2, per_batch), _mod_spec(3, per_batch), _mod_spec(4, per_batch), _mod_spec(5, per_batch),
            _const2((1, d)), _const2(wo.shape), _layer_spec(win, layer), _layer_spec(wout, layer),
        ],
        out_specs=_tok_spec(tm, d),
        out_shape=jax.ShapeDtypeStruct((b, t, d), F32),
        compiler_params=_params(2),
        name="attn_out_ffn",
    )(xin, o, mods, mods, mods, mods, g2, wo, win, wout)


def _post1_kernel(x_ref, y_ref, shm_ref, scm_ref, gtm_ref, shf_ref, scf_ref, gtf_ref, g1_ref, g2_ref, gn_ref,
                  wg_ref, wo_ref, win_ref, wout_ref, out_ref):
    parts = _row_parts(x_ref, ROW_PARTS)
    xs = [x_ref[rows, :] for rows in parts]
    h1s = [(_rms(x, g1_ref[...]) * (1.0 + scm_ref[...]) + shm_ref[...]).astype(BF16) for x in xs]
    gates = [_dot(h1, wg_ref[...]) for h1 in h1s]
    gated = []
    for rows, gate in zip(parts, gates):
        normed = []
        for h in range(RET_HEADS):
            o = y_ref[rows, h * RET_V_DIM:(h + 1) * RET_V_DIM].astype(F32)
            d = o - jnp.mean(o, axis=-1, keepdims=True)
            normed.append(d * lax.rsqrt(jnp.mean(d * d, axis=-1, keepdims=True) + EPS))
        gated.append((_silu(gate) * (jnp.concatenate(normed, axis=1) * gn_ref[...])).astype(BF16))
    mixed = [_dot(g, wo_ref[...]) for g in gated]
    x1s = [x + gtm_ref[...] * a for x, a in zip(xs, mixed)]
    _ffn_tail(x1s, parts, shf_ref, scf_ref, gtf_ref, g2_ref, win_ref, wout_ref, out_ref)


def _post1(xin, y, mods, g1, g2, gn, wg, wo, win, wout, layer, tm):
    b, t, d = xin.shape
    return pl.pallas_call(
        _post1_kernel,
        grid=(b, t // tm),
        in_specs=[
            _tok_spec(tm, d), _tok_spec(tm, y.shape[2]),
            *[_mod_spec(k, True) for k in range(6)],
            _const2((1, d)), _const2((1, d)), _const2(gn.shape),
            pl.BlockSpec((d, RET_VWIDTH), lambda b, j: (0, wg.shape[1] // RET_VWIDTH - 1), pipeline_mode=pl.Buffered(1)),
            _const2(wo.shape), _layer_spec(win, layer), _layer_spec(wout, layer),
        ],
        out_specs=_tok_spec(tm, d),
        out_shape=jax.ShapeDtypeStruct((b, t, d), F32),
        compiler_params=_params(2),
        name="ret_out_ffn",
    )(xin, y, mods, mods, mods, mods, mods, mods, g1, g2, gn, wg, wo, win, wout)


def _proj1_kernel(x_ref, sh_ref, sc_ref, g1_ref, w_ref, cos_ref, sin_ref, q_ref, k_ref, v_ref):
    qk_w = 2 * RET_HEADS * RET_QK_DIM
    hb = (_rms(x_ref[...], g1_ref[...]) * (1.0 + sc_ref[...]) + sh_ref[...]).astype(BF16)
    qk = _dot(hb, w_ref[:, :qk_w])
    k_scale = RET_QK_DIM ** -0.5
    for cb in range(qk_w // LANES):
        ts = slice((cb % 2) * LANES, (cb % 2 + 1) * LANES)
        tb = qk[:, cb * LANES:(cb + 1) * LANES]
        r = tb * cos_ref[:, ts] + pltpu.roll(tb, LANES // 2, 1) * sin_ref[:, ts]
        if cb < qk_w // (2 * LANES):
            q_ref[:, cb * LANES:(cb + 1) * LANES] = r.astype(BF16)
        else:
            cc = cb - qk_w // (2 * LANES)
            k_ref[:, cc * LANES:(cc + 1) * LANES] = (r * k_scale).astype(BF16)
    v_ref[...] = _dot(hb, w_ref[:, qk_w:]).astype(BF16)


def _proj1_ctx_kernel(x_ref, sh_ref, sc_ref, g1_ref, w_ref, k_ref, v_ref):
    qw = RET_HEADS * RET_QK_DIM
    hb = (_rms(x_ref[...], g1_ref[...]) * (1.0 + sc_ref[...]) + sh_ref[...]).astype(BF16)
    kv = _dot(hb, w_ref[:, qw:])
    k_ref[...] = (kv[:, :qw] * RET_QK_DIM ** -0.5).astype(BF16)
    v_ref[...] = kv[:, qw:].astype(BF16)


def _proj1(xin, mods, per_batch, g1, w, tables, tm):
    b, t, d = xin.shape
    qw = RET_HEADS * RET_QK_DIM
    specs = [
        _tok_spec(tm, d),
        _mod_spec(0, per_batch), _mod_spec(1, per_batch),
        _const2((1, d)),
        pl.BlockSpec((d, 2 * qw + RET_VWIDTH), lambda b, j: (0, 0), pipeline_mode=pl.Buffered(1)),
    ]
    outs = [(qw, _tok_spec(tm, qw)), (RET_VWIDTH, _tok_spec(tm, RET_VWIDTH))]
    if tables is not None:
        specs += [pl.BlockSpec((tm, 2 * LANES), lambda b, j: (j, 0))] * 2
        outs = [(qw, _tok_spec(tm, qw))] + outs
    return pl.pallas_call(
        _proj1_ctx_kernel if tables is None else _proj1_kernel,
        grid=(b, t // tm),
        in_specs=specs,
        out_specs=[spec for _, spec in outs],
        out_shape=[jax.ShapeDtypeStruct((b, t, width), BF16) for width, _ in outs],
        compiler_params=_params(2),
        name="ret_qkv_proj",
    )(xin, mods, mods, g1, w, *(tables or ()))


def _ret_kernel(dl_ref, q_ref, k_ref, v_ref, kc_ref, vc_ref, y_ref,
                o_scr, sf_scr, sb_scr, comb_scr, qdf_scr, qdb_scr, kdf_scr, kdb_scr, cdf_scr, cdb_scr,
                ctxf_scr, ctxb_scr, *, nc):
    hd = pl.program_id(0)
    c_len = RET_CHUNK
    l_ctx = kc_ref.shape[0]

    @pl.when(pl.program_id(1) == 0)
    def _decay_tables():
        def log_decay(direction, shp):
            logit = jnp.full(shp, dl_ref[direction, hd], F32)
            return jnp.minimum(logit, 0.0) - jnp.log1p(jnp.exp(-jnp.abs(logit)))

        shape = (c_len, c_len)
        n = lax.broadcasted_iota(jnp.int32, shape, 0).astype(F32)
        m = lax.broadcasted_iota(jnp.int32, shape, 1).astype(F32)
        comb_scr[...] = (jnp.where(n >= m, jnp.exp(log_decay(0, shape) * jnp.maximum(n - m, 0.0)), 0.0)
                         + jnp.where(m >= n, jnp.exp(log_decay(1, shape) * jnp.maximum(m - n, 0.0)), 0.0))
        nk = lax.broadcasted_iota(jnp.int32, (c_len, RET_QK_DIM), 0).astype(F32)
        nv = lax.broadcasted_iota(jnp.int32, (c_len, RET_V_DIM), 0).astype(F32)
        qdf_scr[...] = jnp.exp(log_decay(0, nv.shape) * (nv + 1.0))
        kdf_scr[...] = jnp.exp(log_decay(0, nk.shape) * (c_len - 1.0 - nk))
        cdf_scr[...] = jnp.exp(log_decay(0, (1, RET_V_DIM)) * float(c_len))
        qdb_scr[...] = jnp.exp(log_decay(1, nv.shape) * (c_len - nv))
        kdb_scr[...] = jnp.exp(log_decay(1, nk.shape) * nk)
        cdb_scr[...] = jnp.exp(log_decay(1, (1, RET_V_DIM)) * float(c_len))
        tt = lax.broadcasted_iota(jnp.int32, (l_ctx, RET_QK_DIM), 0).astype(F32)
        ctxf_scr[...] = jnp.exp(log_decay(0, tt.shape) * (l_ctx - 1.0 - tt))
        ctxb_scr[...] = jnp.exp(log_decay(1, tt.shape) * tt)

    kc = kc_ref[...].astype(F32)
    vc = vc_ref[...]
    sf_scr[...] = _dot_tn((kc * ctxf_scr[...]).astype(BF16), vc)
    sb_scr[...] = _dot_tn((kc * ctxb_scr[...]).astype(BF16), vc)

    def rows(c):
        return slice(c * c_len, (c + 1) * c_len)

    def masked_scores(c):
        return (_dot_nt(q_ref[rows(c), :], k_ref[rows(c), :]) * comb_scr[...]).astype(BF16)

    def forward(c, scores):
        qc, kx, vx = q_ref[rows(c), :], k_ref[rows(c), :], v_ref[rows(c), :]
        st = sf_scr[...]
        out = _dot(scores, vx) + _dot(qc, st.astype(BF16)) * qdf_scr[...]
        sf_scr[...] = st * cdf_scr[...] + _dot_tn((kx.astype(F32) * kdf_scr[...]).astype(BF16), vx)
        return out

    def backward(c):
        qc, kx, vx = q_ref[rows(c), :], k_ref[rows(c), :], v_ref[rows(c), :]
        st = sb_scr[...]
        out = _dot(qc, st.astype(BF16)) * qdb_scr[...]
        sb_scr[...] = st * cdb_scr[...] + _dot_tn((kx.astype(F32) * kdb_scr[...]).astype(BF16), vx)
        return out

    scores = {}
    for step in range(nc + RET_LOOKAHEAD):
        if step < nc:
            scores[step] = masked_scores(step)
        i = step - RET_LOOKAHEAD
        if i < 0:
            continue
        for c, out in ((i, forward(i, scores.pop(i))), (nc - 1 - i, backward(nc - 1 - i))):
            if i < nc // 2:
                o_scr[rows(c), :] = out
            else:
                y_ref[rows(c), :] = (o_scr[rows(c), :] + out).astype(BF16)


def _retention(dl, q, k, v, kc, vc):
    b, s, _ = q.shape
    l = kc.shape[1]
    nc = s // RET_CHUNK
    assert nc % 2 == 0
    hspec = lambda t, w: pl.BlockSpec((None, t, w), lambda h, b: (b, 0, h))
    return pl.pallas_call(
        functools.partial(_ret_kernel, nc=nc),
        grid=(RET_HEADS, b),
        in_specs=[
            pl.BlockSpec(memory_space=pltpu.SMEM),
            hspec(s, RET_QK_DIM), hspec(s, RET_QK_DIM), hspec(s, RET_V_DIM),
            hspec(l, RET_QK_DIM), hspec(l, RET_V_DIM),
        ],
        out_specs=hspec(s, RET_V_DIM),
        out_shape=jax.ShapeDtypeStruct((b, s, RET_VWIDTH), BF16),
        scratch_shapes=[
            pltpu.VMEM((s, RET_V_DIM), F32),
            pltpu.VMEM((RET_QK_DIM, RET_V_DIM), F32),
            pltpu.VMEM((RET_QK_DIM, RET_V_DIM), F32),
            pltpu.VMEM((RET_CHUNK, RET_CHUNK), F32),
            pltpu.VMEM((RET_CHUNK, RET_V_DIM), F32),
            pltpu.VMEM((RET_CHUNK, RET_V_DIM), F32),
            pltpu.VMEM((RET_CHUNK, RET_QK_DIM), F32),
            pltpu.VMEM((RET_CHUNK, RET_QK_DIM), F32),
            pltpu.VMEM((1, RET_V_DIM), F32),
            pltpu.VMEM((1, RET_V_DIM), F32),
            pltpu.VMEM((l, RET_QK_DIM), F32),
            pltpu.VMEM((l, RET_QK_DIM), F32),
        ],
        compiler_params=_params(2),
        name="retention",
    )(dl, q, k, v, kc, vc)


def _rope_tables(n, head_dim):
    rows = n // GRID_W
    row = jnp.broadcast_to(jnp.arange(rows, dtype=jnp.int32)[:, None], (rows, GRID_W)).reshape(n)
    col = jnp.broadcast_to(jnp.arange(GRID_W, dtype=jnp.int32)[None, :], (rows, GRID_W)).reshape(n)
    axis_dim = head_dim // 2
    inv = ROPE_BASE ** (-jnp.arange(0, axis_dim, 2, dtype=F32) / axis_dim)
    ang_r = row.astype(F32)[:, None] * inv
    ang_c = col.astype(F32)[:, None] * inv
    cos = jnp.concatenate([jnp.cos(ang_r)] * 2 + [jnp.cos(ang_c)] * 2, axis=-1)
    sin = jnp.concatenate([-jnp.sin(ang_r), jnp.sin(ang_r), -jnp.sin(ang_c), jnp.sin(ang_c)], axis=-1)
    return cos, sin


def kernel(x, c, ctx, c_ctx, ada_w, ada_b, norm1_g, norm2_g, ffn_w_in, ffn_w_out, attn_w_qkv, attn_q_norm,
           attn_k_norm, attn_sink, attn_w_o, ret_w_qkvg, ret_decay_logit, ret_gn_g, ret_w_o):
    b, s, d = x.shape
    l = ctx.shape[1]
    tm_x = min(FFN_TILE, s)
    tm_c = min(FFN_TILE, l)
    tp_x = min(PROJ_TILE, s)
    tp_c = min(PROJ_TILE, l)

    rows = -(-(b + 1) // 8) * 8
    cin = jnp.concatenate([c, c_ctx[None, :], jnp.zeros((rows - b - 1, d), F32)], axis=0)
    mods = _mods(cin, ada_w, ada_b)
    mx0 = mods[0, :b].reshape(b, 1, 6 * d)
    mc0 = mods[0, b:b + 1].reshape(1, 1, 6 * d)
    mx1 = mods[1, :b].reshape(b, 1, 6 * d)
    mc1 = mods[1, b:b + 1].reshape(1, 1, 6 * d)

    w_qkv = attn_w_qkv[0].astype(BF16)
    scale = HEAD_DIM ** -0.5 * LOG2E
    gain = jnp.concatenate([jnp.tile(attn_q_norm[0] * scale, N_HEADS), jnp.tile(attn_k_norm[0], N_KV_HEADS)])[None, :]
    lane_head = jnp.arange(MXU_TILE) // HEAD_DIM
    gsum = ((lane_head[:, None] == lane_head[None, :]).astype(F32) / HEAD_DIM).astype(BF16)
    cos64, sin64 = _rope_tables(s, HEAD_DIM)
    cos_x = jnp.tile(cos64, (1, LANES // HEAD_DIM))
    sin_x = jnp.tile(sin64, (1, LANES // HEAD_DIM))
    cos_c = jnp.ones((l, LANES), F32)
    sin_c = jnp.zeros((l, LANES), F32)
    g1 = norm1_g[0][None, :]
    g2 = norm2_g[0][None, :]
    q_x, kk_x, vv_x, w_o, w_in, w_out, w_qkvg, w_ro = _qkv0(
        x, mx0, True, g1, w_qkv, gain, gsum, cos_x, sin_x, tp_x,
        side=(attn_w_o[0], ffn_w_in, ffn_w_out, ret_w_qkvg[0], ret_w_o[0]))
    q_c, kk_c, vv_c = _qkv0(ctx, mc0, False, g1, w_qkv, gain, gsum, cos_c, sin_c, tp_c)
    sink = attn_sink[0].astype(F32)
    o_x = _attn_x(sink, q_x, kk_x, vv_x, kk_c, vv_c)
    o_c = _attn_c(sink, q_c, kk_c, vv_c)
    x1 = _post0(x, o_x, mx0, True, g2, w_o, w_in, w_out, 0, tm_x)
    y_ctx = _post0(ctx, o_c, mc0, False, g2, w_o, w_in, w_out, 0, tm_c)

    cos256, sin256 = _rope_tables(s, RET_QK_DIM)
    g1 = norm1_g[1][None, :]
    g2 = norm2_g[1][None, :]
    q1, k1, v1 = _proj1(x1, mx1, True, g1, w_qkvg, (cos256, sin256), tp_x)
    k1c, v1c = _proj1(y_ctx, mc1, False, g1, w_qkvg, None, tp_c)
    y = _retention(ret_decay_logit[0].astype(F32), q1, k1, v1, k1c, v1c)
    return _post1(x1, y, mx1, g1, g2, ret_gn_g[0][None, :], w_qkvg, w_ro,
                  w_in, w_out, 1, tm_x)
```

```python
import functools

import jax
import jax.numpy as jnp
from jax import lax
from jax.experimental import pallas as pl
from jax.experimental.pallas import tpu as pltpu

F32 = jnp.float32
BF16 = jnp.bfloat16

D_MODEL = 1024
GRID_W = 64
HEAD_DIM = 64
N_HEADS = D_MODEL // HEAD_DIM
N_KV_HEADS = N_HEADS // 4
WINDOW = 128
ATTN_BLOCK = 128
RET_HEADS = 4
RET_QK_DIM = 256
RET_V_DIM = 512
RET_VWIDTH = 2 * D_MODEL
RET_CHUNK = 256
ATTN_PAIRS_X = 2
ATTN_PAIRS_C = 2
ATTN_BLOCKS_PER_STEP = 4
ATTN_LOOKAHEAD = 2
RET_SEQS = 2
RET_LOOKAHEAD = 2
D_FF = 2816
MXU_TILE = 256
FF_CHUNKS = ((0, 6 * MXU_TILE), (6 * MXU_TILE, D_FF))
ROW_PARTS = 2
ROPE_BASE = 10000.0
EPS = 1e-6
NEG_INF = -1e30
LOG2E = 1.4426950408889634
LANES = 128

VMEM_LIMIT = 56 * 1024 * 1024
FFN_TILE = 512
PROJ_TILE = 1024


def _dot(a, b):
    return jnp.dot(a, b, preferred_element_type=F32)


def _dot_nt(a, b):
    return lax.dot_general(a, b, (((1,), (1,)), ((), ())), preferred_element_type=F32)


def _dot_tn(a, b):
    return lax.dot_general(a, b, (((0,), (0,)), ((), ())), preferred_element_type=F32)


def _rms(x, g):
    return x * lax.rsqrt(jnp.mean(x * x, axis=-1, keepdims=True) + EPS) * g


def _silu(x):
    return x * jax.nn.sigmoid(x)


def _params(n_axes):
    return pltpu.CompilerParams(dimension_semantics=("arbitrary",) * n_axes, vmem_limit_bytes=VMEM_LIMIT)


def _const2(shape):
    return pl.BlockSpec(shape, lambda b, j: (0,) * len(shape), pipeline_mode=pl.Buffered(1))


def _layer_spec(stacked, layer):
    return pl.BlockSpec((None,) + stacked.shape[1:], lambda b, j: (layer,) + (0,) * (stacked.ndim - 1),
                        pipeline_mode=pl.Buffered(1))


def _mod_spec(k, per_batch):
    if per_batch:
        return pl.BlockSpec((None, 1, D_MODEL), lambda b, j, k=k: (b, 0, k))
    return pl.BlockSpec((None, 1, D_MODEL), lambda b, j, k=k: (0, 0, k))


def _tok_spec(tm, width):
    return pl.BlockSpec((None, tm, width), lambda b, j: (b, j, 0))


def _row_parts(ref, n):
    tm = ref.shape[0]
    n = n if tm % (n * ATTN_BLOCK) == 0 else 1
    return [slice(i * tm // n, (i + 1) * tm // n) for i in range(n)]


def _mods_kernel(c_ref, w_ref, b_ref, o_ref):
    a = _silu(c_ref[...]).astype(BF16)
    o_ref[...] = _dot(a, w_ref[...].astype(BF16)) + b_ref[...]


def _mods(cin, ada_w, ada_b):
    depth, d, n = ada_w.shape
    rows = cin.shape[0]
    tn = 1536
    return pl.pallas_call(
        _mods_kernel,
        grid=(depth, n // tn),
        in_specs=[
            pl.BlockSpec((rows, d), lambda i, j: (0, 0)),
            pl.BlockSpec((None, d, tn), lambda i, j: (i, 0, j)),
            pl.BlockSpec((None, 1, tn), lambda i, j: (i, 0, j)),
        ],
        out_specs=pl.BlockSpec((None, rows, tn), lambda i, j: (i, 0, j)),
        out_shape=jax.ShapeDtypeStruct((depth, rows, n), F32),
        compiler_params=_params(2),
        name="adaln_mods",
    )(cin, ada_w, ada_b.reshape(depth, 1, n))


def _dup_halves(r):
    lane = lax.broadcasted_iota(jnp.int32, r.shape, 1)
    lo = lane < HEAD_DIM
    r64 = pltpu.roll(r, HEAD_DIM, 1)
    return jnp.where(lo, r, r64), jnp.where(lo, r64, r)


def _qkv0_kernel(x_ref, sh_ref, sc_ref, g1_ref, w_ref, gain_ref, gsum_ref, cos_ref, sin_ref, *rest):
    n_side = (len(rest) - 3) // 2
    q_ref, kk_ref, vt_ref = rest[n_side:n_side + 3]
    for src_ref, dst_ref in zip(rest[:n_side], rest[n_side + 3:]):
        dst_ref[...] = src_ref[...].astype(BF16)
    qd = N_HEADS * HEAD_DIM
    qkd = qd + N_KV_HEADS * HEAD_DIM
    parts = _row_parts(x_ref, ROW_PARTS)
    hs = [_rms(x_ref[rows, :], g1_ref[...]) * (1.0 + sc_ref[...]) + sh_ref[...] for rows in parts]
    qkvs = [_dot(h.astype(BF16), w_ref[...]) for h in hs]
    invfs = []
    for qkv in qkvs:
        sq = (qkv[:, :qkd] * qkv[:, :qkd]).astype(BF16)
        ms = jnp.concatenate([_dot(sq[:, c:c + MXU_TILE], gsum_ref[...]) for c in range(0, qkd, MXU_TILE)], axis=1)
        invfs.append(lax.rsqrt(ms + EPS))
    row = lax.broadcasted_iota(jnp.int32, (LANES - HEAD_DIM, ATTN_BLOCK), 0)
    tail = jnp.where(row == 0, 1.0, 0.0).astype(BF16)
    for rows, qkv, invf in zip(parts, qkvs, invfs):
        t = qkv[:, :qkd] * gain_ref[...]
        cos = cos_ref[rows, :]
        sin = sin_ref[rows, :]
        lane = lax.broadcasted_iota(jnp.int32, cos.shape, 1)
        first = (lane % 32) < 16
        for cb in range(qkd // LANES):
            sl = slice(cb * LANES, (cb + 1) * LANES)
            tb = t[:, sl]
            sw = jnp.where(first, pltpu.roll(tb, LANES - 16, 1), pltpu.roll(tb, 16, 1))
            r = (tb * cos + sw * sin) * invf[:, sl]
            if cb < qd // LANES:
                q_ref[rows, sl] = r.astype(BF16)
            else:
                p = cb - qd // LANES
                a, b = _dup_halves(r)
                kk_ref[rows, (2 * p) * LANES:(2 * p + 1) * LANES] = a.astype(BF16)
                kk_ref[rows, (2 * p + 1) * LANES:(2 * p + 2) * LANES] = b.astype(BF16)
        vt = qkv[:, qkd:].T
        for blk in range(vt.shape[1] // ATTN_BLOCK):
            cols = slice(blk * ATTN_BLOCK, (blk + 1) * ATTN_BLOCK)
            slab = rows.start // ATTN_BLOCK + blk
            for g in range(N_KV_HEADS):
                vt_ref[slab, g * LANES:g * LANES + HEAD_DIM, :] = vt[g * HEAD_DIM:(g + 1) * HEAD_DIM, cols].astype(BF16)
                vt_ref[slab, g * LANES + HEAD_DIM:(g + 1) * LANES, :] = tail


def _row_blocks(rows, steps):
    return next(n for n in range(min(steps, rows // 16), 0, -1) if rows % n == 0 and (rows // n) % 16 == 0)


def _qkv0(xin, mods, per_batch, g1, w, gain, gsum, cos, sin, tm, side=()):
    b, t, d = xin.shape
    kw = N_KV_HEADS * LANES
    nj = t // tm
    flat = [a.reshape(-1, a.shape[-1]) for a in side]
    side_specs = []
    for a in flat:
        nblk = _row_blocks(a.shape[0], b * nj)
        side_specs.append(pl.BlockSpec((a.shape[0] // nblk, a.shape[1]),
                                       lambda b, j, nblk=nblk: (jnp.minimum(b * nj + j, nblk - 1), 0)))
    outs = pl.pallas_call(
        _qkv0_kernel,
        grid=(b, nj),
        in_specs=[
            _tok_spec(tm, d),
            _mod_spec(0, per_batch),
            _mod_spec(1, per_batch),
            _const2((1, d)),
            _const2(w.shape),
            _const2(gain.shape),
            _const2(gsum.shape),
            pl.BlockSpec((tm, LANES), lambda b, j: (j, 0)),
            pl.BlockSpec((tm, LANES), lambda b, j: (j, 0)),
            *side_specs,
        ],
        out_specs=[_tok_spec(tm, d), _tok_spec(tm, kw),
                   pl.BlockSpec((None, tm // ATTN_BLOCK, kw, ATTN_BLOCK), lambda b, j: (b, j, 0, 0)), *side_specs],
        out_shape=[
            jax.ShapeDtypeStruct((b, t, d), BF16),
            jax.ShapeDtypeStruct((b, t, kw), BF16),
            jax.ShapeDtypeStruct((b, t // ATTN_BLOCK, kw, ATTN_BLOCK), BF16),
            *[jax.ShapeDtypeStruct(a.shape, BF16) for a in flat],
        ],
        compiler_params=_params(2),
        name="attn_qkv_proj",
    )(xin, mods, mods, g1, w, gain, gsum, cos, sin, *flat)
    return list(outs[:3]) + [o.reshape(a.shape) for o, a in zip(outs[3:], side)]


def _band_masks(j, nb, heads):
    shape = (ATTN_BLOCK, heads * ATTN_BLOCK)
    key = lax.broadcasted_iota(jnp.int32, shape, 0)
    qry = lax.broadcasted_iota(jnp.int32, shape, 1) % ATTN_BLOCK
    return (key >= qry) & (j > 0), (key <= qry) & (j < nb - 1)


def _attn_scores(sink_ref, q_ref, key_refs, pr, pairs, masks):
    tq = q_ref.shape[0]
    g = pr // (N_HEADS // N_KV_HEADS // 2)
    gs = slice(g * LANES, (g + 1) * LANES)
    lo = lax.broadcasted_iota(jnp.int32, (tq, LANES), 1) < HEAD_DIM
    zero = jnp.zeros((tq, LANES), BF16)
    stack = []
    for p in range(pr, pr + pairs):
        qb = q_ref[:, p * LANES:(p + 1) * LANES]
        stack += [jnp.where(lo, qb, zero), jnp.where(lo, zero, qb)]
    keys = jnp.concatenate([r[:, gs] for r in key_refs], axis=0) if len(key_refs) > 1 else key_refs[0][:, gs]
    st = _dot_nt(keys, jnp.concatenate(stack, axis=0))
    parts = [st[i * LANES:(i + 1) * LANES, :] for i in range(st.shape[0] // LANES)]
    if masks is not None:
        ia = key_refs[0].shape[0] // LANES
        parts[ia] = jnp.where(masks[0], parts[ia], NEG_INF)
        parts[ia + 2] = jnp.where(masks[1], parts[ia + 2], NEG_INF)
    sink = jnp.concatenate([jnp.full((1, tq), sink_ref[2 * pr + i] * LOG2E, F32) for i in range(2 * pairs)], axis=1)
    mx = parts[0]
    for part in parts[1:]:
        mx = jnp.maximum(mx, part)
    m = jnp.maximum(jnp.max(mx, axis=0, keepdims=True), sink)
    return jnp.concatenate([jnp.exp2(part - m).astype(BF16) for part in parts], axis=0), jnp.exp2(sink - m)


def _attn_values(probs, val_refs, o_ref, pr, pairs):
    pt, p_sink = probs
    tq = o_ref.shape[0]
    g = pr // (N_HEADS // N_KV_HEADS // 2)
    gs = slice(g * LANES, (g + 1) * LANES)
    vt = jnp.concatenate([r[i, gs, :] for r in val_refs for i in range(r.shape[0])], axis=1)
    ot = _dot(vt, pt)
    den = jnp.sum(ot[HEAD_DIM:HEAD_DIM + 8, :], axis=0, keepdims=True) + p_sink
    out = ot[:HEAD_DIM, :] / den
    for i in range(pairs):
        pair = jnp.concatenate([out[:, 2 * i * tq:(2 * i + 1) * tq], out[:, (2 * i + 1) * tq:(2 * i + 2) * tq]], axis=0)
        o_ref[:, (pr + i) * LANES:(pr + i + 1) * LANES] = pair.T.astype(BF16)


def _attn_pipeline(sink_ref, tasks, pairs):
    items = [(task, g) for task in tasks for g in range(0, N_HEADS // 2, pairs)]
    probs = {}
    for step in range(len(items) + ATTN_LOOKAHEAD):
        if step < len(items):
            (q_ref, key_refs, _, _, masks), g = items[step]
            probs[step] = _attn_scores(sink_ref, q_ref, key_refs, g, pairs, masks)
        done = step - ATTN_LOOKAHEAD
        if done >= 0:
            (_, _, val_refs, o_ref, _), g = items[done]
            _attn_values(probs.pop(done), val_refs, o_ref, g, pairs)


def _attn_ctx_kernel(sink_ref, q_ref, kc_ref, vc_ref, o_ref, *, pairs):
    _attn_pipeline(sink_ref, [(q_ref, [kc_ref], [vc_ref], o_ref, None)], pairs)


def _attn_band_kernel(sink_ref, q_ref, kc_ref, k_ref, vc_ref, vt_ref, o_ref, *, nb, pairs):
    tasks = []
    for t in range(q_ref.shape[0] // ATTN_BLOCK):
        j = pl.program_id(1) * (q_ref.shape[0] // ATTN_BLOCK) + t
        rows = pl.ds(t * ATTN_BLOCK, ATTN_BLOCK)
        band = [jnp.maximum(j - 1, 0), j, jnp.minimum(j + 1, nb - 1)]
        key_refs = [kc_ref] + [k_ref.at[pl.ds(pl.multiple_of(blk * ATTN_BLOCK, ATTN_BLOCK), ATTN_BLOCK), :]
                               for blk in band]
        val_refs = [vc_ref] + [vt_ref.at[pl.ds(blk, 1)] for blk in band]
        tasks.append((q_ref.at[rows, :], key_refs, val_refs, o_ref.at[rows, :], _band_masks(j, nb, 2 * pairs)))
    _attn_pipeline(sink_ref, tasks, pairs)


def _attn_x(sink, q, kk, vv, kkc, vvc):
    b, s, d = q.shape
    nb = s // ATTN_BLOCK
    rows = ATTN_BLOCKS_PER_STEP * ATTN_BLOCK
    per_batch = lambda a: pl.BlockSpec((None,) + a.shape[1:], lambda b, j: (b,) + (0,) * (a.ndim - 1))
    blk = pl.BlockSpec((None, rows, d), lambda b, j: (b, j, 0))
    return pl.pallas_call(
        functools.partial(_attn_band_kernel, nb=nb, pairs=ATTN_PAIRS_X),
        grid=(b, s // rows),
        in_specs=[pl.BlockSpec(memory_space=pltpu.SMEM), blk,
                  per_batch(kkc), per_batch(kk), per_batch(vvc), per_batch(vv)],
        out_specs=blk,
        out_shape=jax.ShapeDtypeStruct((b, s, d), BF16),
        compiler_params=_params(2),
        name="attn_latent",
    )(sink, q, kkc, kk, vvc, vv)


def _attn_c(sink, qc, kkc, vvc):
    b, l, d = qc.shape
    return pl.pallas_call(
        functools.partial(_attn_ctx_kernel, pairs=ATTN_PAIRS_C),
        grid=(b,),
        in_specs=[
            pl.BlockSpec(memory_space=pltpu.SMEM),
            pl.BlockSpec((None, l, d), lambda b: (b, 0, 0)),
            pl.BlockSpec((None, l, kkc.shape[2]), lambda b: (b, 0, 0)),
            pl.BlockSpec((None,) + vvc.shape[1:], lambda b: (b, 0, 0, 0)),
        ],
        out_specs=pl.BlockSpec((None, l, d), lambda b: (b, 0, 0)),
        out_shape=jax.ShapeDtypeStruct((b, l, d), BF16),
        compiler_params=_params(1),
        name="attn_context",
    )(sink, qc, kkc, vvc)


def _ffn_tail(x1s, parts, shf_ref, scf_ref, gtf_ref, g2_ref, win_ref, wout_ref, out_ref):
    h2s = [(_rms(x1, g2_ref[...]) * (1.0 + scf_ref[...]) + shf_ref[...]).astype(BF16) for x1 in x1s]
    accs = [None] * len(x1s)
    hidden = {}
    for step in range(len(FF_CHUNKS) + 1):
        if step < len(FF_CHUNKS):
            lo, hi = FF_CHUNKS[step]
            hidden[step] = [(_dot(h2, win_ref[:, lo:hi]), _dot(h2, win_ref[:, D_FF + lo:D_FF + hi])) for h2 in h2s]
        if step > 0:
            lo, hi = FF_CHUNKS[step - 1]
            for i, (gate, up) in enumerate(hidden.pop(step - 1)):
                down = _dot((_silu(gate) * up).astype(BF16), wout_ref[lo:hi, :])
                accs[i] = down if accs[i] is None else accs[i] + down
    for rows, x1, acc in zip(parts, x1s, accs):
        out_ref[rows, :] = x1 + gtf_ref[...] * acc


def _post0_kernel(x_ref, o_ref, gtm_ref, shf_ref, scf_ref, gtf_ref, g2_ref, wo_ref, win_ref, wout_ref, out_ref):
    parts = _row_parts(x_ref, ROW_PARTS)
    mixed = [_dot(o_ref[rows, :], wo_ref[...]) for rows in parts]
    x1s = [x_ref[rows, :] + gtm_ref[...] * a for rows, a in zip(parts, mixed)]
    _ffn_tail(x1s, parts, shf_ref, scf_ref, gtf_ref, g2_ref, win_ref, wout_ref, out_ref)


def _post0(xin, o, mods, per_batch, g2, wo, win, wout, layer, tm):
    b, t, d = xin.shape
    return pl.pallas_call(
        _post0_kernel,
        grid=(b, t // tm),
        in_specs=[
            _tok_spec(tm, d), _tok_spec(tm, d),
            _mod_spec(2, per_batch), _mod_spec(3, per_batch), _mod_spec(4, per_batch), _mod_spec(5, per_batch),
            _const2((1, d)), _const2(wo.shape), _layer_spec(win, layer), _layer_spec(wout, layer),
        ],
        out_specs=_tok_spec(tm, d),
        out_shape=jax.ShapeDtypeStruct((b, t, d), F32),
        compiler_params=_params(2),
        name="attn_out_ffn",
    )(xin, o, mods, mods, mods, mods, g2, wo, win, wout)


def _post1_kernel(x_ref, y_ref, shm_ref, scm_ref, gtm_ref, shf_ref, scf_ref, gtf_ref, g1_ref, g2_ref, gn_ref,
                  wg_ref, wo_ref, win_ref, wout_ref, out_ref):
    parts = _row_parts(x_ref, ROW_PARTS)
    xs = [x_ref[rows, :] for rows in parts]
    h1s = [(_rms(x, g1_ref[...]) * (1.0 + scm_ref[...]) + shm_ref[...]).astype(BF16) for x in xs]
    gates = [_dot(h1, wg_ref[...]) for h1 in h1s]
    gated = []
    for rows, gate in zip(parts, gates):
        normed = []
        for h in range(RET_HEADS):
            o = y_ref[rows, h * RET_V_DIM:(h + 1) * RET_V_DIM].astype(F32)
            d = o - jnp.mean(o, axis=-1, keepdims=True)
            normed.append(d * lax.rsqrt(jnp.mean(d * d, axis=-1, keepdims=True) + EPS))
        gated.append((_silu(gate) * (jnp.concatenate(normed, axis=1) * gn_ref[...])).astype(BF16))
    mixed = [_dot(g, wo_ref[...]) for g in gated]
    x1s = [x + gtm_ref[...] * a for x, a in zip(xs, mixed)]
    _ffn_tail(x1s, parts, shf_ref, scf_ref, gtf_ref, g2_ref, win_ref, wout_ref, out_ref)


def _post1(xin, y, mods, g1, g2, gn, wg, wo, win, wout, layer, tm):
    b, t, d = xin.shape
    return pl.pallas_call(
        _post1_kernel,
        grid=(b, t // tm),
        in_specs=[
            _tok_spec(tm, d), _tok_spec(tm, y.shape[2]),
            *[_mod_spec(k, True) for k in range(6)],
            _const2((1, d)), _const2((1, d)), _const2(gn.shape),
            pl.BlockSpec((d, RET_VWIDTH), lambda b, j: (0, wg.shape[1] // RET_VWIDTH - 1), pipeline_mode=pl.Buffered(1)),
            _const2(wo.shape), _layer_spec(win, layer), _layer_spec(wout, layer),
        ],
        out_specs=_tok_spec(tm, d),
        out_shape=jax.ShapeDtypeStruct((b, t, d), F32),
        compiler_params=_params(2),
        name="ret_out_ffn",
    )(xin, y, mods, mods, mods, mods, mods, mods, g1, g2, gn, wg, wo, win, wout)


def _proj1_kernel(x_ref, sh_ref, sc_ref, g1_ref, w_ref, cos_ref, sin_ref, q_ref, k_ref, v_ref):
    qk_w = 2 * RET_HEADS * RET_QK_DIM
    hb = (_rms(x_ref[...], g1_ref[...]) * (1.0 + sc_ref[...]) + sh_ref[...]).astype(BF16)
    qk = _dot(hb, w_ref[:, :qk_w])
    k_scale = RET_QK_DIM ** -0.5
    for cb in range(qk_w // LANES):
        ts = slice((cb % 2) * LANES, (cb % 2 + 1) * LANES)
        tb = qk[:, cb * LANES:(cb + 1) * LANES]
        r = tb * cos_ref[:, ts] + pltpu.roll(tb, LANES // 2, 1) * sin_ref[:, ts]
        if cb < qk_w // (2 * LANES):
            q_ref[:, cb * LANES:(cb + 1) * LANES] = r.astype(BF16)
        else:
            cc = cb - qk_w // (2 * LANES)
            k_ref[:, cc * LANES:(cc + 1) * LANES] = (r * k_scale).astype(BF16)
    v_ref[...] = _dot(hb, w_ref[:, qk_w:]).astype(BF16)


def _proj1_ctx_kernel(x_ref, sh_ref, sc_ref, g1_ref, w_ref, k_ref, v_ref):
    qw = RET_HEADS * RET_QK_DIM
    hb = (_rms(x_ref[...], g1_ref[...]) * (1.0 + sc_ref[...]) + sh_ref[...]).astype(BF16)
    kv = _dot(hb, w_ref[:, qw:])
    k_ref[...] = (kv[:, :qw] * RET_QK_DIM ** -0.5).astype(BF16)
    v_ref[...] = kv[:, qw:].astype(BF16)


def _proj1(xin, mods, per_batch, g1, w, tables, tm):
    b, t, d = xin.shape
    qw = RET_HEADS * RET_QK_DIM
    specs = [
        _tok_spec(tm, d),
        _mod_spec(0, per_batch), _mod_spec(1, per_batch),
        _const2((1, d)),
        pl.BlockSpec((d, 2 * qw + RET_VWIDTH), lambda b, j: (0, 0), pipeline_mode=pl.Buffered(1)),
    ]
    outs = [(qw, _tok_spec(tm, qw)), (RET_VWIDTH, _tok_spec(tm, RET_VWIDTH))]
    if tables is not None:
        specs += [pl.BlockSpec((tm, 2 * LANES), lambda b, j: (j, 0))] * 2
        outs = [(qw, _tok_spec(tm, qw))] + outs
    return pl.pallas_call(
        _proj1_ctx_kernel if tables is None else _proj1_kernel,
        grid=(b, t // tm),
        in_specs=specs,
        out_specs=[spec for _, spec in outs],
        out_shape=[jax.ShapeDtypeStruct((b, t, width), BF16) for width, _ in outs],
        compiler_params=_params(2),
        name="ret_qkv_proj",
    )(xin, mods, mods, g1, w, *(tables or ()))


def _ret_kernel(dl_ref, q_ref, k_ref, v_ref, kc_ref, vc_ref, y_ref,
                o_scr, sf_scr, sb_scr, comb_scr, qdf_scr, qdb_scr, kdf_scr, kdb_scr, cdf_scr, cdb_scr,
                ctxf_scr, ctxb_scr, *, nc):
    hd = pl.program_id(0)
    c_len = RET_CHUNK
    l_ctx = kc_ref.shape[1]

    @pl.when(pl.program_id(1) == 0)
    def _decay_tables():
        def log_decay(direction, shp):
            logit = jnp.full(shp, dl_ref[direction, hd], F32)
            return jnp.minimum(logit, 0.0) - jnp.log1p(jnp.exp(-jnp.abs(logit)))

        shape = (c_len, c_len)
        n = lax.broadcasted_iota(jnp.int32, shape, 0).astype(F32)
        m = lax.broadcasted_iota(jnp.int32, shape, 1).astype(F32)
        comb_scr[...] = (jnp.where(n >= m, jnp.exp(log_decay(0, shape) * jnp.maximum(n - m, 0.0)), 0.0)
                         + jnp.where(m >= n, jnp.exp(log_decay(1, shape) * jnp.maximum(m - n, 0.0)), 0.0))
        nk = lax.broadcasted_iota(jnp.int32, (c_len, RET_QK_DIM), 0).astype(F32)
        nv = lax.broadcasted_iota(jnp.int32, (c_len, RET_V_DIM), 0).astype(F32)
        qdf_scr[...] = jnp.exp(log_decay(0, nv.shape) * (nv + 1.0))
        kdf_scr[...] = jnp.exp(log_decay(0, nk.shape) * (c_len - 1.0 - nk))
        cdf_scr[...] = jnp.exp(log_decay(0, (1, RET_V_DIM)) * float(c_len))
        qdb_scr[...] = jnp.exp(log_decay(1, nv.shape) * (c_len - nv))
        kdb_scr[...] = jnp.exp(log_decay(1, nk.shape) * nk)
        cdb_scr[...] = jnp.exp(log_decay(1, (1, RET_V_DIM)) * float(c_len))
        tt = lax.broadcasted_iota(jnp.int32, (l_ctx, RET_QK_DIM), 0).astype(F32)
        ctxf_scr[...] = jnp.exp(log_decay(0, tt.shape) * (l_ctx - 1.0 - tt))
        ctxb_scr[...] = jnp.exp(log_decay(1, tt.shape) * tt)

    seqs = range(q_ref.shape[0])

    for n in seqs:
        kc = kc_ref[n].astype(F32)
        sf_scr[n] = _dot_tn((kc * ctxf_scr[...]).astype(BF16), vc_ref[n])
        sb_scr[n] = _dot_tn((kc * ctxb_scr[...]).astype(BF16), vc_ref[n])

    def rows(c):
        return slice(c * c_len, (c + 1) * c_len)

    def masked_scores(n, c):
        return (_dot_nt(q_ref[n, rows(c), :], k_ref[n, rows(c), :]) * comb_scr[...]).astype(BF16)

    def forward(n, c, scores):
        qc, kx, vx = q_ref[n, rows(c), :], k_ref[n, rows(c), :], v_ref[n, rows(c), :]
        st = sf_scr[n]
        out = _dot(scores, vx) + _dot(qc, st.astype(BF16)) * qdf_scr[...]
        sf_scr[n] = st * cdf_scr[...] + _dot_tn((kx.astype(F32) * kdf_scr[...]).astype(BF16), vx)
        return out

    def backward(n, c):
        qc, kx, vx = q_ref[n, rows(c), :], k_ref[n, rows(c), :], v_ref[n, rows(c), :]
        st = sb_scr[n]
        out = _dot(qc, st.astype(BF16)) * qdb_scr[...]
        sb_scr[n] = st * cdb_scr[...] + _dot_tn((kx.astype(F32) * kdb_scr[...]).astype(BF16), vx)
        return out

    scores = {}
    for step in range(nc + RET_LOOKAHEAD):
        if step < nc:
            for n in seqs:
                scores[n, step] = masked_scores(n, step)
        i = step - RET_LOOKAHEAD
        if i < 0:
            continue
        for n in seqs:
            for c, out in ((i, forward(n, i, scores.pop((n, i)))), (nc - 1 - i, backward(n, nc - 1 - i))):
                if i < nc // 2:
                    o_scr[n, rows(c), :] = out
                else:
                    y_ref[n, rows(c), :] = (o_scr[n, rows(c), :] + out).astype(BF16)


def _retention(dl, q, k, v, kc, vc):
    b, s, _ = q.shape
    l = kc.shape[1]
    nc = s // RET_CHUNK
    assert nc % 2 == 0
    seqs = RET_SEQS if b % RET_SEQS == 0 else 1
    hspec = lambda t, w: pl.BlockSpec((seqs, t, w), lambda h, b: (b, 0, h))
    return pl.pallas_call(
        functools.partial(_ret_kernel, nc=nc),
        grid=(RET_HEADS, b // seqs),
        in_specs=[
            pl.BlockSpec(memory_space=pltpu.SMEM),
            hspec(s, RET_QK_DIM), hspec(s, RET_QK_DIM), hspec(s, RET_V_DIM),
            hspec(l, RET_QK_DIM), hspec(l, RET_V_DIM),
        ],
        out_specs=hspec(s, RET_V_DIM),
        out_shape=jax.ShapeDtypeStruct((b, s, RET_VWIDTH), BF16),
        scratch_shapes=[
            pltpu.VMEM((seqs, s, RET_V_DIM), F32),
            pltpu.VMEM((seqs, RET_QK_DIM, RET_V_DIM), F32),
            pltpu.VMEM((seqs, RET_QK_DIM, RET_V_DIM), F32),
            pltpu.VMEM((RET_CHUNK, RET_CHUNK), F32),
            pltpu.VMEM((RET_CHUNK, RET_V_DIM), F32),
            pltpu.VMEM((RET_CHUNK, RET_V_DIM), F32),
            pltpu.VMEM((RET_CHUNK, RET_QK_DIM), F32),
            pltpu.VMEM((RET_CHUNK, RET_QK_DIM), F32),
            pltpu.VMEM((1, RET_V_DIM), F32),
            pltpu.VMEM((1, RET_V_DIM), F32),
            pltpu.VMEM((l, RET_QK_DIM), F32),
            pltpu.VMEM((l, RET_QK_DIM), F32),
        ],
        compiler_params=_params(2),
        name="retention",
    )(dl, q, k, v, kc, vc)


def _rope_tables(n, head_dim):
    rows = n // GRID_W
    row = jnp.broadcast_to(jnp.arange(rows, dtype=jnp.int32)[:, None], (rows, GRID_W)).reshape(n)
    col = jnp.broadcast_to(jnp.arange(GRID_W, dtype=jnp.int32)[None, :], (rows, GRID_W)).reshape(n)
    axis_dim = head_dim // 2
    inv = ROPE_BASE ** (-jnp.arange(0, axis_dim, 2, dtype=F32) / axis_dim)
    ang_r = row.astype(F32)[:, None] * inv
    ang_c = col.astype(F32)[:, None] * inv
    cos = jnp.concatenate([jnp.cos(ang_r)] * 2 + [jnp.cos(ang_c)] * 2, axis=-1)
    sin = jnp.concatenate([-jnp.sin(ang_r), jnp.sin(ang_r), -jnp.sin(ang_c), jnp.sin(ang_c)], axis=-1)
    return cos, sin


def kernel(x, c, ctx, c_ctx, ada_w, ada_b, norm1_g, norm2_g, ffn_w_in, ffn_w_out, attn_w_qkv, attn_q_norm,
           attn_k_norm, attn_sink, attn_w_o, ret_w_qkvg, ret_decay_logit, ret_gn_g, ret_w_o):
    b, s, d = x.shape
    l = ctx.shape[1]
    tm_x = min(FFN_TILE, s)
    tm_c = min(FFN_TILE, l)
    tp_x = min(PROJ_TILE, s)
    tp_c = min(PROJ_TILE, l)

    rows = -(-(b + 1) // 8) * 8
    cin = jnp.concatenate([c, c_ctx[None, :], jnp.zeros((rows - b - 1, d), F32)], axis=0)
    mods = _mods(cin, ada_w, ada_b)
    mx0 = mods[0, :b].reshape(b, 1, 6 * d)
    mc0 = mods[0, b:b + 1].reshape(1, 1, 6 * d)
    mx1 = mods[1, :b].reshape(b, 1, 6 * d)
    mc1 = mods[1, b:b + 1].reshape(1, 1, 6 * d)

    w_qkv = attn_w_qkv[0].astype(BF16)
    scale = HEAD_DIM ** -0.5 * LOG2E
    gain = jnp.concatenate([jnp.tile(attn_q_norm[0] * scale, N_HEADS), jnp.tile(attn_k_norm[0], N_KV_HEADS)])[None, :]
    lane_head = jnp.arange(MXU_TILE) // HEAD_DIM
    gsum = ((lane_head[:, None] == lane_head[None, :]).astype(F32) / HEAD_DIM).astype(BF16)
    cos64, sin64 = _rope_tables(s, HEAD_DIM)
    cos_x = jnp.tile(cos64, (1, LANES // HEAD_DIM))
    sin_x = jnp.tile(sin64, (1, LANES // HEAD_DIM))
    cos_c = jnp.ones((l, LANES), F32)
    sin_c = jnp.zeros((l, LANES), F32)
    g1 = norm1_g[0][None, :]
    g2 = norm2_g[0][None, :]
    q_x, kk_x, vv_x, w_o, w_in, w_out, w_qkvg, w_ro = _qkv0(
        x, mx0, True, g1, w_qkv, gain, gsum, cos_x, sin_x, tp_x,
        side=(attn_w_o[0], ffn_w_in, ffn_w_out, ret_w_qkvg[0], ret_w_o[0]))
    q_c, kk_c, vv_c = _qkv0(ctx, mc0, False, g1, w_qkv, gain, gsum, cos_c, sin_c, tp_c)
    sink = attn_sink[0].astype(F32)
    o_x = _attn_x(sink, q_x, kk_x, vv_x, kk_c, vv_c)
    o_c = _attn_c(sink, q_c, kk_c, vv_c)
    x1 = _post0(x, o_x, mx0, True, g2, w_o, w_in, w_out, 0, tm_x)
    y_ctx = _post0(ctx, o_c, mc0, False, g2, w_o, w_in, w_out, 0, tm_c)

    cos256, sin256 = _rope_tables(s, RET_QK_DIM)
    g1 = norm1_g[1][None, :]
    g2 = norm2_g[1][None, :]
    q1, k1, v1 = _proj1(x1, mx1, True, g1, w_qkvg, (cos256, sin256), tp_x)
    k1c, v1c = _proj1(y_ctx, mc1, False, g1, w_qkvg, None, tp_c)
    y = _retention(ret_decay_logit[0].astype(F32), q1, k1, v1, k1c, v1c)
    return _post1(x1, y, mx1, g1, g2, ret_gn_g[0][None, :], w_qkvg, w_ro,
                  w_in, w_out, 1, tm_x)
```

```python
import functools

import jax
import jax.numpy as jnp
from jax import lax
from jax.experimental import pallas as pl
from jax.experimental.pallas import tpu as pltpu

F32 = jnp.float32
BF16 = jnp.bfloat16

D_MODEL = 1024
GRID_W = 64
HEAD_DIM = 64
N_HEADS = D_MODEL // HEAD_DIM
N_KV_HEADS = N_HEADS // 4
WINDOW = 128
ATTN_BLOCK = 128
RET_HEADS = 4
RET_QK_DIM = 256
RET_V_DIM = 512
RET_VWIDTH = 2 * D_MODEL
RET_CHUNK = 256
ATTN_PAIRS_X = 2
ATTN_PAIRS_C = 2
ATTN_BLOCKS_PER_STEP = 8
ATTN_LOOKAHEAD = 2
RET_SEQS = 2
RET_LOOKAHEAD = 2
D_FF = 2816
MXU_TILE = 256
FF_CHUNKS = ((0, 6 * MXU_TILE), (6 * MXU_TILE, D_FF))
ROW_PARTS = 2
ROPE_BASE = 10000.0
EPS = 1e-6
NEG_INF = -1e30
LOG2E = 1.4426950408889634
LANES = 128

VMEM_LIMIT = 56 * 1024 * 1024
FFN_TILE = 512
PROJ_TILE = 1024


def _dot(a, b):
    return jnp.dot(a, b, preferred_element_type=F32)


def _dot_nt(a, b):
    return lax.dot_general(a, b, (((1,), (1,)), ((), ())), preferred_element_type=F32)


def _dot_tn(a, b):
    return lax.dot_general(a, b, (((0,), (0,)), ((), ())), preferred_element_type=F32)


def _rms(x, g):
    return x * lax.rsqrt(jnp.mean(x * x, axis=-1, keepdims=True) + EPS) * g


def _silu(x):
    return x * jax.nn.sigmoid(x)


def _params(n_axes):
    return pltpu.CompilerParams(dimension_semantics=("arbitrary",) * n_axes, vmem_limit_bytes=VMEM_LIMIT)


def _const2(shape):
    return pl.BlockSpec(shape, lambda b, j: (0,) * len(shape), pipeline_mode=pl.Buffered(1))


def _layer_spec(stacked, layer):
    return pl.BlockSpec((None,) + stacked.shape[1:], lambda b, j: (layer,) + (0,) * (stacked.ndim - 1),
                        pipeline_mode=pl.Buffered(1))


def _mod_spec(k, per_batch):
    if per_batch:
        return pl.BlockSpec((None, 1, D_MODEL), lambda b, j, k=k: (b, 0, k))
    return pl.BlockSpec((None, 1, D_MODEL), lambda b, j, k=k: (0, 0, k))


def _tok_spec(tm, width):
    return pl.BlockSpec((None, tm, width), lambda b, j: (b, j, 0))


def _row_parts(ref, n):
    tm = ref.shape[0]
    n = n if tm % (n * ATTN_BLOCK) == 0 else 1
    return [slice(i * tm // n, (i + 1) * tm // n) for i in range(n)]


def _mods_kernel(c_ref, w_ref, b_ref, o_ref):
    a = _silu(c_ref[...]).astype(BF16)
    o_ref[...] = _dot(a, w_ref[...].astype(BF16)) + b_ref[...]


def _mods(cin, ada_w, ada_b):
    depth, d, n = ada_w.shape
    rows = cin.shape[0]
    tn = 1536
    return pl.pallas_call(
        _mods_kernel,
        grid=(depth, n // tn),
        in_specs=[
            pl.BlockSpec((rows, d), lambda i, j: (0, 0)),
            pl.BlockSpec((None, d, tn), lambda i, j: (i, 0, j)),
            pl.BlockSpec((None, 1, tn), lambda i, j: (i, 0, j)),
        ],
        out_specs=pl.BlockSpec((None, rows, tn), lambda i, j: (i, 0, j)),
        out_shape=jax.ShapeDtypeStruct((depth, rows, n), F32),
        compiler_params=_params(2),
        name="adaln_mods",
    )(cin, ada_w, ada_b.reshape(depth, 1, n))


def _dup_halves(r):
    lane = lax.broadcasted_iota(jnp.int32, r.shape, 1)
    lo = lane < HEAD_DIM
    r64 = pltpu.roll(r, HEAD_DIM, 1)
    return jnp.where(lo, r, r64), jnp.where(lo, r64, r)


def _qkv0_kernel(x_ref, sh_ref, sc_ref, g1_ref, w_ref, gain_ref, gsum_ref, cos_ref, sin_ref, *rest):
    n_side = (len(rest) - 3) // 2
    q_ref, kk_ref, vt_ref = rest[n_side:n_side + 3]
    for src_ref, dst_ref in zip(rest[:n_side], rest[n_side + 3:]):
        dst_ref[...] = src_ref[...].astype(BF16)
    qd = N_HEADS * HEAD_DIM
    qkd = qd + N_KV_HEADS * HEAD_DIM
    parts = _row_parts(x_ref, ROW_PARTS)
    hs = [_rms(x_ref[rows, :], g1_ref[...]) * (1.0 + sc_ref[...]) + sh_ref[...] for rows in parts]
    qkvs = [_dot(h.astype(BF16), w_ref[...]) for h in hs]
    invfs = []
    for qkv in qkvs:
        sq = (qkv[:, :qkd] * qkv[:, :qkd]).astype(BF16)
        ms = jnp.concatenate([_dot(sq[:, c:c + MXU_TILE], gsum_ref[...]) for c in range(0, qkd, MXU_TILE)], axis=1)
        invfs.append(lax.rsqrt(ms + EPS))
    row = lax.broadcasted_iota(jnp.int32, (LANES - HEAD_DIM, ATTN_BLOCK), 0)
    tail = jnp.where(row == 0, 1.0, 0.0).astype(BF16)
    for rows, qkv, invf in zip(parts, qkvs, invfs):
        t = qkv[:, :qkd] * gain_ref[...]
        cos = cos_ref[rows, :]
        sin = sin_ref[rows, :]
        lane = lax.broadcasted_iota(jnp.int32, cos.shape, 1)
        first = (lane % 32) < 16
        for cb in range(qkd // LANES):
            sl = slice(cb * LANES, (cb + 1) * LANES)
            tb = t[:, sl]
            sw = jnp.where(first, pltpu.roll(tb, LANES - 16, 1), pltpu.roll(tb, 16, 1))
            r = (tb * cos + sw * sin) * invf[:, sl]
            if cb < qd // LANES:
                q_ref[rows, sl] = r.astype(BF16)
            else:
                p = cb - qd // LANES
                a, b = _dup_halves(r)
                kk_ref[rows, (2 * p) * LANES:(2 * p + 1) * LANES] = a.astype(BF16)
                kk_ref[rows, (2 * p + 1) * LANES:(2 * p + 2) * LANES] = b.astype(BF16)
        vt = qkv[:, qkd:].T
        for blk in range(vt.shape[1] // ATTN_BLOCK):
            cols = slice(blk * ATTN_BLOCK, (blk + 1) * ATTN_BLOCK)
            slab = rows.start // ATTN_BLOCK + blk
            for g in range(N_KV_HEADS):
                vt_ref[slab, g * LANES:g * LANES + HEAD_DIM, :] = vt[g * HEAD_DIM:(g + 1) * HEAD_DIM, cols].astype(BF16)
                vt_ref[slab, g * LANES + HEAD_DIM:(g + 1) * LANES, :] = tail


def _row_blocks(rows, steps):
    return next(n for n in range(min(steps, rows // 16), 0, -1) if rows % n == 0 and (rows // n) % 16 == 0)


def _qkv0(xin, mods, per_batch, g1, w, gain, gsum, cos, sin, tm, side=()):
    b, t, d = xin.shape
    kw = N_KV_HEADS * LANES
    nj = t // tm
    flat = [a.reshape(-1, a.shape[-1]) for a in side]
    side_specs = []
    for a in flat:
        nblk = _row_blocks(a.shape[0], b * nj)
        side_specs.append(pl.BlockSpec((a.shape[0] // nblk, a.shape[1]),
                                       lambda b, j, nblk=nblk: (jnp.minimum(b * nj + j, nblk - 1), 0)))
    outs = pl.pallas_call(
        _qkv0_kernel,
        grid=(b, nj),
        in_specs=[
            _tok_spec(tm, d),
            _mod_spec(0, per_batch),
            _mod_spec(1, per_batch),
            _const2((1, d)),
            _const2(w.shape),
            _const2(gain.shape),
            _const2(gsum.shape),
            pl.BlockSpec((tm, LANES), lambda b, j: (j, 0)),
            pl.BlockSpec((tm, LANES), lambda b, j: (j, 0)),
            *side_specs,
        ],
        out_specs=[_tok_spec(tm, d), _tok_spec(tm, kw),
                   pl.BlockSpec((None, tm // ATTN_BLOCK, kw, ATTN_BLOCK), lambda b, j: (b, j, 0, 0)), *side_specs],
        out_shape=[
            jax.ShapeDtypeStruct((b, t, d), BF16),
            jax.ShapeDtypeStruct((b, t, kw), BF16),
            jax.ShapeDtypeStruct((b, t // ATTN_BLOCK, kw, ATTN_BLOCK), BF16),
            *[jax.ShapeDtypeStruct(a.shape, BF16) for a in flat],
        ],
        compiler_params=_params(2),
        name="attn_qkv_proj",
    )(xin, mods, mods, g1, w, gain, gsum, cos, sin, *flat)
    return list(outs[:3]) + [o.reshape(a.shape) for o, a in zip(outs[3:], side)]


def _band_masks(j, nb, heads):
    shape = (ATTN_BLOCK, heads * ATTN_BLOCK)
    key = lax.broadcasted_iota(jnp.int32, shape, 0)
    qry = lax.broadcasted_iota(jnp.int32, shape, 1) % ATTN_BLOCK
    return (key >= qry) & (j > 0), (key <= qry) & (j < nb - 1)


def _attn_scores(sink_ref, q_ref, key_refs, pr, pairs, masks):
    tq = q_ref.shape[0]
    g = pr // (N_HEADS // N_KV_HEADS // 2)
    gs = slice(g * LANES, (g + 1) * LANES)
    lo = lax.broadcasted_iota(jnp.int32, (tq, LANES), 1) < HEAD_DIM
    zero = jnp.zeros((tq, LANES), BF16)
    stack = []
    for p in range(pr, pr + pairs):
        qb = q_ref[:, p * LANES:(p + 1) * LANES]
        stack += [jnp.where(lo, qb, zero), jnp.where(lo, zero, qb)]
    keys = jnp.concatenate([r[:, gs] for r in key_refs], axis=0) if len(key_refs) > 1 else key_refs[0][:, gs]
    st = _dot_nt(keys, jnp.concatenate(stack, axis=0))
    parts = [st[i * LANES:(i + 1) * LANES, :] for i in range(st.shape[0] // LANES)]
    if masks is not None:
        ia = key_refs[0].shape[0] // LANES
        parts[ia] = jnp.where(masks[0], parts[ia], NEG_INF)
        parts[ia + 2] = jnp.where(masks[1], parts[ia + 2], NEG_INF)
    sink = jnp.concatenate([jnp.full((1, tq), sink_ref[2 * pr + i] * LOG2E, F32) for i in range(2 * pairs)], axis=1)
    mx = parts[0]
    for part in parts[1:]:
        mx = jnp.maximum(mx, part)
    m = jnp.maximum(jnp.max(mx, axis=0, keepdims=True), sink)
    return jnp.concatenate([jnp.exp2(part - m).astype(BF16) for part in parts], axis=0), jnp.exp2(sink - m)


def _attn_values(probs, val_refs, o_ref, pr, pairs):
    pt, p_sink = probs
    tq = o_ref.shape[0]
    g = pr // (N_HEADS // N_KV_HEADS // 2)
    gs = slice(g * LANES, (g + 1) * LANES)
    vt = jnp.concatenate([r[i, gs, :] for r in val_refs for i in range(r.shape[0])], axis=1)
    ot = _dot(vt, pt)
    den = jnp.sum(ot[HEAD_DIM:HEAD_DIM + 8, :], axis=0, keepdims=True) + p_sink
    out = ot[:HEAD_DIM, :] / den
    for i in range(pairs):
        pair = jnp.concatenate([out[:, 2 * i * tq:(2 * i + 1) * tq], out[:, (2 * i + 1) * tq:(2 * i + 2) * tq]], axis=0)
        o_ref[:, (pr + i) * LANES:(pr + i + 1) * LANES] = pair.T.astype(BF16)


def _attn_pipeline(sink_ref, tasks, pairs):
    items = [(task, g) for task in tasks for g in range(0, N_HEADS // 2, pairs)]
    probs = {}
    for step in range(len(items) + ATTN_LOOKAHEAD):
        if step < len(items):
            (q_ref, key_refs, _, _, masks), g = items[step]
            probs[step] = _attn_scores(sink_ref, q_ref, key_refs, g, pairs, masks)
        done = step - ATTN_LOOKAHEAD
        if done >= 0:
            (_, _, val_refs, o_ref, _), g = items[done]
            _attn_values(probs.pop(done), val_refs, o_ref, g, pairs)


def _attn_ctx_kernel(sink_ref, q_ref, kc_ref, vc_ref, o_ref, *, pairs):
    _attn_pipeline(sink_ref, [(q_ref, [kc_ref], [vc_ref], o_ref, None)], pairs)


def _attn_band_kernel(sink_ref, q_ref, kc_ref, k_ref, vc_ref, vt_ref, o_ref, *, nb, pairs):
    tasks = []
    for t in range(q_ref.shape[0] // ATTN_BLOCK):
        j = pl.program_id(1) * (q_ref.shape[0] // ATTN_BLOCK) + t
        rows = pl.ds(t * ATTN_BLOCK, ATTN_BLOCK)
        band = [jnp.maximum(j - 1, 0), j, jnp.minimum(j + 1, nb - 1)]
        key_refs = [kc_ref] + [k_ref.at[pl.ds(pl.multiple_of(blk * ATTN_BLOCK, ATTN_BLOCK), ATTN_BLOCK), :]
                               for blk in band]
        val_refs = [vc_ref] + [vt_ref.at[pl.ds(blk, 1)] for blk in band]
        tasks.append((q_ref.at[rows, :], key_refs, val_refs, o_ref.at[rows, :], _band_masks(j, nb, 2 * pairs)))
    _attn_pipeline(sink_ref, tasks, pairs)


def _attn_x(sink, q, kk, vv, kkc, vvc):
    b, s, d = q.shape
    nb = s // ATTN_BLOCK
    rows = ATTN_BLOCK * max(n for n in range(1, ATTN_BLOCKS_PER_STEP + 1) if nb % n == 0)
    per_batch = lambda a: pl.BlockSpec((None,) + a.shape[1:], lambda b, j: (b,) + (0,) * (a.ndim - 1))
    blk = pl.BlockSpec((None, rows, d), lambda b, j: (b, j, 0))
    return pl.pallas_call(
        functools.partial(_attn_band_kernel, nb=nb, pairs=ATTN_PAIRS_X),
        grid=(b, s // rows),
        in_specs=[pl.BlockSpec(memory_space=pltpu.SMEM), blk,
                  per_batch(kkc), per_batch(kk), per_batch(vvc), per_batch(vv)],
        out_specs=blk,
        out_shape=jax.ShapeDtypeStruct((b, s, d), BF16),
        compiler_params=_params(2),
        name="attn_latent",
    )(sink, q, kkc, kk, vvc, vv)


def _attn_c(sink, qc, kkc, vvc):
    b, l, d = qc.shape
    return pl.pallas_call(
        functools.partial(_attn_ctx_kernel, pairs=ATTN_PAIRS_C),
        grid=(b,),
        in_specs=[
            pl.BlockSpec(memory_space=pltpu.SMEM),
            pl.BlockSpec((None, l, d), lambda b: (b, 0, 0)),
            pl.BlockSpec((None, l, kkc.shape[2]), lambda b: (b, 0, 0)),
            pl.BlockSpec((None,) + vvc.shape[1:], lambda b: (b, 0, 0, 0)),
        ],
        out_specs=pl.BlockSpec((None, l, d), lambda b: (b, 0, 0)),
        out_shape=jax.ShapeDtypeStruct((b, l, d), BF16),
        compiler_params=_params(1),
        name="attn_context",
    )(sink, qc, kkc, vvc)


def _ffn_tail(x1s, parts, shf_ref, scf_ref, gtf_ref, g2_ref, win_ref, wout_ref, out_ref):
    h2s = [(_rms(x1, g2_ref[...]) * (1.0 + scf_ref[...]) + shf_ref[...]).astype(BF16) for x1 in x1s]
    accs = [None] * len(x1s)
    hidden = {}
    for step in range(len(FF_CHUNKS) + 1):
        if step < len(FF_CHUNKS):
            lo, hi = FF_CHUNKS[step]
            hidden[step] = [(_dot(h2, win_ref[:, lo:hi]), _dot(h2, win_ref[:, D_FF + lo:D_FF + hi])) for h2 in h2s]
        if step > 0:
            lo, hi = FF_CHUNKS[step - 1]
            for i, (gate, up) in enumerate(hidden.pop(step - 1)):
                down = _dot((_silu(gate) * up).astype(BF16), wout_ref[lo:hi, :])
                accs[i] = down if accs[i] is None else accs[i] + down
    for rows, x1, acc in zip(parts, x1s, accs):
        out_ref[rows, :] = x1 + gtf_ref[...] * acc


def _post0_kernel(x_ref, o_ref, gtm_ref, shf_ref, scf_ref, gtf_ref, g2_ref, wo_ref, win_ref, wout_ref, out_ref):
    parts = _row_parts(x_ref, ROW_PARTS)
    mixed = [_dot(o_ref[rows, :], wo_ref[...]) for rows in parts]
    x1s = [x_ref[rows, :] + gtm_ref[...] * a for rows, a in zip(parts, mixed)]
    _ffn_tail(x1s, parts, shf_ref, scf_ref, gtf_ref, g2_ref, win_ref, wout_ref, out_ref)


def _post0(xin, o, mods, per_batch, g2, wo, win, wout, layer, tm):
    b, t, d = xin.shape
    return pl.pallas_call(
        _post0_kernel,
        grid=(b, t // tm),
        in_specs=[
            _tok_spec(tm, d), _tok_spec(tm, d),
            _mod_spec(2, per_batch), _mod_spec(3, per_batch), _mod_spec(4, per_batch), _mod_spec(5, per_batch),
            _const2((1, d)), _const2(wo.shape), _layer_spec(win, layer), _layer_spec(wout, layer),
        ],
        out_specs=_tok_spec(tm, d),
        out_shape=jax.ShapeDtypeStruct((b, t, d), F32),
        compiler_params=_params(2),
        name="attn_out_ffn",
    )(xin, o, mods, mods, mods, mods, g2, wo, win, wout)


def _post1_kernel(x_ref, y_ref, shm_ref, scm_ref, gtm_ref, shf_ref, scf_ref, gtf_ref, g1_ref, g2_ref, gn_ref,
                  wg_ref, wo_ref, win_ref, wout_ref, out_ref):
    parts = _row_parts(x_ref, ROW_PARTS)
    xs = [x_ref[rows, :] for rows in parts]
    h1s = [(_rms(x, g1_ref[...]) * (1.0 + scm_ref[...]) + shm_ref[...]).astype(BF16) for x in xs]
    gates = [_dot(h1, wg_ref[...]) for h1 in h1s]
    gated = []
    for rows, gate in zip(parts, gates):
        normed = []
        for h in range(RET_HEADS):
            o = y_ref[rows, h * RET_V_DIM:(h + 1) * RET_V_DIM].astype(F32)
            d = o - jnp.mean(o, axis=-1, keepdims=True)
            normed.append(d * lax.rsqrt(jnp.mean(d * d, axis=-1, keepdims=True) + EPS))
        gated.append((_silu(gate) * (jnp.concatenate(normed, axis=1) * gn_ref[...])).astype(BF16))
    mixed = [_dot(g, wo_ref[...]) for g in gated]
    x1s = [x + gtm_ref[...] * a for x, a in zip(xs, mixed)]
    _ffn_tail(x1s, parts, shf_ref, scf_ref, gtf_ref, g2_ref, win_ref, wout_ref, out_ref)


def _post1(xin, y, mods, g1, g2, gn, wg, wo, win, wout, layer, tm):
    b, t, d = xin.shape
    return pl.pallas_call(
        _post1_kernel,
        grid=(b, t // tm),
        in_specs=[
            _tok_spec(tm, d), _tok_spec(tm, y.shape[2]),
            *[_mod_spec(k, True) for k in range(6)],
            _const2((1, d)), _const2((1, d)), _const2(gn.shape),
            pl.BlockSpec((d, RET_VWIDTH), lambda b, j: (0, wg.shape[1] // RET_VWIDTH - 1), pipeline_mode=pl.Buffered(1)),
            _const2(wo.shape), _layer_spec(win, layer), _layer_spec(wout, layer),
        ],
        out_specs=_tok_spec(tm, d),
        out_shape=jax.ShapeDtypeStruct((b, t, d), F32),
        compiler_params=_params(2),
        name="ret_out_ffn",
    )(xin, y, mods, mods, mods, mods, mods, mods, g1, g2, gn, wg, wo, win, wout)


def _proj1_kernel(x_ref, sh_ref, sc_ref, g1_ref, w_ref, cos_ref, sin_ref, q_ref, k_ref, v_ref):
    qk_w = 2 * RET_HEADS * RET_QK_DIM
    hb = (_rms(x_ref[...], g1_ref[...]) * (1.0 + sc_ref[...]) + sh_ref[...]).astype(BF16)
    qk = _dot(hb, w_ref[:, :qk_w])
    k_scale = RET_QK_DIM ** -0.5
    for cb in range(qk_w // LANES):
        ts = slice((cb % 2) * LANES, (cb % 2 + 1) * LANES)
        tb = qk[:, cb * LANES:(cb + 1) * LANES]
        r = tb * cos_ref[:, ts] + pltpu.roll(tb, LANES // 2, 1) * sin_ref[:, ts]
        if cb < qk_w // (2 * LANES):
            q_ref[:, cb * LANES:(cb + 1) * LANES] = r.astype(BF16)
        else:
            cc = cb - qk_w // (2 * LANES)
            k_ref[:, cc * LANES:(cc + 1) * LANES] = (r * k_scale).astype(BF16)
    v_ref[...] = _dot(hb, w_ref[:, qk_w:]).astype(BF16)


def _proj1_ctx_kernel(x_ref, sh_ref, sc_ref, g1_ref, w_ref, k_ref, v_ref):
    qw = RET_HEADS * RET_QK_DIM
    hb = (_rms(x_ref[...], g1_ref[...]) * (1.0 + sc_ref[...]) + sh_ref[...]).astype(BF16)
    kv = _dot(hb, w_ref[:, qw:])
    k_ref[...] = (kv[:, :qw] * RET_QK_DIM ** -0.5).astype(BF16)
    v_ref[...] = kv[:, qw:].astype(BF16)


def _proj1(xin, mods, per_batch, g1, w, tables, tm):
    b, t, d = xin.shape
    qw = RET_HEADS * RET_QK_DIM
    specs = [
        _tok_spec(tm, d),
        _mod_spec(0, per_batch), _mod_spec(1, per_batch),
        _const2((1, d)),
        pl.BlockSpec((d, 2 * qw + RET_VWIDTH), lambda b, j: (0, 0), pipeline_mode=pl.Buffered(1)),
    ]
    outs = [(qw, _tok_spec(tm, qw)), (RET_VWIDTH, _tok_spec(tm, RET_VWIDTH))]
    if tables is not None:
        specs += [pl.BlockSpec((tm, 2 * LANES), lambda b, j: (j, 0))] * 2
        outs = [(qw, _tok_spec(tm, qw))] + outs
    return pl.pallas_call(
        _proj1_ctx_kernel if tables is None else _proj1_kernel,
        grid=(b, t // tm),
        in_specs=specs,
        out_specs=[spec for _, spec in outs],
        out_shape=[jax.ShapeDtypeStruct((b, t, width), BF16) for width, _ in outs],
        compiler_params=_params(2),
        name="ret_qkv_proj",
    )(xin, mods, mods, g1, w, *(tables or ()))


def _ret_kernel(dl_ref, q_ref, k_ref, v_ref, kc_ref, vc_ref, y_ref,
                o_scr, sf_scr, sb_scr, comb_scr, qdf_scr, qdb_scr, kdf_scr, kdb_scr, cdf_scr, cdb_scr,
                ctxf_scr, ctxb_scr, *, nc):
    hd = pl.program_id(0)
    c_len = RET_CHUNK
    l_ctx = kc_ref.shape[1]

    @pl.when(pl.program_id(1) == 0)
    def _decay_tables():
        def log_decay(direction, shp):
            logit = jnp.full(shp, dl_ref[direction, hd], F32)
            return jnp.minimum(logit, 0.0) - jnp.log1p(jnp.exp(-jnp.abs(logit)))

        shape = (c_len, c_len)
        n = lax.broadcasted_iota(jnp.int32, shape, 0).astype(F32)
        m = lax.broadcasted_iota(jnp.int32, shape, 1).astype(F32)
        comb_scr[...] = (jnp.where(n >= m, jnp.exp(log_decay(0, shape) * jnp.maximum(n - m, 0.0)), 0.0)
                         + jnp.where(m >= n, jnp.exp(log_decay(1, shape) * jnp.maximum(m - n, 0.0)), 0.0))
        nk = lax.broadcasted_iota(jnp.int32, (c_len, RET_QK_DIM), 0).astype(F32)
        nv = lax.broadcasted_iota(jnp.int32, (c_len, RET_V_DIM), 0).astype(F32)
        qdf_scr[...] = jnp.exp(log_decay(0, nv.shape) * (nv + 1.0))
        kdf_scr[...] = jnp.exp(log_decay(0, nk.shape) * (c_len - 1.0 - nk))
        cdf_scr[...] = jnp.exp(log_decay(0, (1, RET_V_DIM)) * float(c_len))
        qdb_scr[...] = jnp.exp(log_decay(1, nv.shape) * (c_len - nv))
        kdb_scr[...] = jnp.exp(log_decay(1, nk.shape) * nk)
        cdb_scr[...] = jnp.exp(log_decay(1, (1, RET_V_DIM)) * float(c_len))
        tt = lax.broadcasted_iota(jnp.int32, (l_ctx, RET_QK_DIM), 0).astype(F32)
        ctxf_scr[...] = jnp.exp(log_decay(0, tt.shape) * (l_ctx - 1.0 - tt))
        ctxb_scr[...] = jnp.exp(log_decay(1, tt.shape) * tt)

    seqs = range(q_ref.shape[0])

    for n in seqs:
        kc = kc_ref[n].astype(F32)
        sf_scr[n] = _dot_tn((kc * ctxf_scr[...]).astype(BF16), vc_ref[n])
        sb_scr[n] = _dot_tn((kc * ctxb_scr[...]).astype(BF16), vc_ref[n])

    def rows(c):
        return slice(c * c_len, (c + 1) * c_len)

    def masked_scores(n, c):
        return (_dot_nt(q_ref[n, rows(c), :], k_ref[n, rows(c), :]) * comb_scr[...]).astype(BF16)

    def forward(n, c, scores):
        qc, kx, vx = q_ref[n, rows(c), :], k_ref[n, rows(c), :], v_ref[n, rows(c), :]
        st = sf_scr[n]
        out = _dot(scores, vx) + _dot(qc, st.astype(BF16)) * qdf_scr[...]
        sf_scr[n] = st * cdf_scr[...] + _dot_tn((kx.astype(F32) * kdf_scr[...]).astype(BF16), vx)
        return out

    def backward(n, c):
        qc, kx, vx = q_ref[n, rows(c), :], k_ref[n, rows(c), :], v_ref[n, rows(c), :]
        st = sb_scr[n]
        out = _dot(qc, st.astype(BF16)) * qdb_scr[...]
        sb_scr[n] = st * cdb_scr[...] + _dot_tn((kx.astype(F32) * kdb_scr[...]).astype(BF16), vx)
        return out

    scores = {}
    for step in range(nc + RET_LOOKAHEAD):
        if step < nc:
            for n in seqs:
                scores[n, step] = masked_scores(n, step)
        i = step - RET_LOOKAHEAD
        if i < 0:
            continue
        for n in seqs:
            for c, out in ((i, forward(n, i, scores.pop((n, i)))), (nc - 1 - i, backward(n, nc - 1 - i))):
                if i < nc // 2:
                    o_scr[n, rows(c), :] = out
                else:
                    y_ref[n, rows(c), :] = (o_scr[n, rows(c), :] + out).astype(BF16)


def _retention(dl, q, k, v, kc, vc):
    b, s, _ = q.shape
    l = kc.shape[1]
    nc = s // RET_CHUNK
    assert nc % 2 == 0
    seqs = RET_SEQS if b % RET_SEQS == 0 else 1
    hspec = lambda t, w: pl.BlockSpec((seqs, t, w), lambda h, b: (b, 0, h))
    return pl.pallas_call(
        functools.partial(_ret_kernel, nc=nc),
        grid=(RET_HEADS, b // seqs),
        in_specs=[
            pl.BlockSpec(memory_space=pltpu.SMEM),
            hspec(s, RET_QK_DIM), hspec(s, RET_QK_DIM), hspec(s, RET_V_DIM),
            hspec(l, RET_QK_DIM), hspec(l, RET_V_DIM),
        ],
        out_specs=hspec(s, RET_V_DIM),
        out_shape=jax.ShapeDtypeStruct((b, s, RET_VWIDTH), BF16),
        scratch_shapes=[
            pltpu.VMEM((seqs, s, RET_V_DIM), F32),
            pltpu.VMEM((seqs, RET_QK_DIM, RET_V_DIM), F32),
            pltpu.VMEM((seqs, RET_QK_DIM, RET_V_DIM), F32),
            pltpu.VMEM((RET_CHUNK, RET_CHUNK), F32),
            pltpu.VMEM((RET_CHUNK, RET_V_DIM), F32),
            pltpu.VMEM((RET_CHUNK, RET_V_DIM), F32),
            pltpu.VMEM((RET_CHUNK, RET_QK_DIM), F32),
            pltpu.VMEM((RET_CHUNK, RET_QK_DIM), F32),
            pltpu.VMEM((1, RET_V_DIM), F32),
            pltpu.VMEM((1, RET_V_DIM), F32),
            pltpu.VMEM((l, RET_QK_DIM), F32),
            pltpu.VMEM((l, RET_QK_DIM), F32),
        ],
        compiler_params=_params(2),
        name="retention",
    )(dl, q, k, v, kc, vc)


def _rope_tables(n, head_dim):
    rows = n // GRID_W
    row = jnp.broadcast_to(jnp.arange(rows, dtype=jnp.int32)[:, None], (rows, GRID_W)).reshape(n)
    col = jnp.broadcast_to(jnp.arange(GRID_W, dtype=jnp.int32)[None, :], (rows, GRID_W)).reshape(n)
    axis_dim = head_dim // 2
    inv = ROPE_BASE ** (-jnp.arange(0, axis_dim, 2, dtype=F32) / axis_dim)
    ang_r = row.astype(F32)[:, None] * inv
    ang_c = col.astype(F32)[:, None] * inv
    cos = jnp.concatenate([jnp.cos(ang_r)] * 2 + [jnp.cos(ang_c)] * 2, axis=-1)
    sin = jnp.concatenate([-jnp.sin(ang_r), jnp.sin(ang_r), -jnp.sin(ang_c), jnp.sin(ang_c)], axis=-1)
    return cos, sin


def kernel(x, c, ctx, c_ctx, ada_w, ada_b, norm1_g, norm2_g, ffn_w_in, ffn_w_out, attn_w_qkv, attn_q_norm,
           attn_k_norm, attn_sink, attn_w_o, ret_w_qkvg, ret_decay_logit, ret_gn_g, ret_w_o):
    b, s, d = x.shape
    l = ctx.shape[1]
    tm_x = min(FFN_TILE, s)
    tm_c = min(FFN_TILE, l)
    tp_x = min(PROJ_TILE, s)
    tp_c = min(PROJ_TILE, l)

    rows = -(-(b + 1) // 8) * 8
    cin = jnp.concatenate([c, c_ctx[None, :], jnp.zeros((rows - b - 1, d), F32)], axis=0)
    mods = _mods(cin, ada_w, ada_b)
    mx0 = mods[0, :b].reshape(b, 1, 6 * d)
    mc0 = mods[0, b:b + 1].reshape(1, 1, 6 * d)
    mx1 = mods[1, :b].reshape(b, 1, 6 * d)
    mc1 = mods[1, b:b + 1].reshape(1, 1, 6 * d)

    w_qkv = attn_w_qkv[0].astype(BF16)
    scale = HEAD_DIM ** -0.5 * LOG2E
    gain = jnp.concatenate([jnp.tile(attn_q_norm[0] * scale, N_HEADS), jnp.tile(attn_k_norm[0], N_KV_HEADS)])[None, :]
    lane_head = jnp.arange(MXU_TILE) // HEAD_DIM
    gsum = ((lane_head[:, None] == lane_head[None, :]).astype(F32) / HEAD_DIM).astype(BF16)
    cos64, sin64 = _rope_tables(s, HEAD_DIM)
    cos_x = jnp.tile(cos64, (1, LANES // HEAD_DIM))
    sin_x = jnp.tile(sin64, (1, LANES // HEAD_DIM))
    cos_c = jnp.ones((l, LANES), F32)
    sin_c = jnp.zeros((l, LANES), F32)
    g1 = norm1_g[0][None, :]
    g2 = norm2_g[0][None, :]
    q_x, kk_x, vv_x, w_o, w_in, w_out, w_qkvg, w_ro = _qkv0(
        x, mx0, True, g1, w_qkv, gain, gsum, cos_x, sin_x, tp_x,
        side=(attn_w_o[0], ffn_w_in, ffn_w_out, ret_w_qkvg[0], ret_w_o[0]))
    q_c, kk_c, vv_c = _qkv0(ctx, mc0, False, g1, w_qkv, gain, gsum, cos_c, sin_c, tp_c)
    sink = attn_sink[0].astype(F32)
    o_x = _attn_x(sink, q_x, kk_x, vv_x, kk_c, vv_c)
    o_c = _attn_c(sink, q_c, kk_c, vv_c)
    x1 = _post0(x, o_x, mx0, True, g2, w_o, w_in, w_out, 0, tm_x)
    y_ctx = _post0(ctx, o_c, mc0, False, g2, w_o, w_in, w_out, 0, tm_c)

    cos256, sin256 = _rope_tables(s, RET_QK_DIM)
    g1 = norm1_g[1][None, :]
    g2 = norm2_g[1][None, :]
    q1, k1, v1 = _proj1(x1, mx1, True, g1, w_qkvg, (cos256, sin256), tp_x)
    k1c, v1c = _proj1(y_ctx, mc1, False, g1, w_qkvg, None, tp_c)
    y = _retention(ret_decay_logit[0].astype(F32), q1, k1, v1, k1c, v1c)
    return _post1(x1, y, mx1, g1, g2, ret_gn_g[0][None, :], w_qkvg, w_ro,
                  w_in, w_out, 1, tm_x)
```

```python
import functools

import jax
import jax.numpy as jnp
import numpy as np
from jax import lax
from jax.experimental import pallas as pl
from jax.experimental.pallas import tpu as pltpu

F32 = jnp.float32
BF16 = jnp.bfloat16

D_MODEL = 1024
GRID_W = 64
HEAD_DIM = 64
N_HEADS = D_MODEL // HEAD_DIM
N_KV_HEADS = N_HEADS // 4
WINDOW = 128
ATTN_BLOCK = 128
assert WINDOW == ATTN_BLOCK
RET_HEADS = 4
RET_QK_DIM = 256
RET_V_DIM = 512
RET_VWIDTH = 2 * D_MODEL
RET_CHUNK = 256
ATTN_PAIRS_X = 2
ATTN_PAIRS_C = 2
ATTN_BLOCKS_PER_STEP = 4
ATTN_LOOKAHEAD = 2
RET_SEQS = 2
RET_LOOKAHEAD = 2
D_FF = 2816
MXU_TILE = 256
FF_CHUNKS = ((0, 6 * MXU_TILE), (6 * MXU_TILE, D_FF))
ROW_PARTS = 2
ROPE_BASE = 10000.0
EPS = 1e-6
NEG_INF = -1e30
LOG2E = 1.4426950408889634
LANES = 128
SUBLANES = 8
BF16_ROWS = 2 * SUBLANES
MODS_TILE = 6 * MXU_TILE

VMEM_LIMIT = 56 * 1024 * 1024
FFN_TILE = 512
PROJ_TILE = 1024


def _dot(a, b):
    return jnp.dot(a, b, preferred_element_type=F32)


def _dot_nt(a, b):
    return lax.dot_general(a, b, (((1,), (1,)), ((), ())), preferred_element_type=F32)


def _dot_tn(a, b):
    return lax.dot_general(a, b, (((0,), (0,)), ((), ())), preferred_element_type=F32)


def _rms(x, g):
    return x * lax.rsqrt(jnp.mean(x * x, axis=-1, keepdims=True) + EPS) * g


def _silu(x):
    return x * jax.nn.sigmoid(x)


def _params(n_axes):
    return pltpu.CompilerParams(dimension_semantics=("arbitrary",) * n_axes, vmem_limit_bytes=VMEM_LIMIT)


def _const2(shape):
    return pl.BlockSpec(shape, lambda b, j: (0,) * len(shape), pipeline_mode=pl.Buffered(1))


def _layer_spec(stacked, layer):
    return pl.BlockSpec((None,) + stacked.shape[1:], lambda b, j: (layer,) + (0,) * (stacked.ndim - 1),
                        pipeline_mode=pl.Buffered(1))


def _mod_spec(k, per_batch):
    if per_batch:
        return pl.BlockSpec((None, 1, D_MODEL), lambda b, j, k=k: (b, 0, k))
    return pl.BlockSpec((None, 1, D_MODEL), lambda b, j, k=k: (0, 0, k))


def _tok_spec(tm, width):
    return pl.BlockSpec((None, tm, width), lambda b, j: (b, j, 0))


def _row_parts(ref, n):
    tm = ref.shape[0]
    n = n if tm % (n * ATTN_BLOCK) == 0 else 1
    return [slice(i * tm // n, (i + 1) * tm // n) for i in range(n)]


def _mods_kernel(c_ref, w_ref, b_ref, o_ref):
    a = _silu(c_ref[...]).astype(BF16)
    o_ref[...] = _dot(a, w_ref[...].astype(BF16)) + b_ref[...]


def _mods(cin, ada_w, ada_b):
    depth, d, n = ada_w.shape
    rows = cin.shape[0]
    tn = MODS_TILE
    return pl.pallas_call(
        _mods_kernel,
        grid=(depth, n // tn),
        in_specs=[
            pl.BlockSpec((rows, d), lambda i, j: (0, 0)),
            pl.BlockSpec((None, d, tn), lambda i, j: (i, 0, j)),
            pl.BlockSpec((None, 1, tn), lambda i, j: (i, 0, j)),
        ],
        out_specs=pl.BlockSpec((None, rows, tn), lambda i, j: (i, 0, j)),
        out_shape=jax.ShapeDtypeStruct((depth, rows, n), F32),
        compiler_params=_params(2),
        name="adaln_mods",
    )(cin, ada_w, ada_b.reshape(depth, 1, n))


def _dup_halves(r):
    lane = lax.broadcasted_iota(jnp.int32, r.shape, 1)
    lo = lane < HEAD_DIM
    r64 = pltpu.roll(r, HEAD_DIM, 1)
    return jnp.where(lo, r, r64), jnp.where(lo, r64, r)


def _qkv0_kernel(x_ref, sh_ref, sc_ref, g1_ref, w_ref, gain_ref, gsum_ref, cos_ref, sin_ref, *rest):
    n_side = (len(rest) - 3) // 2
    q_ref, kk_ref, vt_ref = rest[n_side:n_side + 3]
    for src_ref, dst_ref in zip(rest[:n_side], rest[n_side + 3:]):
        dst_ref[...] = src_ref[...].astype(BF16)
    qd = N_HEADS * HEAD_DIM
    qkd = qd + N_KV_HEADS * HEAD_DIM
    parts = _row_parts(x_ref, ROW_PARTS)
    hs = [_rms(x_ref[rows, :], g1_ref[...]) * (1.0 + sc_ref[...]) + sh_ref[...] for rows in parts]
    qkvs = [_dot(h.astype(BF16), w_ref[...]) for h in hs]
    invfs = []
    for qkv in qkvs:
        sq = (qkv[:, :qkd] * qkv[:, :qkd]).astype(BF16)
        ms = jnp.concatenate([_dot(sq[:, c:c + MXU_TILE], gsum_ref[...]) for c in range(0, qkd, MXU_TILE)], axis=1)
        invfs.append(lax.rsqrt(ms + EPS))
    row = lax.broadcasted_iota(jnp.int32, (LANES - HEAD_DIM, ATTN_BLOCK), 0)
    tail = jnp.where(row == 0, 1.0, 0.0).astype(BF16)
    for rows, qkv, invf in zip(parts, qkvs, invfs):
        t = qkv[:, :qkd] * gain_ref[...]
        cos = cos_ref[rows, :]
        sin = sin_ref[rows, :]
        half = HEAD_DIM // 4
        lane = lax.broadcasted_iota(jnp.int32, cos.shape, 1)
        first = (lane % (2 * half)) < half
        for cb in range(qkd // LANES):
            sl = slice(cb * LANES, (cb + 1) * LANES)
            tb = t[:, sl]
            sw = jnp.where(first, pltpu.roll(tb, LANES - half, 1), pltpu.roll(tb, half, 1))
            r = (tb * cos + sw * sin) * invf[:, sl]
            if cb < qd // LANES:
                q_ref[rows, sl] = r.astype(BF16)
            else:
                p = cb - qd // LANES
                a, b = _dup_halves(r)
                kk_ref[rows, (2 * p) * LANES:(2 * p + 1) * LANES] = a.astype(BF16)
                kk_ref[rows, (2 * p + 1) * LANES:(2 * p + 2) * LANES] = b.astype(BF16)
        vt = qkv[:, qkd:].T
        for blk in range(vt.shape[1] // ATTN_BLOCK):
            cols = slice(blk * ATTN_BLOCK, (blk + 1) * ATTN_BLOCK)
            slab = rows.start // ATTN_BLOCK + blk
            for g in range(N_KV_HEADS):
                vt_ref[slab, g * LANES:g * LANES + HEAD_DIM, :] = vt[g * HEAD_DIM:(g + 1) * HEAD_DIM, cols].astype(BF16)
                vt_ref[slab, g * LANES + HEAD_DIM:(g + 1) * LANES, :] = tail


def _row_blocks(rows, steps):
    return next(n for n in range(min(steps, rows // BF16_ROWS), 0, -1)
                if rows % n == 0 and (rows // n) % BF16_ROWS == 0)


def _qkv0(xin, mods, per_batch, g1, w, gain, gsum, cos, sin, tm, side=()):
    b, t, d = xin.shape
    kw = N_KV_HEADS * LANES
    nj = t // tm
    flat = [a.reshape(-1, a.shape[-1]) for a in side]
    side_specs = []
    for a in flat:
        nblk = _row_blocks(a.shape[0], b * nj)
        side_specs.append(pl.BlockSpec((a.shape[0] // nblk, a.shape[1]),
                                       lambda b, j, nblk=nblk: (jnp.minimum(b * nj + j, nblk - 1), 0)))
    outs = pl.pallas_call(
        _qkv0_kernel,
        grid=(b, nj),
        in_specs=[
            _tok_spec(tm, d),
            _mod_spec(0, per_batch),
            _mod_spec(1, per_batch),
            _const2((1, d)),
            _const2(w.shape),
            _const2(gain.shape),
            _const2(gsum.shape),
            pl.BlockSpec((tm, LANES), lambda b, j: (j, 0)),
            pl.BlockSpec((tm, LANES), lambda b, j: (j, 0)),
            *side_specs,
        ],
        out_specs=[_tok_spec(tm, d), _tok_spec(tm, kw),
                   pl.BlockSpec((None, tm // ATTN_BLOCK, kw, ATTN_BLOCK), lambda b, j: (b, j, 0, 0)), *side_specs],
        out_shape=[
            jax.ShapeDtypeStruct((b, t, d), BF16),
            jax.ShapeDtypeStruct((b, t, kw), BF16),
            jax.ShapeDtypeStruct((b, t // ATTN_BLOCK, kw, ATTN_BLOCK), BF16),
            *[jax.ShapeDtypeStruct(a.shape, BF16) for a in flat],
        ],
        compiler_params=_params(2),
        name="attn_qkv_proj",
    )(xin, mods, mods, g1, w, gain, gsum, cos, sin, *flat)
    return list(outs[:3]) + [o.reshape(a.shape) for o, a in zip(outs[3:], side)]


def _band_masks(j, nb, heads):
    shape = (ATTN_BLOCK, heads * ATTN_BLOCK)
    key = lax.broadcasted_iota(jnp.int32, shape, 0)
    qry = lax.broadcasted_iota(jnp.int32, shape, 1) % ATTN_BLOCK
    return (key >= qry) & (j > 0), (key <= qry) & (j < nb - 1)


def _attn_scores(sink_ref, q_ref, key_refs, pr, pairs, masks):
    tq = q_ref.shape[0]
    g = pr // (N_HEADS // N_KV_HEADS // 2)
    gs = slice(g * LANES, (g + 1) * LANES)
    lo = lax.broadcasted_iota(jnp.int32, (tq, LANES), 1) < HEAD_DIM
    zero = jnp.zeros((tq, LANES), BF16)
    stack = []
    for p in range(pr, pr + pairs):
        qb = q_ref[:, p * LANES:(p + 1) * LANES]
        stack += [jnp.where(lo, qb, zero), jnp.where(lo, zero, qb)]
    keys = jnp.concatenate([r[:, gs] for r in key_refs], axis=0) if len(key_refs) > 1 else key_refs[0][:, gs]
    st = _dot_nt(keys, jnp.concatenate(stack, axis=0))
    parts = [st[i * LANES:(i + 1) * LANES, :] for i in range(st.shape[0] // LANES)]
    if masks is not None:
        ia = key_refs[0].shape[0] // LANES
        parts[ia] = jnp.where(masks[0], parts[ia], NEG_INF)
        parts[ia + 2] = jnp.where(masks[1], parts[ia + 2], NEG_INF)
    sink = jnp.concatenate([jnp.full((1, tq), sink_ref[2 * pr + i] * LOG2E, F32) for i in range(2 * pairs)], axis=1)
    mx = parts[0]
    for part in parts[1:]:
        mx = jnp.maximum(mx, part)
    m = jnp.maximum(jnp.max(mx, axis=0, keepdims=True), sink)
    return jnp.concatenate([jnp.exp2(part - m).astype(BF16) for part in parts], axis=0), jnp.exp2(sink - m)


def _attn_values(probs, val_refs, o_ref, pr, pairs):
    pt, p_sink = probs
    tq = o_ref.shape[0]
    g = pr // (N_HEADS // N_KV_HEADS // 2)
    gs = slice(g * LANES, (g + 1) * LANES)
    vt = jnp.concatenate([r[i, gs, :] for r in val_refs for i in range(r.shape[0])], axis=1)
    ot = _dot(vt, pt)
    den = jnp.sum(ot[HEAD_DIM:HEAD_DIM + SUBLANES, :], axis=0, keepdims=True) + p_sink
    out = ot[:HEAD_DIM, :] / den
    for i in range(pairs):
        pair = jnp.concatenate([out[:, 2 * i * tq:(2 * i + 1) * tq], out[:, (2 * i + 1) * tq:(2 * i + 2) * tq]], axis=0)
        o_ref[:, (pr + i) * LANES:(pr + i + 1) * LANES] = pair.T.astype(BF16)


def _attn_pipeline(sink_ref, tasks, pairs):
    items = [(task, g) for task in tasks for g in range(0, N_HEADS // 2, pairs)]
    probs = {}
    for step in range(len(items) + ATTN_LOOKAHEAD):
        if step < len(items):
            (q_ref, key_refs, _, _, masks), g = items[step]
            probs[step] = _attn_scores(sink_ref, q_ref, key_refs, g, pairs, masks)
        done = step - ATTN_LOOKAHEAD
        if done >= 0:
            (_, _, val_refs, o_ref, _), g = items[done]
            _attn_values(probs.pop(done), val_refs, o_ref, g, pairs)


def _attn_ctx_kernel(sink_ref, q_ref, kc_ref, vc_ref, o_ref, *, pairs):
    _attn_pipeline(sink_ref, [(q_ref, [kc_ref], [vc_ref], o_ref, None)], pairs)


def _attn_band_kernel(sink_ref, q_ref, kc_ref, k_ref, vc_ref, vt_ref, o_ref, *, nb, pairs):
    tasks = []
    for t in range(q_ref.shape[0] // ATTN_BLOCK):
        j = pl.program_id(1) * (q_ref.shape[0] // ATTN_BLOCK) + t
        rows = pl.ds(t * ATTN_BLOCK, ATTN_BLOCK)
        band = [jnp.maximum(j - 1, 0), j, jnp.minimum(j + 1, nb - 1)]
        key_refs = [kc_ref] + [k_ref.at[pl.ds(pl.multiple_of(blk * ATTN_BLOCK, ATTN_BLOCK), ATTN_BLOCK), :]
                               for blk in band]
        val_refs = [vc_ref] + [vt_ref.at[pl.ds(blk, 1)] for blk in band]
        tasks.append((q_ref.at[rows, :], key_refs, val_refs, o_ref.at[rows, :], _band_masks(j, nb, 2 * pairs)))
    _attn_pipeline(sink_ref, tasks, pairs)


def _attn_x(sink, q, kk, vv, kkc, vvc):
    b, s, d = q.shape
    nb = s // ATTN_BLOCK
    rows = ATTN_BLOCK * max(n for n in range(1, ATTN_BLOCKS_PER_STEP + 1) if nb % n == 0)
    per_batch = lambda a: pl.BlockSpec((None,) + a.shape[1:], lambda b, j: (b,) + (0,) * (a.ndim - 1))
    blk = pl.BlockSpec((None, rows, d), lambda b, j: (b, j, 0))
    return pl.pallas_call(
        functools.partial(_attn_band_kernel, nb=nb, pairs=ATTN_PAIRS_X),
        grid=(b, s // rows),
        in_specs=[pl.BlockSpec(memory_space=pltpu.SMEM), blk,
                  per_batch(kkc), per_batch(kk), per_batch(vvc), per_batch(vv)],
        out_specs=blk,
        out_shape=jax.ShapeDtypeStruct((b, s, d), BF16),
        compiler_params=_params(2),
        name="attn_latent",
    )(sink, q, kkc, kk, vvc, vv)


def _attn_c(sink, qc, kkc, vvc):
    b, l, d = qc.shape
    return pl.pallas_call(
        functools.partial(_attn_ctx_kernel, pairs=ATTN_PAIRS_C),
        grid=(b,),
        in_specs=[
            pl.BlockSpec(memory_space=pltpu.SMEM),
            pl.BlockSpec((None, l, d), lambda b: (b, 0, 0)),
            pl.BlockSpec((None, l, kkc.shape[2]), lambda b: (b, 0, 0)),
            pl.BlockSpec((None,) + vvc.shape[1:], lambda b: (b, 0, 0, 0)),
        ],
        out_specs=pl.BlockSpec((None, l, d), lambda b: (b, 0, 0)),
        out_shape=jax.ShapeDtypeStruct((b, l, d), BF16),
        compiler_params=_params(1),
        name="attn_context",
    )(sink, qc, kkc, vvc)


def _ffn_tail(x1s, parts, shf_ref, scf_ref, gtf_ref, g2_ref, win_ref, wout_ref, out_ref):
    h2s = [(_rms(x1, g2_ref[...]) * (1.0 + scf_ref[...]) + shf_ref[...]).astype(BF16) for x1 in x1s]
    accs = [None] * len(x1s)
    hidden = {}
    for step in range(len(FF_CHUNKS) + 1):
        if step < len(FF_CHUNKS):
            lo, hi = FF_CHUNKS[step]
            hidden[step] = [(_dot(h2, win_ref[:, lo:hi]), _dot(h2, win_ref[:, D_FF + lo:D_FF + hi])) for h2 in h2s]
        if step > 0:
            lo, hi = FF_CHUNKS[step - 1]
            for i, (gate, up) in enumerate(hidden.pop(step - 1)):
                down = _dot((_silu(gate) * up).astype(BF16), wout_ref[lo:hi, :])
                accs[i] = down if accs[i] is None else accs[i] + down
    for rows, x1, acc in zip(parts, x1s, accs):
        out_ref[rows, :] = x1 + gtf_ref[...] * acc


def _post0_kernel(x_ref, o_ref, gtm_ref, shf_ref, scf_ref, gtf_ref, g2_ref, wo_ref, win_ref, wout_ref, out_ref):
    parts = _row_parts(x_ref, ROW_PARTS)
    mixed = [_dot(o_ref[rows, :], wo_ref[...]) for rows in parts]
    x1s = [x_ref[rows, :] + gtm_ref[...] * a for rows, a in zip(parts, mixed)]
    _ffn_tail(x1s, parts, shf_ref, scf_ref, gtf_ref, g2_ref, win_ref, wout_ref, out_ref)


def _post0(xin, o, mods, per_batch, g2, wo, win, wout, layer, tm):
    b, t, d = xin.shape
    return pl.pallas_call(
        _post0_kernel,
        grid=(b, t // tm),
        in_specs=[
            _tok_spec(tm, d), _tok_spec(tm, d),
            _mod_spec(2, per_batch), _mod_spec(3, per_batch), _mod_spec(4, per_batch), _mod_spec(5, per_batch),
            _const2((1, d)), _const2(wo.shape), _layer_spec(win, layer), _layer_spec(wout, layer),
        ],
        out_specs=_tok_spec(tm, d),
        out_shape=jax.ShapeDtypeStruct((b, t, d), F32),
        compiler_params=_params(2),
        name="attn_out_ffn",
    )(xin, o, mods, mods, mods, mods, g2, wo, win, wout)


def _post1_kernel(x_ref, y_ref, shm_ref, scm_ref, gtm_ref, shf_ref, scf_ref, gtf_ref, g1_ref, g2_ref, gn_ref,
                  wg_ref, wo_ref, win_ref, wout_ref, out_ref):
    parts = _row_parts(x_ref, ROW_PARTS)
    xs = [x_ref[rows, :] for rows in parts]
    h1s = [(_rms(x, g1_ref[...]) * (1.0 + scm_ref[...]) + shm_ref[...]).astype(BF16) for x in xs]
    gates = [_dot(h1, wg_ref[...]) for h1 in h1s]
    gated = []
    for rows, gate in zip(parts, gates):
        normed = []
        for h in range(RET_HEADS):
            o = y_ref[rows, h * RET_V_DIM:(h + 1) * RET_V_DIM].astype(F32)
            d = o - jnp.mean(o, axis=-1, keepdims=True)
            normed.append(d * lax.rsqrt(jnp.mean(d * d, axis=-1, keepdims=True) + EPS))
        gated.append((_silu(gate) * (jnp.concatenate(normed, axis=1) * gn_ref[...])).astype(BF16))
    mixed = [_dot(g, wo_ref[...]) for g in gated]
    x1s = [x + gtm_ref[...] * a for x, a in zip(xs, mixed)]
    _ffn_tail(x1s, parts, shf_ref, scf_ref, gtf_ref, g2_ref, win_ref, wout_ref, out_ref)


def _post1(xin, y, mods, g1, g2, gn, wg, wo, win, wout, layer, tm):
    b, t, d = xin.shape
    return pl.pallas_call(
        _post1_kernel,
        grid=(b, t // tm),
        in_specs=[
            _tok_spec(tm, d), _tok_spec(tm, y.shape[2]),
            *[_mod_spec(k, True) for k in range(6)],
            _const2((1, d)), _const2((1, d)), _const2(gn.shape),
            pl.BlockSpec((d, RET_VWIDTH), lambda b, j: (0, wg.shape[1] // RET_VWIDTH - 1), pipeline_mode=pl.Buffered(1)),
            _const2(wo.shape), _layer_spec(win, layer), _layer_spec(wout, layer),
        ],
        out_specs=_tok_spec(tm, d),
        out_shape=jax.ShapeDtypeStruct((b, t, d), F32),
        compiler_params=_params(2),
        name="ret_out_ffn",
    )(xin, y, mods, mods, mods, mods, mods, mods, g1, g2, gn, wg, wo, win, wout)


def _proj1_kernel(x_ref, sh_ref, sc_ref, g1_ref, w_ref, cos_ref, sin_ref, q_ref, k_ref, v_ref):
    qk_w = 2 * RET_HEADS * RET_QK_DIM
    hb = (_rms(x_ref[...], g1_ref[...]) * (1.0 + sc_ref[...]) + sh_ref[...]).astype(BF16)
    qk = _dot(hb, w_ref[:, :qk_w])
    k_scale = RET_QK_DIM ** -0.5
    for cb in range(qk_w // LANES):
        ts = slice((cb % 2) * LANES, (cb % 2 + 1) * LANES)
        tb = qk[:, cb * LANES:(cb + 1) * LANES]
        r = tb * cos_ref[:, ts] + pltpu.roll(tb, LANES // 2, 1) * sin_ref[:, ts]
        if cb < qk_w // (2 * LANES):
            q_ref[:, cb * LANES:(cb + 1) * LANES] = r.astype(BF16)
        else:
            cc = cb - qk_w // (2 * LANES)
            k_ref[:, cc * LANES:(cc + 1) * LANES] = (r * k_scale).astype(BF16)
    v_ref[...] = _dot(hb, w_ref[:, qk_w:]).astype(BF16)


def _proj1_ctx_kernel(x_ref, sh_ref, sc_ref, g1_ref, w_ref, k_ref, v_ref):
    qw = RET_HEADS * RET_QK_DIM
    hb = (_rms(x_ref[...], g1_ref[...]) * (1.0 + sc_ref[...]) + sh_ref[...]).astype(BF16)
    kv = _dot(hb, w_ref[:, qw:])
    k_ref[...] = (kv[:, :qw] * RET_QK_DIM ** -0.5).astype(BF16)
    v_ref[...] = kv[:, qw:].astype(BF16)


def _proj1(xin, mods, per_batch, g1, w, tables, tm):
    b, t, d = xin.shape
    qw = RET_HEADS * RET_QK_DIM
    specs = [
        _tok_spec(tm, d),
        _mod_spec(0, per_batch), _mod_spec(1, per_batch),
        _const2((1, d)),
        pl.BlockSpec((d, 2 * qw + RET_VWIDTH), lambda b, j: (0, 0), pipeline_mode=pl.Buffered(1)),
    ]
    outs = [(qw, _tok_spec(tm, qw)), (RET_VWIDTH, _tok_spec(tm, RET_VWIDTH))]
    if tables is not None:
        specs += [pl.BlockSpec((tm, 2 * LANES), lambda b, j: (j, 0))] * 2
        outs = [(qw, _tok_spec(tm, qw))] + outs
    return pl.pallas_call(
        _proj1_ctx_kernel if tables is None else _proj1_kernel,
        grid=(b, t // tm),
        in_specs=specs,
        out_specs=[spec for _, spec in outs],
        out_shape=[jax.ShapeDtypeStruct((b, t, width), BF16) for width, _ in outs],
        compiler_params=_params(2),
        name="ret_qkv_proj",
    )(xin, mods, mods, g1, w, *(tables or ()))


def _ret_kernel(dl_ref, q_ref, k_ref, v_ref, kc_ref, vc_ref, y_ref,
                o_scr, sf_scr, sb_scr, comb_scr, qdf_scr, qdb_scr, kdf_scr, kdb_scr, cdf_scr, cdb_scr,
                ctxf_scr, ctxb_scr, *, nc):
    hd = pl.program_id(0)
    c_len = RET_CHUNK
    l_ctx = kc_ref.shape[1]

    @pl.when(pl.program_id(1) == 0)
    def _decay_tables():
        def log_decay(direction, shp):
            logit = jnp.full(shp, dl_ref[direction, hd], F32)
            return jnp.minimum(logit, 0.0) - jnp.log1p(jnp.exp(-jnp.abs(logit)))

        shape = (c_len, c_len)
        n = lax.broadcasted_iota(jnp.int32, shape, 0).astype(F32)
        m = lax.broadcasted_iota(jnp.int32, shape, 1).astype(F32)
        comb_scr[...] = (jnp.where(n >= m, jnp.exp(log_decay(0, shape) * jnp.maximum(n - m, 0.0)), 0.0)
                         + jnp.where(m >= n, jnp.exp(log_decay(1, shape) * jnp.maximum(m - n, 0.0)), 0.0))
        nk = lax.broadcasted_iota(jnp.int32, (c_len, RET_QK_DIM), 0).astype(F32)
        nv = lax.broadcasted_iota(jnp.int32, (c_len, RET_V_DIM), 0).astype(F32)
        qdf_scr[...] = jnp.exp(log_decay(0, nv.shape) * (nv + 1.0))
        kdf_scr[...] = jnp.exp(log_decay(0, nk.shape) * (c_len - 1.0 - nk))
        cdf_scr[...] = jnp.exp(log_decay(0, (1, RET_V_DIM)) * float(c_len))
        qdb_scr[...] = jnp.exp(log_decay(1, nv.shape) * (c_len - nv))
        kdb_scr[...] = jnp.exp(log_decay(1, nk.shape) * nk)
        cdb_scr[...] = jnp.exp(log_decay(1, (1, RET_V_DIM)) * float(c_len))
        tt = lax.broadcasted_iota(jnp.int32, (l_ctx, RET_QK_DIM), 0).astype(F32)
        ctxf_scr[...] = jnp.exp(log_decay(0, tt.shape) * (l_ctx - 1.0 - tt))
        ctxb_scr[...] = jnp.exp(log_decay(1, tt.shape) * tt)

    seqs = range(q_ref.shape[0])

    for n in seqs:
        kc = kc_ref[n].astype(F32)
        sf_scr[n] = _dot_tn((kc * ctxf_scr[...]).astype(BF16), vc_ref[n])
        sb_scr[n] = _dot_tn((kc * ctxb_scr[...]).astype(BF16), vc_ref[n])

    def rows(c):
        return slice(c * c_len, (c + 1) * c_len)

    def masked_scores(n, c):
        return (_dot_nt(q_ref[n, rows(c), :], k_ref[n, rows(c), :]) * comb_scr[...]).astype(BF16)

    def forward(n, c, scores):
        qc, kx, vx = q_ref[n, rows(c), :], k_ref[n, rows(c), :], v_ref[n, rows(c), :]
        st = sf_scr[n]
        out = _dot(scores, vx) + _dot(qc, st.astype(BF16)) * qdf_scr[...]
        sf_scr[n] = st * cdf_scr[...] + _dot_tn((kx.astype(F32) * kdf_scr[...]).astype(BF16), vx)
        return out

    def backward(n, c):
        qc, kx, vx = q_ref[n, rows(c), :], k_ref[n, rows(c), :], v_ref[n, rows(c), :]
        st = sb_scr[n]
        out = _dot(qc, st.astype(BF16)) * qdb_scr[...]
        sb_scr[n] = st * cdb_scr[...] + _dot_tn((kx.astype(F32) * kdb_scr[...]).astype(BF16), vx)
        return out

    scores = {}
    for step in range(nc + RET_LOOKAHEAD):
        if step < nc:
            for n in seqs:
                scores[n, step] = masked_scores(n, step)
        i = step - RET_LOOKAHEAD
        if i < 0:
            continue
        for n in seqs:
            for c, out in ((i, forward(n, i, scores.pop((n, i)))), (nc - 1 - i, backward(n, nc - 1 - i))):
                if i < nc // 2:
                    o_scr[n, rows(c), :] = out
                else:
                    y_ref[n, rows(c), :] = (o_scr[n, rows(c), :] + out).astype(BF16)


def _retention(dl, q, k, v, kc, vc):
    b, s, _ = q.shape
    l = kc.shape[1]
    nc = s // RET_CHUNK
    assert nc % 2 == 0
    seqs = RET_SEQS if b % RET_SEQS == 0 else 1
    hspec = lambda t, w: pl.BlockSpec((seqs, t, w), lambda h, b: (b, 0, h))
    return pl.pallas_call(
        functools.partial(_ret_kernel, nc=nc),
        grid=(RET_HEADS, b // seqs),
        in_specs=[
            pl.BlockSpec(memory_space=pltpu.SMEM),
            hspec(s, RET_QK_DIM), hspec(s, RET_QK_DIM), hspec(s, RET_V_DIM),
            hspec(l, RET_QK_DIM), hspec(l, RET_V_DIM),
        ],
        out_specs=hspec(s, RET_V_DIM),
        out_shape=jax.ShapeDtypeStruct((b, s, RET_VWIDTH), BF16),
        scratch_shapes=[
            pltpu.VMEM((seqs, s, RET_V_DIM), F32),
            pltpu.VMEM((seqs, RET_QK_DIM, RET_V_DIM), F32),
            pltpu.VMEM((seqs, RET_QK_DIM, RET_V_DIM), F32),
            pltpu.VMEM((RET_CHUNK, RET_CHUNK), F32),
            pltpu.VMEM((RET_CHUNK, RET_V_DIM), F32),
            pltpu.VMEM((RET_CHUNK, RET_V_DIM), F32),
            pltpu.VMEM((RET_CHUNK, RET_QK_DIM), F32),
            pltpu.VMEM((RET_CHUNK, RET_QK_DIM), F32),
            pltpu.VMEM((1, RET_V_DIM), F32),
            pltpu.VMEM((1, RET_V_DIM), F32),
            pltpu.VMEM((l, RET_QK_DIM), F32),
            pltpu.VMEM((l, RET_QK_DIM), F32),
        ],
        compiler_params=_params(2),
        name="retention",
    )(dl, q, k, v, kc, vc)


def _rope_tables(n, head_dim):
    rows = n // GRID_W
    row = np.broadcast_to(np.arange(rows, dtype=np.float32)[:, None], (rows, GRID_W)).reshape(n)
    col = np.broadcast_to(np.arange(GRID_W, dtype=np.float32)[None, :], (rows, GRID_W)).reshape(n)
    axis_dim = head_dim // 2
    inv = (ROPE_BASE ** (-np.arange(0, axis_dim, 2, dtype=np.float32) / axis_dim)).astype(np.float32)
    ang_r = row[:, None] * inv
    ang_c = col[:, None] * inv
    cos = np.concatenate([np.cos(ang_r)] * 2 + [np.cos(ang_c)] * 2, axis=-1)
    sin = np.concatenate([-np.sin(ang_r), np.sin(ang_r), -np.sin(ang_c), np.sin(ang_c)], axis=-1)
    return cos.astype(np.float32), sin.astype(np.float32)


def kernel(x, c, ctx, c_ctx, ada_w, ada_b, norm1_g, norm2_g, ffn_w_in, ffn_w_out, attn_w_qkv, attn_q_norm,
           attn_k_norm, attn_sink, attn_w_o, ret_w_qkvg, ret_decay_logit, ret_gn_g, ret_w_o):
    b, s, d = x.shape
    l = ctx.shape[1]
    tm_x = min(FFN_TILE, s)
    tm_c = min(FFN_TILE, l)
    tp_x = min(PROJ_TILE, s)
    tp_c = min(PROJ_TILE, l)

    rows = -(-(b + 1) // SUBLANES) * SUBLANES
    cin = jnp.concatenate([c, c_ctx[None, :], jnp.zeros((rows - b - 1, d), F32)], axis=0)
    mods = _mods(cin, ada_w, ada_b)
    mx0 = mods[0, :b].reshape(b, 1, 6 * d)
    mc0 = mods[0, b:b + 1].reshape(1, 1, 6 * d)
    mx1 = mods[1, :b].reshape(b, 1, 6 * d)
    mc1 = mods[1, b:b + 1].reshape(1, 1, 6 * d)

    w_qkv = attn_w_qkv[0].astype(BF16)
    scale = HEAD_DIM ** -0.5 * LOG2E
    gain = jnp.concatenate([jnp.tile(attn_q_norm[0] * scale, N_HEADS), jnp.tile(attn_k_norm[0], N_KV_HEADS)])[None, :]
    lane_head = np.arange(MXU_TILE) // HEAD_DIM
    gsum = jnp.asarray((lane_head[:, None] == lane_head[None, :]) / HEAD_DIM, BF16)
    cos64, sin64 = _rope_tables(s, HEAD_DIM)
    cos_x = np.tile(cos64, (1, LANES // HEAD_DIM))
    sin_x = np.tile(sin64, (1, LANES // HEAD_DIM))
    cos_c = np.ones((l, LANES), np.float32)
    sin_c = np.zeros((l, LANES), np.float32)
    g1 = norm1_g[0][None, :]
    g2 = norm2_g[0][None, :]
    q_x, kk_x, vv_x, w_o, w_in, w_out, w_qkvg, w_ro = _qkv0(
        x, mx0, True, g1, w_qkv, gain, gsum, cos_x, sin_x, tp_x,
        side=(attn_w_o[0], ffn_w_in, ffn_w_out, ret_w_qkvg[0], ret_w_o[0]))
    q_c, kk_c, vv_c = _qkv0(ctx, mc0, False, g1, w_qkv, gain, gsum, cos_c, sin_c, tp_c)
    sink = attn_sink[0].astype(F32)
    o_x = _attn_x(sink, q_x, kk_x, vv_x, kk_c, vv_c)
    o_c = _attn_c(sink, q_c, kk_c, vv_c)
    x1 = _post0(x, o_x, mx0, True, g2, w_o, w_in, w_out, 0, tm_x)
    y_ctx = _post0(ctx, o_c, mc0, False, g2, w_o, w_in, w_out, 0, tm_c)

    cos256, sin256 = _rope_tables(s, RET_QK_DIM)
    g1 = norm1_g[1][None, :]
    g2 = norm2_g[1][None, :]
    q1, k1, v1 = _proj1(x1, mx1, True, g1, w_qkvg, (cos256, sin256), tp_x)
    k1c, v1c = _proj1(y_ctx, mc1, False, g1, w_qkvg, None, tp_c)
    y = _retention(ret_decay_logit[0].astype(F32), q1, k1, v1, k1c, v1c)
    return _post1(x1, y, mx1, g1, g2, ret_gn_g[0][None, :], w_qkvg, w_ro,
                  w_in, w_out, 1, tm_x)
```

```python
import functools

import jax
import jax.numpy as jnp
import numpy as np
from jax import lax
from jax.experimental import pallas as pl
from jax.experimental.pallas import tpu as pltpu

F32 = jnp.float32
BF16 = jnp.bfloat16

D_MODEL = 1024
GRID_W = 64
HEAD_DIM = 64
N_HEADS = D_MODEL // HEAD_DIM
N_KV_HEADS = N_HEADS // 4
WINDOW = 128
ATTN_BLOCK = 128
assert WINDOW == ATTN_BLOCK
RET_HEADS = 4
RET_QK_DIM = 256
RET_V_DIM = 512
RET_VWIDTH = 2 * D_MODEL
RET_CHUNK = 256
ATTN_PAIRS = N_HEADS // N_KV_HEADS // 2
ATTN_BLOCKS_PER_STEP = 4
ATTN_LOOKAHEAD = 1
RET_SEQS = 2
RET_LOOKAHEAD = 2
D_FF = 2816
MXU_TILE = 256
FF_CHUNKS = ((0, 6 * MXU_TILE), (6 * MXU_TILE, D_FF))
ROW_PARTS = 2
ROPE_BASE = 10000.0
EPS = 1e-6
NEG_INF = -1e30
LOG2E = 1.4426950408889634
LANES = 128
SUBLANES = 8
BF16_ROWS = 2 * SUBLANES
MODS_TILE = 6 * MXU_TILE

VMEM_LIMIT = 56 * 1024 * 1024
FFN_TILE = 512
PROJ_TILE = 1024


def _dot(a, b):
    return jnp.dot(a, b, preferred_element_type=F32)


def _dot_nt(a, b):
    return lax.dot_general(a, b, (((1,), (1,)), ((), ())), preferred_element_type=F32)


def _dot_tn(a, b):
    return lax.dot_general(a, b, (((0,), (0,)), ((), ())), preferred_element_type=F32)


def _rms(x, g):
    return x * lax.rsqrt(jnp.mean(x * x, axis=-1, keepdims=True) + EPS) * g


def _silu(x):
    return x * jax.nn.sigmoid(x)


def _params(n_axes):
    return pltpu.CompilerParams(dimension_semantics=("arbitrary",) * n_axes, vmem_limit_bytes=VMEM_LIMIT)


def _const2(shape):
    return pl.BlockSpec(shape, lambda b, j: (0,) * len(shape), pipeline_mode=pl.Buffered(1))


def _layer_spec(stacked, layer):
    return pl.BlockSpec((None,) + stacked.shape[1:], lambda b, j: (layer,) + (0,) * (stacked.ndim - 1),
                        pipeline_mode=pl.Buffered(1))


def _mod_spec(k, per_batch):
    if per_batch:
        return pl.BlockSpec((None, 1, D_MODEL), lambda b, j, k=k: (b, 0, k))
    return pl.BlockSpec((None, 1, D_MODEL), lambda b, j, k=k: (0, 0, k))


def _tok_spec(tm, width):
    return pl.BlockSpec((None, tm, width), lambda b, j: (b, j, 0))


def _row_parts(ref, n):
    tm = ref.shape[0]
    n = n if tm % (n * ATTN_BLOCK) == 0 else 1
    return [slice(i * tm // n, (i + 1) * tm // n) for i in range(n)]


def _mods_kernel(c_ref, w_ref, b_ref, o_ref):
    a = _silu(c_ref[...]).astype(BF16)
    o_ref[...] = _dot(a, w_ref[...].astype(BF16)) + b_ref[...]


def _mods(cin, ada_w, ada_b):
    depth, d, n = ada_w.shape
    rows = cin.shape[0]
    tn = MODS_TILE
    return pl.pallas_call(
        _mods_kernel,
        grid=(depth, n // tn),
        in_specs=[
            pl.BlockSpec((rows, d), lambda i, j: (0, 0)),
            pl.BlockSpec((None, d, tn), lambda i, j: (i, 0, j)),
            pl.BlockSpec((None, 1, tn), lambda i, j: (i, 0, j)),
        ],
        out_specs=pl.BlockSpec((None, rows, tn), lambda i, j: (i, 0, j)),
        out_shape=jax.ShapeDtypeStruct((depth, rows, n), F32),
        compiler_params=_params(2),
        name="adaln_mods",
    )(cin, ada_w, ada_b.reshape(depth, 1, n))


def _dup_halves(r):
    lane = lax.broadcasted_iota(jnp.int32, r.shape, 1)
    lo = lane < HEAD_DIM
    r64 = pltpu.roll(r, HEAD_DIM, 1)
    return jnp.where(lo, r, r64), jnp.where(lo, r64, r)


def _qkv0_kernel(x_ref, sh_ref, sc_ref, g1_ref, w_ref, gain_ref, gsum_ref, cos_ref, sin_ref, *rest):
    n_side = (len(rest) - 3) // 2
    q_ref, kk_ref, vt_ref = rest[n_side:n_side + 3]
    for src_ref, dst_ref in zip(rest[:n_side], rest[n_side + 3:]):
        dst_ref[...] = src_ref[...].astype(BF16)
    qd = N_HEADS * HEAD_DIM
    qkd = qd + N_KV_HEADS * HEAD_DIM
    parts = _row_parts(x_ref, ROW_PARTS)
    hs = [_rms(x_ref[rows, :], g1_ref[...]) * (1.0 + sc_ref[...]) + sh_ref[...] for rows in parts]
    qkvs = [_dot(h.astype(BF16), w_ref[...]) for h in hs]
    invfs = []
    for qkv in qkvs:
        sq = (qkv[:, :qkd] * qkv[:, :qkd]).astype(BF16)
        ms = jnp.concatenate([_dot(sq[:, c:c + MXU_TILE], gsum_ref[...]) for c in range(0, qkd, MXU_TILE)], axis=1)
        invfs.append(lax.rsqrt(ms + EPS))
    row = lax.broadcasted_iota(jnp.int32, (LANES - HEAD_DIM, ATTN_BLOCK), 0)
    tail = jnp.where(row == 0, 1.0, 0.0).astype(BF16)
    for rows, qkv, invf in zip(parts, qkvs, invfs):
        t = qkv[:, :qkd] * gain_ref[...]
        cos = cos_ref[rows, :]
        sin = sin_ref[rows, :]
        half = HEAD_DIM // 4
        lane = lax.broadcasted_iota(jnp.int32, cos.shape, 1)
        first = (lane % (2 * half)) < half
        for cb in range(qkd // LANES):
            sl = slice(cb * LANES, (cb + 1) * LANES)
            tb = t[:, sl]
            sw = jnp.where(first, pltpu.roll(tb, LANES - half, 1), pltpu.roll(tb, half, 1))
            r = (tb * cos + sw * sin) * invf[:, sl]
            if cb < qd // LANES:
                q_ref[rows, sl] = r.astype(BF16)
            else:
                p = cb - qd // LANES
                a, b = _dup_halves(r)
                kk_ref[rows, (2 * p) * LANES:(2 * p + 1) * LANES] = a.astype(BF16)
                kk_ref[rows, (2 * p + 1) * LANES:(2 * p + 2) * LANES] = b.astype(BF16)
        vt = qkv[:, qkd:].T
        for blk in range(vt.shape[1] // ATTN_BLOCK):
            cols = slice(blk * ATTN_BLOCK, (blk + 1) * ATTN_BLOCK)
            slab = rows.start // ATTN_BLOCK + blk
            for g in range(N_KV_HEADS):
                vt_ref[slab, g * LANES:g * LANES + HEAD_DIM, :] = vt[g * HEAD_DIM:(g + 1) * HEAD_DIM, cols].astype(BF16)
                vt_ref[slab, g * LANES + HEAD_DIM:(g + 1) * LANES, :] = tail


def _row_blocks(rows, steps):
    return next(n for n in range(min(steps, rows // BF16_ROWS), 0, -1)
                if rows % n == 0 and (rows // n) % BF16_ROWS == 0)


def _qkv0(xin, mods, per_batch, g1, w, gain, gsum, cos, sin, tm, side=()):
    b, t, d = xin.shape
    kw = N_KV_HEADS * LANES
    nj = t // tm
    flat = [a.reshape(-1, a.shape[-1]) for a in side]
    side_specs = []
    for a in flat:
        nblk = _row_blocks(a.shape[0], b * nj)
        side_specs.append(pl.BlockSpec((a.shape[0] // nblk, a.shape[1]),
                                       lambda b, j, nblk=nblk: (jnp.minimum(b * nj + j, nblk - 1), 0)))
    outs = pl.pallas_call(
        _qkv0_kernel,
        grid=(b, nj),
        in_specs=[
            _tok_spec(tm, d),
            _mod_spec(0, per_batch),
            _mod_spec(1, per_batch),
            _const2((1, d)),
            _const2(w.shape),
            _const2(gain.shape),
            _const2(gsum.shape),
            pl.BlockSpec((tm, LANES), lambda b, j: (j, 0)),
            pl.BlockSpec((tm, LANES), lambda b, j: (j, 0)),
            *side_specs,
        ],
        out_specs=[_tok_spec(tm, d), _tok_spec(tm, kw),
                   pl.BlockSpec((None, tm // ATTN_BLOCK, kw, ATTN_BLOCK), lambda b, j: (b, j, 0, 0)), *side_specs],
        out_shape=[
            jax.ShapeDtypeStruct((b, t, d), BF16),
            jax.ShapeDtypeStruct((b, t, kw), BF16),
            jax.ShapeDtypeStruct((b, t // ATTN_BLOCK, kw, ATTN_BLOCK), BF16),
            *[jax.ShapeDtypeStruct(a.shape, BF16) for a in flat],
        ],
        compiler_params=_params(2),
        name="attn_qkv_proj",
    )(xin, mods, mods, g1, w, gain, gsum, cos, sin, *flat)
    return list(outs[:3]) + [o.reshape(a.shape) for o, a in zip(outs[3:], side)]


def _band_masks(j, nb, heads):
    shape = (ATTN_BLOCK, heads * ATTN_BLOCK)
    key = lax.broadcasted_iota(jnp.int32, shape, 0)
    qry = lax.broadcasted_iota(jnp.int32, shape, 1) % ATTN_BLOCK
    return (key >= qry) & (j > 0), (key <= qry) & (j < nb - 1)


def _attn_scores(sink_ref, q_ref, key_refs, pr, pairs, masks):
    tq = q_ref.shape[0]
    g = pr // (N_HEADS // N_KV_HEADS // 2)
    gs = slice(g * LANES, (g + 1) * LANES)
    lo = lax.broadcasted_iota(jnp.int32, (tq, LANES), 1) < HEAD_DIM
    zero = jnp.zeros((tq, LANES), BF16)
    stack = []
    for p in range(pr, pr + pairs):
        qb = q_ref[:, p * LANES:(p + 1) * LANES]
        stack += [jnp.where(lo, qb, zero), jnp.where(lo, zero, qb)]
    keys = jnp.concatenate([r[:, gs] for r in key_refs], axis=0) if len(key_refs) > 1 else key_refs[0][:, gs]
    st = _dot_nt(keys, jnp.concatenate(stack, axis=0)).astype(BF16)
    parts = [st[i * LANES:(i + 1) * LANES, :] for i in range(st.shape[0] // LANES)]
    if masks is not None:
        ia = key_refs[0].shape[0] // LANES
        neg = jnp.full(parts[ia].shape, NEG_INF, BF16)
        parts[ia] = jnp.where(masks[0], parts[ia], neg)
        parts[ia + 2] = jnp.where(masks[1], parts[ia + 2], neg)
    sink = jnp.concatenate([jnp.full((1, tq), sink_ref[2 * pr + i] * LOG2E, F32) for i in range(2 * pairs)], axis=1)
    mx = parts[0]
    for part in parts[1:]:
        mx = jnp.maximum(mx, part)
    m = jnp.maximum(jnp.max(mx, axis=0, keepdims=True).astype(F32), sink).astype(BF16)
    return jnp.concatenate([jnp.exp2(part - m) for part in parts], axis=0), jnp.exp2(sink - m.astype(F32))


def _attn_values(probs, val_refs, o_ref, pr, pairs):
    pt, p_sink = probs
    tq = o_ref.shape[0]
    g = pr // (N_HEADS // N_KV_HEADS // 2)
    gs = slice(g * LANES, (g + 1) * LANES)
    vt = jnp.concatenate([r[i, gs, :] for r in val_refs for i in range(r.shape[0])], axis=1)
    ot = _dot(vt, pt)
    den = jnp.sum(ot[HEAD_DIM:HEAD_DIM + SUBLANES, :], axis=0, keepdims=True) + p_sink
    out = ot[:HEAD_DIM, :] / den
    for i in range(pairs):
        pair = jnp.concatenate([out[:, 2 * i * tq:(2 * i + 1) * tq], out[:, (2 * i + 1) * tq:(2 * i + 2) * tq]], axis=0)
        o_ref[:, (pr + i) * LANES:(pr + i + 1) * LANES] = pair.T.astype(BF16)


def _attn_pipeline(sink_ref, tasks, pairs):
    items = [(task, g) for task in tasks for g in range(0, N_HEADS // 2, pairs)]
    probs = {}
    for step in range(len(items) + ATTN_LOOKAHEAD):
        if step < len(items):
            (q_ref, key_refs, _, _, masks), g = items[step]
            probs[step] = _attn_scores(sink_ref, q_ref, key_refs, g, pairs, masks)
        done = step - ATTN_LOOKAHEAD
        if done >= 0:
            (_, _, val_refs, o_ref, _), g = items[done]
            _attn_values(probs.pop(done), val_refs, o_ref, g, pairs)


def _attn_ctx_kernel(sink_ref, q_ref, kc_ref, vc_ref, o_ref, *, pairs):
    _attn_pipeline(sink_ref, [(q_ref, [kc_ref], [vc_ref], o_ref, None)], pairs)


def _attn_band_kernel(sink_ref, q_ref, kc_ref, k_ref, vc_ref, vt_ref, o_ref, *, nb, pairs):
    tasks = []
    for t in range(q_ref.shape[0] // ATTN_BLOCK):
        j = pl.program_id(1) * (q_ref.shape[0] // ATTN_BLOCK) + t
        rows = pl.ds(t * ATTN_BLOCK, ATTN_BLOCK)
        band = [jnp.maximum(j - 1, 0), j, jnp.minimum(j + 1, nb - 1)]
        key_refs = [kc_ref] + [k_ref.at[pl.ds(pl.multiple_of(blk * ATTN_BLOCK, ATTN_BLOCK), ATTN_BLOCK), :]
                               for blk in band]
        val_refs = [vc_ref] + [vt_ref.at[pl.ds(blk, 1)] for blk in band]
        tasks.append((q_ref.at[rows, :], key_refs, val_refs, o_ref.at[rows, :], _band_masks(j, nb, 2 * pairs)))
    _attn_pipeline(sink_ref, tasks, pairs)


def _attn_x(sink, q, kk, vv, kkc, vvc):
    b, s, d = q.shape
    nb = s // ATTN_BLOCK
    rows = ATTN_BLOCK * max(n for n in range(1, ATTN_BLOCKS_PER_STEP + 1) if nb % n == 0)
    per_batch = lambda a: pl.BlockSpec((None,) + a.shape[1:], lambda b, j: (b,) + (0,) * (a.ndim - 1))
    blk = pl.BlockSpec((None, rows, d), lambda b, j: (b, j, 0))
    return pl.pallas_call(
        functools.partial(_attn_band_kernel, nb=nb, pairs=ATTN_PAIRS),
        grid=(b, s // rows),
        in_specs=[pl.BlockSpec(memory_space=pltpu.SMEM), blk,
                  per_batch(kkc), per_batch(kk), per_batch(vvc), per_batch(vv)],
        out_specs=blk,
        out_shape=jax.ShapeDtypeStruct((b, s, d), BF16),
        compiler_params=_params(2),
        name="attn_latent",
    )(sink, q, kkc, kk, vvc, vv)


def _attn_c(sink, qc, kkc, vvc):
    b, l, d = qc.shape
    return pl.pallas_call(
        functools.partial(_attn_ctx_kernel, pairs=ATTN_PAIRS),
        grid=(b,),
        in_specs=[
            pl.BlockSpec(memory_space=pltpu.SMEM),
            pl.BlockSpec((None, l, d), lambda b: (b, 0, 0)),
            pl.BlockSpec((None, l, kkc.shape[2]), lambda b: (b, 0, 0)),
            pl.BlockSpec((None,) + vvc.shape[1:], lambda b: (b, 0, 0, 0)),
        ],
        out_specs=pl.BlockSpec((None, l, d), lambda b: (b, 0, 0)),
        out_shape=jax.ShapeDtypeStruct((b, l, d), BF16),
        compiler_params=_params(1),
        name="attn_context",
    )(sink, qc, kkc, vvc)


def _ffn_tail(x1s, parts, shf_ref, scf_ref, gtf_ref, g2_ref, win_ref, wout_ref, out_ref):
    h2s = [(_rms(x1, g2_ref[...]) * (1.0 + scf_ref[...]) + shf_ref[...]).astype(BF16) for x1 in x1s]
    accs = [None] * len(x1s)
    hidden = {}
    for step in range(len(FF_CHUNKS) + 1):
        if step < len(FF_CHUNKS):
            lo, hi = FF_CHUNKS[step]
            hidden[step] = [(_dot(h2, win_ref[:, lo:hi]), _dot(h2, win_ref[:, D_FF + lo:D_FF + hi])) for h2 in h2s]
        if step > 0:
            lo, hi = FF_CHUNKS[step - 1]
            for i, (gate, up) in enumerate(hidden.pop(step - 1)):
                down = _dot((_silu(gate) * up).astype(BF16), wout_ref[lo:hi, :])
                accs[i] = down if accs[i] is None else accs[i] + down
    for rows, x1, acc in zip(parts, x1s, accs):
        out_ref[rows, :] = x1 + gtf_ref[...] * acc


def _post0_kernel(x_ref, o_ref, gtm_ref, shf_ref, scf_ref, gtf_ref, g2_ref, wo_ref, win_ref, wout_ref, out_ref):
    parts = _row_parts(x_ref, ROW_PARTS)
    mixed = [_dot(o_ref[rows, :], wo_ref[...]) for rows in parts]
    x1s = [x_ref[rows, :] + gtm_ref[...] * a for rows, a in zip(parts, mixed)]
    _ffn_tail(x1s, parts, shf_ref, scf_ref, gtf_ref, g2_ref, win_ref, wout_ref, out_ref)


def _post0(xin, o, mods, per_batch, g2, wo, win, wout, layer, tm):
    b, t, d = xin.shape
    return pl.pallas_call(
        _post0_kernel,
        grid=(b, t // tm),
        in_specs=[
            _tok_spec(tm, d), _tok_spec(tm, d),
            _mod_spec(2, per_batch), _mod_spec(3, per_batch), _mod_spec(4, per_batch), _mod_spec(5, per_batch),
            _const2((1, d)), _const2(wo.shape), _layer_spec(win, layer), _layer_spec(wout, layer),
        ],
        out_specs=_tok_spec(tm, d),
        out_shape=jax.ShapeDtypeStruct((b, t, d), F32),
        compiler_params=_params(2),
        name="attn_out_ffn",
    )(xin, o, mods, mods, mods, mods, g2, wo, win, wout)


def _post1_kernel(x_ref, y_ref, shm_ref, scm_ref, gtm_ref, shf_ref, scf_ref, gtf_ref, g1_ref, g2_ref, gn_ref,
                  wg_ref, wo_ref, win_ref, wout_ref, out_ref):
    parts = _row_parts(x_ref, ROW_PARTS)
    xs = [x_ref[rows, :] for rows in parts]
    h1s = [(_rms(x, g1_ref[...]) * (1.0 + scm_ref[...]) + shm_ref[...]).astype(BF16) for x in xs]
    gates = [_dot(h1, wg_ref[...]) for h1 in h1s]
    gated = []
    for rows, gate in zip(parts, gates):
        normed = []
        for h in range(RET_HEADS):
            o = y_ref[rows, h * RET_V_DIM:(h + 1) * RET_V_DIM].astype(F32)
            d = o - jnp.mean(o, axis=-1, keepdims=True)
            normed.append(d * lax.rsqrt(jnp.mean(d * d, axis=-1, keepdims=True) + EPS))
        gated.append((_silu(gate) * (jnp.concatenate(normed, axis=1) * gn_ref[...])).astype(BF16))
    mixed = [_dot(g, wo_ref[...]) for g in gated]
    x1s = [x + gtm_ref[...] * a for x, a in zip(xs, mixed)]
    _ffn_tail(x1s, parts, shf_ref, scf_ref, gtf_ref, g2_ref, win_ref, wout_ref, out_ref)


def _post1(xin, y, mods, g1, g2, gn, wg, wo, win, wout, layer, tm):
    b, t, d = xin.shape
    return pl.pallas_call(
        _post1_kernel,
        grid=(b, t // tm),
        in_specs=[
            _tok_spec(tm, d), _tok_spec(tm, y.shape[2]),
            *[_mod_spec(k, True) for k in range(6)],
            _const2((1, d)), _const2((1, d)), _const2(gn.shape),
            pl.BlockSpec((d, RET_VWIDTH), lambda b, j: (0, wg.shape[1] // RET_VWIDTH - 1), pipeline_mode=pl.Buffered(1)),
            _const2(wo.shape), _layer_spec(win, layer), _layer_spec(wout, layer),
        ],
        out_specs=_tok_spec(tm, d),
        out_shape=jax.ShapeDtypeStruct((b, t, d), F32),
        compiler_params=_params(2),
        name="ret_out_ffn",
    )(xin, y, mods, mods, mods, mods, mods, mods, g1, g2, gn, wg, wo, win, wout)


def _proj1_kernel(x_ref, sh_ref, sc_ref, g1_ref, w_ref, cos_ref, sin_ref, q_ref, k_ref, v_ref):
    qk_w = 2 * RET_HEADS * RET_QK_DIM
    hb = (_rms(x_ref[...], g1_ref[...]) * (1.0 + sc_ref[...]) + sh_ref[...]).astype(BF16)
    qk = _dot(hb, w_ref[:, :qk_w])
    k_scale = RET_QK_DIM ** -0.5
    for cb in range(qk_w // LANES):
        ts = slice((cb % 2) * LANES, (cb % 2 + 1) * LANES)
        tb = qk[:, cb * LANES:(cb + 1) * LANES]
        r = tb * cos_ref[:, ts] + pltpu.roll(tb, LANES // 2, 1) * sin_ref[:, ts]
        if cb < qk_w // (2 * LANES):
            q_ref[:, cb * LANES:(cb + 1) * LANES] = r.astype(BF16)
        else:
            cc = cb - qk_w // (2 * LANES)
            k_ref[:, cc * LANES:(cc + 1) * LANES] = (r * k_scale).astype(BF16)
    v_ref[...] = _dot(hb, w_ref[:, qk_w:]).astype(BF16)


def _proj1_ctx_kernel(x_ref, sh_ref, sc_ref, g1_ref, w_ref, k_ref, v_ref):
    qw = RET_HEADS * RET_QK_DIM
    hb = (_rms(x_ref[...], g1_ref[...]) * (1.0 + sc_ref[...]) + sh_ref[...]).astype(BF16)
    kv = _dot(hb, w_ref[:, qw:])
    k_ref[...] = (kv[:, :qw] * RET_QK_DIM ** -0.5).astype(BF16)
    v_ref[...] = kv[:, qw:].astype(BF16)


def _proj1(xin, mods, per_batch, g1, w, tables, tm):
    b, t, d = xin.shape
    qw = RET_HEADS * RET_QK_DIM
    specs = [
        _tok_spec(tm, d),
        _mod_spec(0, per_batch), _mod_spec(1, per_batch),
        _const2((1, d)),
        pl.BlockSpec((d, 2 * qw + RET_VWIDTH), lambda b, j: (0, 0), pipeline_mode=pl.Buffered(1)),
    ]
    outs = [(qw, _tok_spec(tm, qw)), (RET_VWIDTH, _tok_spec(tm, RET_VWIDTH))]
    if tables is not None:
        specs += [pl.BlockSpec((tm, 2 * LANES), lambda b, j: (j, 0))] * 2
        outs = [(qw, _tok_spec(tm, qw))] + outs
    return pl.pallas_call(
        _proj1_ctx_kernel if tables is None else _proj1_kernel,
        grid=(b, t // tm),
        in_specs=specs,
        out_specs=[spec for _, spec in outs],
        out_shape=[jax.ShapeDtypeStruct((b, t, width), BF16) for width, _ in outs],
        compiler_params=_params(2),
        name="ret_qkv_proj",
    )(xin, mods, mods, g1, w, *(tables or ()))


def _ret_kernel(dl_ref, q_ref, k_ref, v_ref, kc_ref, vc_ref, y_ref,
                o_scr, sf_scr, sb_scr, comb_scr, qdf_scr, qdb_scr, kdf_scr, kdb_scr, cdf_scr, cdb_scr,
                ctxf_scr, ctxb_scr, *, nc):
    hd = pl.program_id(0)
    c_len = RET_CHUNK
    l_ctx = kc_ref.shape[1]

    @pl.when(pl.program_id(1) == 0)
    def _decay_tables():
        def log_decay(direction, shp):
            logit = jnp.full(shp, dl_ref[direction, hd], F32)
            return jnp.minimum(logit, 0.0) - jnp.log1p(jnp.exp(-jnp.abs(logit)))

        shape = (c_len, c_len)
        n = lax.broadcasted_iota(jnp.int32, shape, 0).astype(F32)
        m = lax.broadcasted_iota(jnp.int32, shape, 1).astype(F32)
        comb_scr[...] = (jnp.where(n >= m, jnp.exp(log_decay(0, shape) * jnp.maximum(n - m, 0.0)), 0.0)
                         + jnp.where(m >= n, jnp.exp(log_decay(1, shape) * jnp.maximum(m - n, 0.0)), 0.0))
        nk = lax.broadcasted_iota(jnp.int32, (c_len, RET_QK_DIM), 0).astype(F32)
        nv = lax.broadcasted_iota(jnp.int32, (c_len, RET_V_DIM), 0).astype(F32)
        qdf_scr[...] = jnp.exp(log_decay(0, nv.shape) * (nv + 1.0))
        kdf_scr[...] = jnp.exp(log_decay(0, nk.shape) * (c_len - 1.0 - nk))
        cdf_scr[...] = jnp.exp(log_decay(0, (1, RET_V_DIM)) * float(c_len))
        qdb_scr[...] = jnp.exp(log_decay(1, nv.shape) * (c_len - nv))
        kdb_scr[...] = jnp.exp(log_decay(1, nk.shape) * nk)
        cdb_scr[...] = jnp.exp(log_decay(1, (1, RET_V_DIM)) * float(c_len))
        tt = lax.broadcasted_iota(jnp.int32, (l_ctx, RET_QK_DIM), 0).astype(F32)
        ctxf_scr[...] = jnp.exp(log_decay(0, tt.shape) * (l_ctx - 1.0 - tt))
        ctxb_scr[...] = jnp.exp(log_decay(1, tt.shape) * tt)

    seqs = range(q_ref.shape[0])

    for n in seqs:
        kc = kc_ref[n].astype(F32)
        sf_scr[n] = _dot_tn((kc * ctxf_scr[...]).astype(BF16), vc_ref[n])
        sb_scr[n] = _dot_tn((kc * ctxb_scr[...]).astype(BF16), vc_ref[n])

    def rows(c):
        return slice(c * c_len, (c + 1) * c_len)

    def masked_scores(n, c):
        return (_dot_nt(q_ref[n, rows(c), :], k_ref[n, rows(c), :]) * comb_scr[...]).astype(BF16)

    def forward(n, c, scores):
        qc, kx, vx = q_ref[n, rows(c), :], k_ref[n, rows(c), :], v_ref[n, rows(c), :]
        st = sf_scr[n]
        out = _dot(scores, vx) + _dot(qc, st.astype(BF16)) * qdf_scr[...]
        sf_scr[n] = st * cdf_scr[...] + _dot_tn((kx.astype(F32) * kdf_scr[...]).astype(BF16), vx)
        return out

    def backward(n, c):
        qc, kx, vx = q_ref[n, rows(c), :], k_ref[n, rows(c), :], v_ref[n, rows(c), :]
        st = sb_scr[n]
        out = _dot(qc, st.astype(BF16)) * qdb_scr[...]
        sb_scr[n] = st * cdb_scr[...] + _dot_tn((kx.astype(F32) * kdb_scr[...]).astype(BF16), vx)
        return out

    scores = {}
    for step in range(nc + RET_LOOKAHEAD):
        if step < nc:
            for n in seqs:
                scores[n, step] = masked_scores(n, step)
        i = step - RET_LOOKAHEAD
        if i < 0:
            continue
        for n in seqs:
            for c, out in ((i, forward(n, i, scores.pop((n, i)))), (nc - 1 - i, backward(n, nc - 1 - i))):
                if i < nc // 2:
                    o_scr[n, rows(c), :] = out
                else:
                    y_ref[n, rows(c), :] = (o_scr[n, rows(c), :] + out).astype(BF16)


def _retention(dl, q, k, v, kc, vc):
    b, s, _ = q.shape
    l = kc.shape[1]
    nc = s // RET_CHUNK
    assert nc % 2 == 0
    seqs = RET_SEQS if b % RET_SEQS == 0 else 1
    hspec = lambda t, w: pl.BlockSpec((seqs, t, w), lambda h, b: (b, 0, h))
    return pl.pallas_call(
        functools.partial(_ret_kernel, nc=nc),
        grid=(RET_HEADS, b // seqs),
        in_specs=[
            pl.BlockSpec(memory_space=pltpu.SMEM),
            hspec(s, RET_QK_DIM), hspec(s, RET_QK_DIM), hspec(s, RET_V_DIM),
            hspec(l, RET_QK_DIM), hspec(l, RET_V_DIM),
        ],
        out_specs=hspec(s, RET_V_DIM),
        out_shape=jax.ShapeDtypeStruct((b, s, RET_VWIDTH), BF16),
        scratch_shapes=[
            pltpu.VMEM((seqs, s, RET_V_DIM), F32),
            pltpu.VMEM((seqs, RET_QK_DIM, RET_V_DIM), F32),
            pltpu.VMEM((seqs, RET_QK_DIM, RET_V_DIM), F32),
            pltpu.VMEM((RET_CHUNK, RET_CHUNK), F32),
            pltpu.VMEM((RET_CHUNK, RET_V_DIM), F32),
            pltpu.VMEM((RET_CHUNK, RET_V_DIM), F32),
            pltpu.VMEM((RET_CHUNK, RET_QK_DIM), F32),
            pltpu.VMEM((RET_CHUNK, RET_QK_DIM), F32),
            pltpu.VMEM((1, RET_V_DIM), F32),
            pltpu.VMEM((1, RET_V_DIM), F32),
            pltpu.VMEM((l, RET_QK_DIM), F32),
            pltpu.VMEM((l, RET_QK_DIM), F32),
        ],
        compiler_params=_params(2),
        name="retention",
    )(dl, q, k, v, kc, vc)


def _rope_tables(n, head_dim):
    rows = n // GRID_W
    row = np.broadcast_to(np.arange(rows, dtype=np.float32)[:, None], (rows, GRID_W)).reshape(n)
    col = np.broadcast_to(np.arange(GRID_W, dtype=np.float32)[None, :], (rows, GRID_W)).reshape(n)
    axis_dim = head_dim // 2
    inv = (ROPE_BASE ** (-np.arange(0, axis_dim, 2, dtype=np.float32) / axis_dim)).astype(np.float32)
    ang_r = row[:, None] * inv
    ang_c = col[:, None] * inv
    cos = np.concatenate([np.cos(ang_r)] * 2 + [np.cos(ang_c)] * 2, axis=-1)
    sin = np.concatenate([-np.sin(ang_r), np.sin(ang_r), -np.sin(ang_c), np.sin(ang_c)], axis=-1)
    return cos.astype(np.float32), sin.astype(np.float32)


def kernel(x, c, ctx, c_ctx, ada_w, ada_b, norm1_g, norm2_g, ffn_w_in, ffn_w_out, attn_w_qkv, attn_q_norm,
           attn_k_norm, attn_sink, attn_w_o, ret_w_qkvg, ret_decay_logit, ret_gn_g, ret_w_o):
    b, s, d = x.shape
    l = ctx.shape[1]
    tm_x = min(FFN_TILE, s)
    tm_c = min(FFN_TILE, l)
    tp_x = min(PROJ_TILE, s)
    tp_c = min(PROJ_TILE, l)

    rows = -(-(b + 1) // SUBLANES) * SUBLANES
    cin = jnp.concatenate([c, c_ctx[None, :], jnp.zeros((rows - b - 1, d), F32)], axis=0)
    mods = _mods(cin, ada_w, ada_b)
    mx0 = mods[0, :b].reshape(b, 1, 6 * d)
    mc0 = mods[0, b:b + 1].reshape(1, 1, 6 * d)
    mx1 = mods[1, :b].reshape(b, 1, 6 * d)
    mc1 = mods[1, b:b + 1].reshape(1, 1, 6 * d)

    w_qkv = attn_w_qkv[0].astype(BF16)
    scale = HEAD_DIM ** -0.5 * LOG2E
    gain = jnp.concatenate([jnp.tile(attn_q_norm[0] * scale, N_HEADS), jnp.tile(attn_k_norm[0], N_KV_HEADS)])[None, :]
    lane_head = np.arange(MXU_TILE) // HEAD_DIM
    gsum = jnp.asarray((lane_head[:, None] == lane_head[None, :]) / HEAD_DIM, BF16)
    cos64, sin64 = _rope_tables(s, HEAD_DIM)
    cos_x = np.tile(cos64, (1, LANES // HEAD_DIM))
    sin_x = np.tile(sin64, (1, LANES // HEAD_DIM))
    cos_c = np.ones((l, LANES), np.float32)
    sin_c = np.zeros((l, LANES), np.float32)
    g1 = norm1_g[0][None, :]
    g2 = norm2_g[0][None, :]
    q_x, kk_x, vv_x, w_o, w_in, w_out, w_qkvg, w_ro = _qkv0(
        x, mx0, True, g1, w_qkv, gain, gsum, cos_x, sin_x, tp_x,
        side=(attn_w_o[0], ffn_w_in, ffn_w_out, ret_w_qkvg[0], ret_w_o[0]))
    q_c, kk_c, vv_c = _qkv0(ctx, mc0, False, g1, w_qkv, gain, gsum, cos_c, sin_c, tp_c)
    sink = attn_sink[0].astype(F32)
    o_x = _attn_x(sink, q_x, kk_x, vv_x, kk_c, vv_c)
    o_c = _attn_c(sink, q_c, kk_c, vv_c)
    x1 = _post0(x, o_x, mx0, True, g2, w_o, w_in, w_out, 0, tm_x)
    y_ctx = _post0(ctx, o_c, mc0, False, g2, w_o, w_in, w_out, 0, tm_c)

    cos256, sin256 = _rope_tables(s, RET_QK_DIM)
    g1 = norm1_g[1][None, :]
    g2 = norm2_g[1][None, :]
    q1, k1, v1 = _proj1(x1, mx1, True, g1, w_qkvg, (cos256, sin256), tp_x)
    k1c, v1c = _proj1(y_ctx, mc1, False, g1, w_qkvg, None, tp_c)
    y = _retention(ret_decay_logit[0].astype(F32), q1, k1, v1, k1c, v1c)
    return _post1(x1, y, mx1, g1, g2, ret_gn_g[0][None, :], w_qkvg, w_ro,
                  w_in, w_out, 1, tm_x)
```

```python
import functools

import jax
import jax.numpy as jnp
import numpy as np
from jax import lax
from jax.experimental import pallas as pl
from jax.experimental.pallas import tpu as pltpu

F32 = jnp.float32
BF16 = jnp.bfloat16

D_MODEL = 1024
GRID_W = 64
HEAD_DIM = 64
N_HEADS = D_MODEL // HEAD_DIM
N_KV_HEADS = N_HEADS // 4
WINDOW = 128
ATTN_BLOCK = 128
assert WINDOW == ATTN_BLOCK
RET_HEADS = 4
RET_QK_DIM = 256
RET_V_DIM = 512
RET_VWIDTH = 2 * D_MODEL
RET_CHUNK = 256
ATTN_PAIRS = N_HEADS // N_KV_HEADS // 2
ATTN_BLOCKS_PER_STEP = 4
ATTN_LOOKAHEAD = 1
RET_SEQS = 2
RET_LOOKAHEAD = 2
D_FF = 2816
MXU_TILE = 256
FF_CHUNKS = ((0, 6 * MXU_TILE), (6 * MXU_TILE, D_FF))
ROW_PARTS = 2
ROPE_BASE = 10000.0
EPS = 1e-6
NEG_INF = -1e30
LOG2E = 1.4426950408889634
LANES = 128
SUBLANES = 8
BF16_ROWS = 2 * SUBLANES
MODS_TILE = 6 * MXU_TILE

VMEM_LIMIT = 56 * 1024 * 1024
FFN_TILE = 512
PROJ_TILE = 1024


def _dot(a, b):
    return jnp.dot(a, b, preferred_element_type=F32)


def _dot_nt(a, b):
    return lax.dot_general(a, b, (((1,), (1,)), ((), ())), preferred_element_type=F32)


def _dot_tn(a, b):
    return lax.dot_general(a, b, (((0,), (0,)), ((), ())), preferred_element_type=F32)


def _rms(x, g):
    return x * lax.rsqrt(jnp.mean(x * x, axis=-1, keepdims=True) + EPS) * g


def _silu(x):
    return x * jax.nn.sigmoid(x)


def _params(n_axes):
    return pltpu.CompilerParams(dimension_semantics=("arbitrary",) * n_axes, vmem_limit_bytes=VMEM_LIMIT)


def _const2(shape):
    return pl.BlockSpec(shape, lambda b, j: (0,) * len(shape), pipeline_mode=pl.Buffered(1))


def _layer_spec(stacked, layer):
    return pl.BlockSpec((None,) + stacked.shape[1:], lambda b, j: (layer,) + (0,) * (stacked.ndim - 1),
                        pipeline_mode=pl.Buffered(1))


def _mod_spec(k, per_batch):
    if per_batch:
        return pl.BlockSpec((None, 1, D_MODEL), lambda b, j, k=k: (b, 0, k))
    return pl.BlockSpec((None, 1, D_MODEL), lambda b, j, k=k: (0, 0, k))


def _tok_spec(tm, width):
    return pl.BlockSpec((None, tm, width), lambda b, j: (b, j, 0))


def _row_parts(ref, n):
    tm = ref.shape[0]
    n = n if tm % (n * ATTN_BLOCK) == 0 else 1
    return [slice(i * tm // n, (i + 1) * tm // n) for i in range(n)]


def _mods_kernel(c_ref, w_ref, b_ref, o_ref):
    a = _silu(c_ref[...]).astype(BF16)
    o_ref[...] = _dot(a, w_ref[...].astype(BF16)) + b_ref[...]


def _mods(cin, ada_w, ada_b):
    depth, d, n = ada_w.shape
    rows = cin.shape[0]
    tn = MODS_TILE
    return pl.pallas_call(
        _mods_kernel,
        grid=(depth, n // tn),
        in_specs=[
            pl.BlockSpec((rows, d), lambda i, j: (0, 0)),
            pl.BlockSpec((None, d, tn), lambda i, j: (i, 0, j)),
            pl.BlockSpec((None, 1, tn), lambda i, j: (i, 0, j)),
        ],
        out_specs=pl.BlockSpec((None, rows, tn), lambda i, j: (i, 0, j)),
        out_shape=jax.ShapeDtypeStruct((depth, rows, n), F32),
        compiler_params=_params(2),
        name="adaln_mods",
    )(cin, ada_w, ada_b.reshape(depth, 1, n))


def _dup_halves(r):
    lane = lax.broadcasted_iota(jnp.int32, r.shape, 1)
    lo = lane < HEAD_DIM
    r64 = pltpu.roll(r, HEAD_DIM, 1)
    return jnp.where(lo, r, r64), jnp.where(lo, r64, r)


def _qkv0_kernel(x_ref, sh_ref, sc_ref, g1_ref, w_ref, gain_ref, gsum_ref, cos_ref, sin_ref, *rest):
    n_side = (len(rest) - 3) // 2
    q_ref, kk_ref, vt_ref = rest[n_side:n_side + 3]
    for src_ref, dst_ref in zip(rest[:n_side], rest[n_side + 3:]):
        dst_ref[...] = src_ref[...].astype(BF16)
    qd = N_HEADS * HEAD_DIM
    qkd = qd + N_KV_HEADS * HEAD_DIM
    parts = _row_parts(x_ref, ROW_PARTS)
    hs = [_rms(x_ref[rows, :], g1_ref[...]) * (1.0 + sc_ref[...]) + sh_ref[...] for rows in parts]
    qkvs = [_dot(h.astype(BF16), w_ref[...]) for h in hs]
    invfs = []
    for qkv in qkvs:
        sq = (qkv[:, :qkd] * qkv[:, :qkd]).astype(BF16)
        ms = jnp.concatenate([_dot(sq[:, c:c + MXU_TILE], gsum_ref[...]) for c in range(0, qkd, MXU_TILE)], axis=1)
        invfs.append(lax.rsqrt(ms + EPS))
    row = lax.broadcasted_iota(jnp.int32, (LANES - HEAD_DIM, ATTN_BLOCK), 0)
    tail = jnp.where(row == 0, 1.0, 0.0).astype(BF16)
    for rows, qkv, invf in zip(parts, qkvs, invfs):
        t = qkv[:, :qkd] * gain_ref[...]
        cos = cos_ref[rows, :]
        sin = sin_ref[rows, :]
        half = HEAD_DIM // 4
        lane = lax.broadcasted_iota(jnp.int32, cos.shape, 1)
        first = (lane % (2 * half)) < half
        for cb in range(qkd // LANES):
            sl = slice(cb * LANES, (cb + 1) * LANES)
            tb = t[:, sl]
            sw = jnp.where(first, pltpu.roll(tb, LANES - half, 1), pltpu.roll(tb, half, 1))
            r = (tb * cos + sw * sin) * invf[:, sl]
            if cb < qd // LANES:
                q_ref[rows, sl] = r.astype(BF16)
            else:
                p = cb - qd // LANES
                a, b = _dup_halves(r)
                kk_ref[rows, (2 * p) * LANES:(2 * p + 1) * LANES] = a.astype(BF16)
                kk_ref[rows, (2 * p + 1) * LANES:(2 * p + 2) * LANES] = b.astype(BF16)
        vt = qkv[:, qkd:].T
        for blk in range(vt.shape[1] // ATTN_BLOCK):
            cols = slice(blk * ATTN_BLOCK, (blk + 1) * ATTN_BLOCK)
            slab = rows.start // ATTN_BLOCK + blk
            for g in range(N_KV_HEADS):
                vt_ref[slab, g * LANES:g * LANES + HEAD_DIM, :] = vt[g * HEAD_DIM:(g + 1) * HEAD_DIM, cols].astype(BF16)
                vt_ref[slab, g * LANES + HEAD_DIM:(g + 1) * LANES, :] = tail


def _row_blocks(rows, steps):
    return next(n for n in range(min(steps, rows // BF16_ROWS), 0, -1)
                if rows % n == 0 and (rows // n) % BF16_ROWS == 0)


def _qkv0(xin, mods, per_batch, g1, w, gain, gsum, cos, sin, tm, side=()):
    b, t, d = xin.shape
    kw = N_KV_HEADS * LANES
    nj = t // tm
    flat = [a.reshape(-1, a.shape[-1]) for a in side]
    side_specs = []
    for a in flat:
        nblk = _row_blocks(a.shape[0], b * nj)
        side_specs.append(pl.BlockSpec((a.shape[0] // nblk, a.shape[1]),
                                       lambda b, j, nblk=nblk: (jnp.minimum(b * nj + j, nblk - 1), 0)))
    outs = pl.pallas_call(
        _qkv0_kernel,
        grid=(b, nj),
        in_specs=[
            _tok_spec(tm, d),
            _mod_spec(0, per_batch),
            _mod_spec(1, per_batch),
            _const2((1, d)),
            _const2(w.shape),
            _const2(gain.shape),
            _const2(gsum.shape),
            pl.BlockSpec((tm, LANES), lambda b, j: (j, 0)),
            pl.BlockSpec((tm, LANES), lambda b, j: (j, 0)),
            *side_specs,
        ],
        out_specs=[_tok_spec(tm, d), _tok_spec(tm, kw),
                   pl.BlockSpec((None, tm // ATTN_BLOCK, kw, ATTN_BLOCK), lambda b, j: (b, j, 0, 0)), *side_specs],
        out_shape=[
            jax.ShapeDtypeStruct((b, t, d), BF16),
            jax.ShapeDtypeStruct((b, t, kw), BF16),
            jax.ShapeDtypeStruct((b, t // ATTN_BLOCK, kw, ATTN_BLOCK), BF16),
            *[jax.ShapeDtypeStruct(a.shape, BF16) for a in flat],
        ],
        compiler_params=_params(2),
        name="attn_qkv_proj",
    )(xin, mods, mods, g1, w, gain, gsum, cos, sin, *flat)
    return list(outs[:3]) + [o.reshape(a.shape) for o, a in zip(outs[3:], side)]


def _band_masks(j, nb, heads):
    shape = (ATTN_BLOCK, heads * ATTN_BLOCK)
    key = lax.broadcasted_iota(jnp.int32, shape, 0)
    qry = lax.broadcasted_iota(jnp.int32, shape, 1) % ATTN_BLOCK
    return (key >= qry) & (j > 0), (key <= qry) & (j < nb - 1)


def _attn_scores(sink_ref, q_ref, key_refs, pr, pairs, masks):
    tq = q_ref.shape[0]
    g = pr // (N_HEADS // N_KV_HEADS // 2)
    gs = slice(g * LANES, (g + 1) * LANES)
    lo = lax.broadcasted_iota(jnp.int32, (tq, LANES), 1) < HEAD_DIM
    zero = jnp.zeros((tq, LANES), BF16)
    stack = []
    for p in range(pr, pr + pairs):
        qb = q_ref[:, p * LANES:(p + 1) * LANES]
        stack += [jnp.where(lo, qb, zero), jnp.where(lo, zero, qb)]
    keys = jnp.concatenate([r[:, gs] for r in key_refs], axis=0) if len(key_refs) > 1 else key_refs[0][:, gs]
    st = _dot_nt(keys, jnp.concatenate(stack, axis=0)).astype(BF16)
    parts = [st[i * LANES:(i + 1) * LANES, :] for i in range(st.shape[0] // LANES)]
    if masks is not None:
        ia = key_refs[0].shape[0] // LANES
        neg = jnp.full(parts[ia].shape, NEG_INF, BF16)
        parts[ia] = jnp.where(masks[0], parts[ia], neg)
        parts[ia + 2] = jnp.where(masks[1], parts[ia + 2], neg)
    sink = jnp.concatenate([jnp.full((1, tq), sink_ref[2 * pr + i] * LOG2E, F32) for i in range(2 * pairs)], axis=1)
    mx = parts[0]
    for part in parts[1:]:
        mx = jnp.maximum(mx, part)
    m = jnp.maximum(jnp.max(mx, axis=0, keepdims=True).astype(F32), sink).astype(BF16)
    return jnp.concatenate([jnp.exp2(part - m) for part in parts], axis=0), jnp.exp2(sink - m.astype(F32))


def _attn_values(probs, val_refs, o_ref, pr, pairs):
    pt, p_sink = probs
    tq = o_ref.shape[1]
    g = pr // (N_HEADS // N_KV_HEADS // 2)
    gs = slice(g * LANES, (g + 1) * LANES)
    vt = jnp.concatenate([r[i, gs, :] for r in val_refs for i in range(r.shape[0])], axis=1)
    ot = _dot(vt, pt)
    den = jnp.sum(ot[HEAD_DIM:HEAD_DIM + SUBLANES, :], axis=0, keepdims=True) + p_sink
    out = ot[:HEAD_DIM, :] / den
    for h in range(2 * pairs):
        head = 2 * pr + h
        o_ref[head * HEAD_DIM:(head + 1) * HEAD_DIM, :] = out[:, h * tq:(h + 1) * tq].astype(BF16)


def _attn_pipeline(sink_ref, tasks, pairs):
    items = [(task, g) for task in tasks for g in range(0, N_HEADS // 2, pairs)]
    probs = {}
    for step in range(len(items) + ATTN_LOOKAHEAD):
        if step < len(items):
            (q_ref, key_refs, _, _, masks), g = items[step]
            probs[step] = _attn_scores(sink_ref, q_ref, key_refs, g, pairs, masks)
        done = step - ATTN_LOOKAHEAD
        if done >= 0:
            (_, _, val_refs, o_ref, _), g = items[done]
            _attn_values(probs.pop(done), val_refs, o_ref, g, pairs)


def _attn_ctx_kernel(sink_ref, q_ref, kc_ref, vc_ref, o_ref, *, pairs):
    _attn_pipeline(sink_ref, [(q_ref, [kc_ref], [vc_ref], o_ref, None)], pairs)


def _attn_band_kernel(sink_ref, q_ref, kc_ref, k_ref, vc_ref, vt_ref, o_ref, *, nb, pairs):
    tasks = []
    for t in range(q_ref.shape[0] // ATTN_BLOCK):
        j = pl.program_id(1) * (q_ref.shape[0] // ATTN_BLOCK) + t
        rows = pl.ds(t * ATTN_BLOCK, ATTN_BLOCK)
        band = [jnp.maximum(j - 1, 0), j, jnp.minimum(j + 1, nb - 1)]
        key_refs = [kc_ref] + [k_ref.at[pl.ds(pl.multiple_of(blk * ATTN_BLOCK, ATTN_BLOCK), ATTN_BLOCK), :]
                               for blk in band]
        val_refs = [vc_ref] + [vt_ref.at[pl.ds(blk, 1)] for blk in band]
        tasks.append((q_ref.at[rows, :], key_refs, val_refs, o_ref.at[:, rows], _band_masks(j, nb, 2 * pairs)))
    _attn_pipeline(sink_ref, tasks, pairs)


def _attn_x(sink, q, kk, vv, kkc, vvc):
    b, s, d = q.shape
    nb = s // ATTN_BLOCK
    rows = ATTN_BLOCK * max(n for n in range(1, ATTN_BLOCKS_PER_STEP + 1) if nb % n == 0)
    per_batch = lambda a: pl.BlockSpec((None,) + a.shape[1:], lambda b, j: (b,) + (0,) * (a.ndim - 1))
    blk = pl.BlockSpec((None, rows, d), lambda b, j: (b, j, 0))
    return pl.pallas_call(
        functools.partial(_attn_band_kernel, nb=nb, pairs=ATTN_PAIRS),
        grid=(b, s // rows),
        in_specs=[pl.BlockSpec(memory_space=pltpu.SMEM), blk,
                  per_batch(kkc), per_batch(kk), per_batch(vvc), per_batch(vv)],
        out_specs=pl.BlockSpec((None, d, rows), lambda b, j: (b, 0, j)),
        out_shape=jax.ShapeDtypeStruct((b, d, s), BF16),
        compiler_params=_params(2),
        name="attn_latent",
    )(sink, q, kkc, kk, vvc, vv)


def _attn_c(sink, qc, kkc, vvc):
    b, l, d = qc.shape
    return pl.pallas_call(
        functools.partial(_attn_ctx_kernel, pairs=ATTN_PAIRS),
        grid=(b,),
        in_specs=[
            pl.BlockSpec(memory_space=pltpu.SMEM),
            pl.BlockSpec((None, l, d), lambda b: (b, 0, 0)),
            pl.BlockSpec((None, l, kkc.shape[2]), lambda b: (b, 0, 0)),
            pl.BlockSpec((None,) + vvc.shape[1:], lambda b: (b, 0, 0, 0)),
        ],
        out_specs=pl.BlockSpec((None, d, l), lambda b: (b, 0, 0)),
        out_shape=jax.ShapeDtypeStruct((b, d, l), BF16),
        compiler_params=_params(1),
        name="attn_context",
    )(sink, qc, kkc, vvc)


def _ffn_tail(x1s, parts, shf_ref, scf_ref, gtf_ref, g2_ref, win_ref, wout_ref, out_ref):
    h2s = [(_rms(x1, g2_ref[...]) * (1.0 + scf_ref[...]) + shf_ref[...]).astype(BF16) for x1 in x1s]
    accs = [None] * len(x1s)
    hidden = {}
    for step in range(len(FF_CHUNKS) + 1):
        if step < len(FF_CHUNKS):
            lo, hi = FF_CHUNKS[step]
            hidden[step] = [(_dot(h2, win_ref[:, lo:hi]), _dot(h2, win_ref[:, D_FF + lo:D_FF + hi])) for h2 in h2s]
        if step > 0:
            lo, hi = FF_CHUNKS[step - 1]
            for i, (gate, up) in enumerate(hidden.pop(step - 1)):
                down = _dot((_silu(gate) * up).astype(BF16), wout_ref[lo:hi, :])
                accs[i] = down if accs[i] is None else accs[i] + down
    for rows, x1, acc in zip(parts, x1s, accs):
        out_ref[rows, :] = x1 + gtf_ref[...] * acc


def _post0_kernel(x_ref, o_ref, gtm_ref, shf_ref, scf_ref, gtf_ref, g2_ref, wo_ref, win_ref, wout_ref, out_ref):
    parts = _row_parts(x_ref, ROW_PARTS)
    mixed = [_dot_tn(o_ref[:, rows], wo_ref[...]) for rows in parts]
    x1s = [x_ref[rows, :] + gtm_ref[...] * a for rows, a in zip(parts, mixed)]
    _ffn_tail(x1s, parts, shf_ref, scf_ref, gtf_ref, g2_ref, win_ref, wout_ref, out_ref)


def _post0(xin, o, mods, per_batch, g2, wo, win, wout, layer, tm):
    b, t, d = xin.shape
    return pl.pallas_call(
        _post0_kernel,
        grid=(b, t // tm),
        in_specs=[
            _tok_spec(tm, d), pl.BlockSpec((None, d, tm), lambda b, j: (b, 0, j)),
            _mod_spec(2, per_batch), _mod_spec(3, per_batch), _mod_spec(4, per_batch), _mod_spec(5, per_batch),
            _const2((1, d)), _const2(wo.shape), _layer_spec(win, layer), _layer_spec(wout, layer),
        ],
        out_specs=_tok_spec(tm, d),
        out_shape=jax.ShapeDtypeStruct((b, t, d), F32),
        compiler_params=_params(2),
        name="attn_out_ffn",
    )(xin, o, mods, mods, mods, mods, g2, wo, win, wout)


def _post1_kernel(x_ref, y_ref, shm_ref, scm_ref, gtm_ref, shf_ref, scf_ref, gtf_ref, g1_ref, g2_ref, gn_ref,
                  wg_ref, wo_ref, win_ref, wout_ref, out_ref):
    parts = _row_parts(x_ref, ROW_PARTS)
    xs = [x_ref[rows, :] for rows in parts]
    h1s = [(_rms(x, g1_ref[...]) * (1.0 + scm_ref[...]) + shm_ref[...]).astype(BF16) for x in xs]
    gates = [_dot(h1, wg_ref[...]) for h1 in h1s]
    gated = []
    for rows, gate in zip(parts, gates):
        normed = []
        for h in range(RET_HEADS):
            o = y_ref[rows, h * RET_V_DIM:(h + 1) * RET_V_DIM].astype(F32)
            d = o - jnp.mean(o, axis=-1, keepdims=True)
            normed.append(d * lax.rsqrt(jnp.mean(d * d, axis=-1, keepdims=True) + EPS))
        gated.append((_silu(gate) * (jnp.concatenate(normed, axis=1) * gn_ref[...])).astype(BF16))
    mixed = [_dot(g, wo_ref[...]) for g in gated]
    x1s = [x + gtm_ref[...] * a for x, a in zip(xs, mixed)]
    _ffn_tail(x1s, parts, shf_ref, scf_ref, gtf_ref, g2_ref, win_ref, wout_ref, out_ref)


def _post1(xin, y, mods, g1, g2, gn, wg, wo, win, wout, layer, tm):
    b, t, d = xin.shape
    return pl.pallas_call(
        _post1_kernel,
        grid=(b, t // tm),
        in_specs=[
            _tok_spec(tm, d), _tok_spec(tm, y.shape[2]),
            *[_mod_spec(k, True) for k in range(6)],
            _const2((1, d)), _const2((1, d)), _const2(gn.shape),
            pl.BlockSpec((d, RET_VWIDTH), lambda b, j: (0, wg.shape[1] // RET_VWIDTH - 1), pipeline_mode=pl.Buffered(1)),
            _const2(wo.shape), _layer_spec(win, layer), _layer_spec(wout, layer),
        ],
        out_specs=_tok_spec(tm, d),
        out_shape=jax.ShapeDtypeStruct((b, t, d), F32),
        compiler_params=_params(2),
        name="ret_out_ffn",
    )(xin, y, mods, mods, mods, mods, mods, mods, g1, g2, gn, wg, wo, win, wout)


def _proj1_kernel(x_ref, sh_ref, sc_ref, g1_ref, w_ref, cos_ref, sin_ref, q_ref, k_ref, v_ref):
    qk_w = 2 * RET_HEADS * RET_QK_DIM
    hb = (_rms(x_ref[...], g1_ref[...]) * (1.0 + sc_ref[...]) + sh_ref[...]).astype(BF16)
    qk = _dot(hb, w_ref[:, :qk_w])
    k_scale = RET_QK_DIM ** -0.5
    for cb in range(qk_w // LANES):
        ts = slice((cb % 2) * LANES, (cb % 2 + 1) * LANES)
        tb = qk[:, cb * LANES:(cb + 1) * LANES]
        r = tb * cos_ref[:, ts] + pltpu.roll(tb, LANES // 2, 1) * sin_ref[:, ts]
        if cb < qk_w // (2 * LANES):
            q_ref[:, cb * LANES:(cb + 1) * LANES] = r.astype(BF16)
        else:
            cc = cb - qk_w // (2 * LANES)
            k_ref[:, cc * LANES:(cc + 1) * LANES] = (r * k_scale).astype(BF16)
    v_ref[...] = _dot(hb, w_ref[:, qk_w:]).astype(BF16)


def _proj1_ctx_kernel(x_ref, sh_ref, sc_ref, g1_ref, w_ref, k_ref, v_ref):
    qw = RET_HEADS * RET_QK_DIM
    hb = (_rms(x_ref[...], g1_ref[...]) * (1.0 + sc_ref[...]) + sh_ref[...]).astype(BF16)
    kv = _dot(hb, w_ref[:, qw:])
    k_ref[...] = (kv[:, :qw] * RET_QK_DIM ** -0.5).astype(BF16)
    v_ref[...] = kv[:, qw:].astype(BF16)


def _proj1(xin, mods, per_batch, g1, w, tables, tm):
    b, t, d = xin.shape
    qw = RET_HEADS * RET_QK_DIM
    specs = [
        _tok_spec(tm, d),
        _mod_spec(0, per_batch), _mod_spec(1, per_batch),
        _const2((1, d)),
        pl.BlockSpec((d, 2 * qw + RET_VWIDTH), lambda b, j: (0, 0), pipeline_mode=pl.Buffered(1)),
    ]
    outs = [(qw, _tok_spec(tm, qw)), (RET_VWIDTH, _tok_spec(tm, RET_VWIDTH))]
    if tables is not None:
        specs += [pl.BlockSpec((tm, 2 * LANES), lambda b, j: (j, 0))] * 2
        outs = [(qw, _tok_spec(tm, qw))] + outs
    return pl.pallas_call(
        _proj1_ctx_kernel if tables is None else _proj1_kernel,
        grid=(b, t // tm),
        in_specs=specs,
        out_specs=[spec for _, spec in outs],
        out_shape=[jax.ShapeDtypeStruct((b, t, width), BF16) for width, _ in outs],
        compiler_params=_params(2),
        name="ret_qkv_proj",
    )(xin, mods, mods, g1, w, *(tables or ()))


def _ret_kernel(dl_ref, q_ref, k_ref, v_ref, kc_ref, vc_ref, y_ref,
                o_scr, sf_scr, sb_scr, comb_scr, qdf_scr, qdb_scr, kdf_scr, kdb_scr, cdf_scr, cdb_scr,
                ctxf_scr, ctxb_scr, *, nc):
    hd = pl.program_id(0)
    c_len = RET_CHUNK
    l_ctx = kc_ref.shape[1]

    @pl.when(pl.program_id(1) == 0)
    def _decay_tables():
        def log_decay(direction, shp):
            logit = jnp.full(shp, dl_ref[direction, hd], F32)
            return jnp.minimum(logit, 0.0) - jnp.log1p(jnp.exp(-jnp.abs(logit)))

        shape = (c_len, c_len)
        n = lax.broadcasted_iota(jnp.int32, shape, 0).astype(F32)
        m = lax.broadcasted_iota(jnp.int32, shape, 1).astype(F32)
        comb_scr[...] = (jnp.where(n >= m, jnp.exp(log_decay(0, shape) * jnp.maximum(n - m, 0.0)), 0.0)
                         + jnp.where(m >= n, jnp.exp(log_decay(1, shape) * jnp.maximum(m - n, 0.0)), 0.0))
        nk = lax.broadcasted_iota(jnp.int32, (c_len, RET_QK_DIM), 0).astype(F32)
        nv = lax.broadcasted_iota(jnp.int32, (c_len, RET_V_DIM), 0).astype(F32)
        qdf_scr[...] = jnp.exp(log_decay(0, nv.shape) * (nv + 1.0))
        kdf_scr[...] = jnp.exp(log_decay(0, nk.shape) * (c_len - 1.0 - nk))
        cdf_scr[...] = jnp.exp(log_decay(0, (1, RET_V_DIM)) * float(c_len))
        qdb_scr[...] = jnp.exp(log_decay(1, nv.shape) * (c_len - nv))
        kdb_scr[...] = jnp.exp(log_decay(1, nk.shape) * nk)
        cdb_scr[...] = jnp.exp(log_decay(1, (1, RET_V_DIM)) * float(c_len))
        tt = lax.broadcasted_iota(jnp.int32, (l_ctx, RET_QK_DIM), 0).astype(F32)
        ctxf_scr[...] = jnp.exp(log_decay(0, tt.shape) * (l_ctx - 1.0 - tt))
        ctxb_scr[...] = jnp.exp(log_decay(1, tt.shape) * tt)

    seqs = range(q_ref.shape[0])

    for n in seqs:
        kc = kc_ref[n].astype(F32)
        sf_scr[n] = _dot_tn((kc * ctxf_scr[...]).astype(BF16), vc_ref[n])
        sb_scr[n] = _dot_tn((kc * ctxb_scr[...]).astype(BF16), vc_ref[n])

    def rows(c):
        return slice(c * c_len, (c + 1) * c_len)

    def masked_scores(n, c):
        return (_dot_nt(q_ref[n, rows(c), :], k_ref[n, rows(c), :]) * comb_scr[...]).astype(BF16)

    def forward(n, c, scores):
        qc, kx, vx = q_ref[n, rows(c), :], k_ref[n, rows(c), :], v_ref[n, rows(c), :]
        st = sf_scr[n]
        out = _dot(scores, vx) + _dot(qc, st.astype(BF16)) * qdf_scr[...]
        sf_scr[n] = st * cdf_scr[...] + _dot_tn((kx.astype(F32) * kdf_scr[...]).astype(BF16), vx)
        return out

    def backward(n, c):
        qc, kx, vx = q_ref[n, rows(c), :], k_ref[n, rows(c), :], v_ref[n, rows(c), :]
        st = sb_scr[n]
        out = _dot(qc, st.astype(BF16)) * qdb_scr[...]
        sb_scr[n] = st * cdb_scr[...] + _dot_tn((kx.astype(F32) * kdb_scr[...]).astype(BF16), vx)
        return out

    scores = {}
    for step in range(nc + RET_LOOKAHEAD):
        if step < nc:
            for n in seqs:
                scores[n, step] = masked_scores(n, step)
        i = step - RET_LOOKAHEAD
        if i < 0:
            continue
        for n in seqs:
            for c, out in ((i, forward(n, i, scores.pop((n, i)))), (nc - 1 - i, backward(n, nc - 1 - i))):
                if i < nc // 2:
                    o_scr[n, rows(c), :] = out
                else:
                    y_ref[n, rows(c), :] = (o_scr[n, rows(c), :] + out).astype(BF16)


def _retention(dl, q, k, v, kc, vc):
    b, s, _ = q.shape
    l = kc.shape[1]
    nc = s // RET_CHUNK
    assert nc % 2 == 0
    seqs = RET_SEQS if b % RET_SEQS == 0 else 1
    hspec = lambda t, w: pl.BlockSpec((seqs, t, w), lambda h, b: (b, 0, h))
    return pl.pallas_call(
        functools.partial(_ret_kernel, nc=nc),
        grid=(RET_HEADS, b // seqs),
        in_specs=[
            pl.BlockSpec(memory_space=pltpu.SMEM),
            hspec(s, RET_QK_DIM), hspec(s, RET_QK_DIM), hspec(s, RET_V_DIM),
            hspec(l, RET_QK_DIM), hspec(l, RET_V_DIM),
        ],
        out_specs=hspec(s, RET_V_DIM),
        out_shape=jax.ShapeDtypeStruct((b, s, RET_VWIDTH), BF16),
        scratch_shapes=[
            pltpu.VMEM((seqs, s, RET_V_DIM), F32),
            pltpu.VMEM((seqs, RET_QK_DIM, RET_V_DIM), F32),
            pltpu.VMEM((seqs, RET_QK_DIM, RET_V_DIM), F32),
            pltpu.VMEM((RET_CHUNK, RET_CHUNK), F32),
            pltpu.VMEM((RET_CHUNK, RET_V_DIM), F32),
            pltpu.VMEM((RET_CHUNK, RET_V_DIM), F32),
            pltpu.VMEM((RET_CHUNK, RET_QK_DIM), F32),
            pltpu.VMEM((RET_CHUNK, RET_QK_DIM), F32),
            pltpu.VMEM((1, RET_V_DIM), F32),
            pltpu.VMEM((1, RET_V_DIM), F32),
            pltpu.VMEM((l, RET_QK_DIM), F32),
            pltpu.VMEM((l, RET_QK_DIM), F32),
        ],
        compiler_params=_params(2),
        name="retention",
    )(dl, q, k, v, kc, vc)


def _rope_tables(n, head_dim):
    rows = n // GRID_W
    row = np.broadcast_to(np.arange(rows, dtype=np.float32)[:, None], (rows, GRID_W)).reshape(n)
    col = np.broadcast_to(np.arange(GRID_W, dtype=np.float32)[None, :], (rows, GRID_W)).reshape(n)
    axis_dim = head_dim // 2
    inv = (ROPE_BASE ** (-np.arange(0, axis_dim, 2, dtype=np.float32) / axis_dim)).astype(np.float32)
    ang_r = row[:, None] * inv
    ang_c = col[:, None] * inv
    cos = np.concatenate([np.cos(ang_r)] * 2 + [np.cos(ang_c)] * 2, axis=-1)
    sin = np.concatenate([-np.sin(ang_r), np.sin(ang_r), -np.sin(ang_c), np.sin(ang_c)], axis=-1)
    return cos.astype(np.float32), sin.astype(np.float32)


def kernel(x, c, ctx, c_ctx, ada_w, ada_b, norm1_g, norm2_g, ffn_w_in, ffn_w_out, attn_w_qkv, attn_q_norm,
           attn_k_norm, attn_sink, attn_w_o, ret_w_qkvg, ret_decay_logit, ret_gn_g, ret_w_o):
    b, s, d = x.shape
    l = ctx.shape[1]
    tm_x = min(FFN_TILE, s)
    tm_c = min(FFN_TILE, l)
    tp_x = min(PROJ_TILE, s)
    tp_c = min(PROJ_TILE, l)

    rows = -(-(b + 1) // SUBLANES) * SUBLANES
    cin = jnp.concatenate([c, c_ctx[None, :], jnp.zeros((rows - b - 1, d), F32)], axis=0)
    mods = _mods(cin, ada_w, ada_b)
    mx0 = mods[0, :b].reshape(b, 1, 6 * d)
    mc0 = mods[0, b:b + 1].reshape(1, 1, 6 * d)
    mx1 = mods[1, :b].reshape(b, 1, 6 * d)
    mc1 = mods[1, b:b + 1].reshape(1, 1, 6 * d)

    w_qkv = attn_w_qkv[0].astype(BF16)
    scale = HEAD_DIM ** -0.5 * LOG2E
    gain = jnp.concatenate([jnp.tile(attn_q_norm[0] * scale, N_HEADS), jnp.tile(attn_k_norm[0], N_KV_HEADS)])[None, :]
    lane_head = np.arange(MXU_TILE) // HEAD_DIM
    gsum = jnp.asarray((lane_head[:, None] == lane_head[None, :]) / HEAD_DIM, BF16)
    cos64, sin64 = _rope_tables(s, HEAD_DIM)
    cos_x = np.tile(cos64, (1, LANES // HEAD_DIM))
    sin_x = np.tile(sin64, (1, LANES // HEAD_DIM))
    cos_c = np.ones((l, LANES), np.float32)
    sin_c = np.zeros((l, LANES), np.float32)
    g1 = norm1_g[0][None, :]
    g2 = norm2_g[0][None, :]
    q_x, kk_x, vv_x, w_o, w_in, w_out, w_qkvg, w_ro = _qkv0(
        x, mx0, True, g1, w_qkv, gain, gsum, cos_x, sin_x, tp_x,
        side=(attn_w_o[0], ffn_w_in, ffn_w_out, ret_w_qkvg[0], ret_w_o[0]))
    q_c, kk_c, vv_c = _qkv0(ctx, mc0, False, g1, w_qkv, gain, gsum, cos_c, sin_c, tp_c)
    sink = attn_sink[0].astype(F32)
    o_x = _attn_x(sink, q_x, kk_x, vv_x, kk_c, vv_c)
    o_c = _attn_c(sink, q_c, kk_c, vv_c)
    x1 = _post0(x, o_x, mx0, True, g2, w_o, w_in, w_out, 0, tm_x)
    y_ctx = _post0(ctx, o_c, mc0, False, g2, w_o, w_in, w_out, 0, tm_c)

    cos256, sin256 = _rope_tables(s, RET_QK_DIM)
    g1 = norm1_g[1][None, :]
    g2 = norm2_g[1][None, :]
    q1, k1, v1 = _proj1(x1, mx1, True, g1, w_qkvg, (cos256, sin256), tp_x)
    k1c, v1c = _proj1(y_ctx, mc1, False, g1, w_qkvg, None, tp_c)
    y = _retention(ret_decay_logit[0].astype(F32), q1, k1, v1, k1c, v1c)
    return _post1(x1, y, mx1, g1, g2, ret_gn_g[0][None, :], w_qkvg, w_ro,
                  w_in, w_out, 1, tm_x)
```

```python
import functools

import jax
import jax.numpy as jnp
import numpy as np
from jax import lax
from jax.experimental import pallas as pl
from jax.experimental.pallas import tpu as pltpu

F32 = jnp.float32
BF16 = jnp.bfloat16

D_MODEL = 1024
GRID_W = 64
HEAD_DIM = 64
N_HEADS = D_MODEL // HEAD_DIM
N_KV_HEADS = N_HEADS // 4
WINDOW = 128
ATTN_BLOCK = 128
assert WINDOW == ATTN_BLOCK
RET_HEADS = 4
RET_QK_DIM = 256
RET_V_DIM = 512
RET_VWIDTH = 2 * D_MODEL
RET_CHUNK = 256
ATTN_PAIRS = N_HEADS // N_KV_HEADS // 2
ATTN_BLOCKS_PER_STEP = 4
ATTN_LOOKAHEAD = 1
RET_SEQS = 2
RET_LOOKAHEAD = 2
D_FF = 2816
MXU_TILE = 256
FF_CHUNKS = ((0, 6 * MXU_TILE), (6 * MXU_TILE, D_FF))
ROW_PARTS = 2
FFN_PART_ROWS = 256
ROPE_BASE = 10000.0
EPS = 1e-6
NEG_INF = -1e30
LOG2E = 1.4426950408889634
LANES = 128
SUBLANES = 8
BF16_ROWS = 2 * SUBLANES
MODS_TILE = 6 * MXU_TILE

VMEM_LIMIT = 56 * 1024 * 1024
FFN_TILE = 512
FFN0_TILE = 1024
PROJ_TILE = 1024


def _dot(a, b):
    return jnp.dot(a, b, preferred_element_type=F32)


def _dot_nt(a, b):
    return lax.dot_general(a, b, (((1,), (1,)), ((), ())), preferred_element_type=F32)


def _dot_tn(a, b):
    return lax.dot_general(a, b, (((0,), (0,)), ((), ())), preferred_element_type=F32)


def _rms(x, g):
    return x * lax.rsqrt(jnp.mean(x * x, axis=-1, keepdims=True) + EPS) * g


def _silu(x):
    return x * jax.nn.sigmoid(x)


def _params(n_axes):
    return pltpu.CompilerParams(dimension_semantics=("arbitrary",) * n_axes, vmem_limit_bytes=VMEM_LIMIT)


def _const2(shape):
    return pl.BlockSpec(shape, lambda b, j: (0,) * len(shape), pipeline_mode=pl.Buffered(1))


def _layer_spec(stacked, layer):
    return pl.BlockSpec((None,) + stacked.shape[1:], lambda b, j: (layer,) + (0,) * (stacked.ndim - 1),
                        pipeline_mode=pl.Buffered(1))


def _mod_spec(k, per_batch):
    if per_batch:
        return pl.BlockSpec((None, 1, D_MODEL), lambda b, j, k=k: (b, 0, k))
    return pl.BlockSpec((None, 1, D_MODEL), lambda b, j, k=k: (0, 0, k))


def _tok_spec(tm, width):
    return pl.BlockSpec((None, tm, width), lambda b, j: (b, j, 0))


def _row_parts(ref, n):
    tm = ref.shape[0]
    n = n if tm % (n * ATTN_BLOCK) == 0 else 1
    return [slice(i * tm // n, (i + 1) * tm // n) for i in range(n)]


def _mods_kernel(c_ref, w_ref, b_ref, o_ref):
    a = _silu(c_ref[...]).astype(BF16)
    o_ref[...] = _dot(a, w_ref[...].astype(BF16)) + b_ref[...]


def _mods(cin, ada_w, ada_b):
    depth, d, n = ada_w.shape
    rows = cin.shape[0]
    tn = MODS_TILE
    return pl.pallas_call(
        _mods_kernel,
        grid=(depth, n // tn),
        in_specs=[
            pl.BlockSpec((rows, d), lambda i, j: (0, 0)),
            pl.BlockSpec((None, d, tn), lambda i, j: (i, 0, j)),
            pl.BlockSpec((None, 1, tn), lambda i, j: (i, 0, j)),
        ],
        out_specs=pl.BlockSpec((None, rows, tn), lambda i, j: (i, 0, j)),
        out_shape=jax.ShapeDtypeStruct((depth, rows, n), F32),
        compiler_params=_params(2),
        name="adaln_mods",
    )(cin, ada_w, ada_b.reshape(depth, 1, n))


def _dup_halves(r):
    lane = lax.broadcasted_iota(jnp.int32, r.shape, 1)
    lo = lane < HEAD_DIM
    r64 = pltpu.roll(r, HEAD_DIM, 1)
    return jnp.where(lo, r, r64), jnp.where(lo, r64, r)


def _qkv0_kernel(x_ref, sh_ref, sc_ref, g1_ref, w_ref, gain_ref, gsum_ref, cos_ref, sin_ref, *rest):
    n_side = (len(rest) - 3) // 2
    q_ref, kk_ref, vt_ref = rest[n_side:n_side + 3]
    for src_ref, dst_ref in zip(rest[:n_side], rest[n_side + 3:]):
        dst_ref[...] = src_ref[...].astype(BF16)
    qd = N_HEADS * HEAD_DIM
    qkd = qd + N_KV_HEADS * HEAD_DIM
    parts = _row_parts(x_ref, ROW_PARTS)
    hs = [_rms(x_ref[rows, :], g1_ref[...]) * (1.0 + sc_ref[...]) + sh_ref[...] for rows in parts]
    qkvs = [_dot(h.astype(BF16), w_ref[...]) for h in hs]
    invfs = []
    for qkv in qkvs:
        sq = (qkv[:, :qkd] * qkv[:, :qkd]).astype(BF16)
        ms = jnp.concatenate([_dot(sq[:, c:c + MXU_TILE], gsum_ref[...]) for c in range(0, qkd, MXU_TILE)], axis=1)
        invfs.append(lax.rsqrt(ms + EPS))
    row = lax.broadcasted_iota(jnp.int32, (LANES - HEAD_DIM, ATTN_BLOCK), 0)
    tail = jnp.where(row == 0, 1.0, 0.0).astype(BF16)
    for rows, qkv, invf in zip(parts, qkvs, invfs):
        t = qkv[:, :qkd] * gain_ref[...]
        cos = cos_ref[rows, :]
        sin = sin_ref[rows, :]
        half = HEAD_DIM // 4
        lane = lax.broadcasted_iota(jnp.int32, cos.shape, 1)
        first = (lane % (2 * half)) < half
        for cb in range(qkd // LANES):
            sl = slice(cb * LANES, (cb + 1) * LANES)
            tb = t[:, sl]
            sw = jnp.where(first, pltpu.roll(tb, LANES - half, 1), pltpu.roll(tb, half, 1))
            r = (tb * cos + sw * sin) * invf[:, sl]
            if cb < qd // LANES:
                q_ref[rows, sl] = r.astype(BF16)
            else:
                p = cb - qd // LANES
                a, b = _dup_halves(r)
                kk_ref[rows, (2 * p) * LANES:(2 * p + 1) * LANES] = a.astype(BF16)
                kk_ref[rows, (2 * p + 1) * LANES:(2 * p + 2) * LANES] = b.astype(BF16)
        vt = qkv[:, qkd:].T
        for blk in range(vt.shape[1] // ATTN_BLOCK):
            cols = slice(blk * ATTN_BLOCK, (blk + 1) * ATTN_BLOCK)
            slab = rows.start // ATTN_BLOCK + blk
            for g in range(N_KV_HEADS):
                vt_ref[slab, g * LANES:g * LANES + HEAD_DIM, :] = vt[g * HEAD_DIM:(g + 1) * HEAD_DIM, cols].astype(BF16)
                vt_ref[slab, g * LANES + HEAD_DIM:(g + 1) * LANES, :] = tail


def _row_blocks(rows, steps):
    return next(n for n in range(min(steps, rows // BF16_ROWS), 0, -1)
                if rows % n == 0 and (rows // n) % BF16_ROWS == 0)


def _qkv0(xin, mods, per_batch, g1, w, gain, gsum, cos, sin, tm, side=()):
    b, t, d = xin.shape
    kw = N_KV_HEADS * LANES
    nj = t // tm
    flat = [a.reshape(-1, a.shape[-1]) for a in side]
    side_specs = []
    for a in flat:
        nblk = _row_blocks(a.shape[0], b * nj)
        side_specs.append(pl.BlockSpec((a.shape[0] // nblk, a.shape[1]),
                                       lambda b, j, nblk=nblk: (jnp.minimum(b * nj + j, nblk - 1), 0)))
    outs = pl.pallas_call(
        _qkv0_kernel,
        grid=(b, nj),
        in_specs=[
            _tok_spec(tm, d),
            _mod_spec(0, per_batch),
            _mod_spec(1, per_batch),
            _const2((1, d)),
            _const2(w.shape),
            _const2(gain.shape),
            _const2(gsum.shape),
            pl.BlockSpec((tm, LANES), lambda b, j: (j, 0)),
            pl.BlockSpec((tm, LANES), lambda b, j: (j, 0)),
            *side_specs,
        ],
        out_specs=[_tok_spec(tm, d), _tok_spec(tm, kw),
                   pl.BlockSpec((None, tm // ATTN_BLOCK, kw, ATTN_BLOCK), lambda b, j: (b, j, 0, 0)), *side_specs],
        out_shape=[
            jax.ShapeDtypeStruct((b, t, d), BF16),
            jax.ShapeDtypeStruct((b, t, kw), BF16),
            jax.ShapeDtypeStruct((b, t // ATTN_BLOCK, kw, ATTN_BLOCK), BF16),
            *[jax.ShapeDtypeStruct(a.shape, BF16) for a in flat],
        ],
        compiler_params=_params(2),
        name="attn_qkv_proj",
    )(xin, mods, mods, g1, w, gain, gsum, cos, sin, *flat)
    return list(outs[:3]) + [o.reshape(a.shape) for o, a in zip(outs[3:], side)]


def _band_masks(j, nb, heads):
    shape = (ATTN_BLOCK, heads * ATTN_BLOCK)
    key = lax.broadcasted_iota(jnp.int32, shape, 0)
    qry = lax.broadcasted_iota(jnp.int32, shape, 1) % ATTN_BLOCK
    return (key >= qry) & (j > 0), (key <= qry) & (j < nb - 1)


def _attn_scores(sink_ref, q_ref, key_refs, pr, pairs, masks):
    tq = q_ref.shape[0]
    g = pr // (N_HEADS // N_KV_HEADS // 2)
    gs = slice(g * LANES, (g + 1) * LANES)
    lo = lax.broadcasted_iota(jnp.int32, (tq, LANES), 1) < HEAD_DIM
    zero = jnp.zeros((tq, LANES), BF16)
    stack = []
    for p in range(pr, pr + pairs):
        qb = q_ref[:, p * LANES:(p + 1) * LANES]
        stack += [jnp.where(lo, qb, zero), jnp.where(lo, zero, qb)]
    keys = jnp.concatenate([r[:, gs] for r in key_refs], axis=0) if len(key_refs) > 1 else key_refs[0][:, gs]
    st = _dot_nt(keys, jnp.concatenate(stack, axis=0)).astype(BF16)
    parts = [st[i * LANES:(i + 1) * LANES, :] for i in range(st.shape[0] // LANES)]
    if masks is not None:
        ia = key_refs[0].shape[0] // LANES
        neg = jnp.full(parts[ia].shape, NEG_INF, BF16)
        parts[ia] = jnp.where(masks[0], parts[ia], neg)
        parts[ia + 2] = jnp.where(masks[1], parts[ia + 2], neg)
    sink = jnp.concatenate([jnp.full((1, tq), sink_ref[2 * pr + i] * LOG2E, F32) for i in range(2 * pairs)], axis=1)
    mx = parts[0]
    for part in parts[1:]:
        mx = jnp.maximum(mx, part)
    m = jnp.maximum(jnp.max(mx, axis=0, keepdims=True).astype(F32), sink).astype(BF16)
    return jnp.concatenate([jnp.exp2(part - m) for part in parts], axis=0), jnp.exp2(sink - m.astype(F32))


def _attn_values(probs, val_refs, o_ref, pr, pairs):
    pt, p_sink = probs
    tq = o_ref.shape[0]
    g = pr // (N_HEADS // N_KV_HEADS // 2)
    gs = slice(g * LANES, (g + 1) * LANES)
    vt = jnp.concatenate([r[i, gs, :] for r in val_refs for i in range(r.shape[0])], axis=1)
    ot = _dot(vt, pt)
    den = jnp.sum(ot[HEAD_DIM:HEAD_DIM + SUBLANES, :], axis=0, keepdims=True) + p_sink
    out = ot[:HEAD_DIM, :] / den
    for i in range(pairs):
        pair = jnp.concatenate([out[:, 2 * i * tq:(2 * i + 1) * tq], out[:, (2 * i + 1) * tq:(2 * i + 2) * tq]], axis=0)
        o_ref[:, (pr + i) * LANES:(pr + i + 1) * LANES] = pair.T.astype(BF16)


def _attn_pipeline(sink_ref, tasks, pairs):
    items = [(task, g) for task in tasks for g in range(0, N_HEADS // 2, pairs)]
    probs = {}
    for step in range(len(items) + ATTN_LOOKAHEAD):
        if step < len(items):
            (q_ref, key_refs, _, _, masks), g = items[step]
            probs[step] = _attn_scores(sink_ref, q_ref, key_refs, g, pairs, masks)
        done = step - ATTN_LOOKAHEAD
        if done >= 0:
            (_, _, val_refs, o_ref, _), g = items[done]
            _attn_values(probs.pop(done), val_refs, o_ref, g, pairs)


def _attn_ctx_kernel(sink_ref, q_ref, kc_ref, vc_ref, o_ref, *, pairs):
    _attn_pipeline(sink_ref, [(q_ref, [kc_ref], [vc_ref], o_ref, None)], pairs)


def _attn_band_kernel(sink_ref, q_ref, kc_ref, k_ref, vc_ref, vt_ref, o_ref, *, nb, pairs):
    tasks = []
    for t in range(q_ref.shape[0] // ATTN_BLOCK):
        j = pl.program_id(1) * (q_ref.shape[0] // ATTN_BLOCK) + t
        rows = pl.ds(t * ATTN_BLOCK, ATTN_BLOCK)
        band = [jnp.maximum(j - 1, 0), j, jnp.minimum(j + 1, nb - 1)]
        key_refs = [kc_ref] + [k_ref.at[pl.ds(pl.multiple_of(blk * ATTN_BLOCK, ATTN_BLOCK), ATTN_BLOCK), :]
                               for blk in band]
        val_refs = [vc_ref] + [vt_ref.at[pl.ds(blk, 1)] for blk in band]
        tasks.append((q_ref.at[rows, :], key_refs, val_refs, o_ref.at[rows, :], _band_masks(j, nb, 2 * pairs)))
    _attn_pipeline(sink_ref, tasks, pairs)


def _attn_x(sink, q, kk, vv, kkc, vvc):
    b, s, d = q.shape
    nb = s // ATTN_BLOCK
    rows = ATTN_BLOCK * max(n for n in range(1, ATTN_BLOCKS_PER_STEP + 1) if nb % n == 0)
    per_batch = lambda a: pl.BlockSpec((None,) + a.shape[1:], lambda b, j: (b,) + (0,) * (a.ndim - 1))
    blk = pl.BlockSpec((None, rows, d), lambda b, j: (b, j, 0))
    return pl.pallas_call(
        functools.partial(_attn_band_kernel, nb=nb, pairs=ATTN_PAIRS),
        grid=(b, s // rows),
        in_specs=[pl.BlockSpec(memory_space=pltpu.SMEM), blk,
                  per_batch(kkc), per_batch(kk), per_batch(vvc), per_batch(vv)],
        out_specs=blk,
        out_shape=jax.ShapeDtypeStruct((b, s, d), BF16),
        compiler_params=_params(2),
        name="attn_latent",
    )(sink, q, kkc, kk, vvc, vv)


def _attn_c(sink, qc, kkc, vvc):
    b, l, d = qc.shape
    return pl.pallas_call(
        functools.partial(_attn_ctx_kernel, pairs=ATTN_PAIRS),
        grid=(b,),
        in_specs=[
            pl.BlockSpec(memory_space=pltpu.SMEM),
            pl.BlockSpec((None, l, d), lambda b: (b, 0, 0)),
            pl.BlockSpec((None, l, kkc.shape[2]), lambda b: (b, 0, 0)),
            pl.BlockSpec((None,) + vvc.shape[1:], lambda b: (b, 0, 0, 0)),
        ],
        out_specs=pl.BlockSpec((None, l, d), lambda b: (b, 0, 0)),
        out_shape=jax.ShapeDtypeStruct((b, l, d), BF16),
        compiler_params=_params(1),
        name="attn_context",
    )(sink, qc, kkc, vvc)


def _ffn_tail(x1s, parts, shf_ref, scf_ref, gtf_ref, g2_ref, win_ref, wout_ref, out_ref):
    h2s = [(_rms(x1, g2_ref[...]) * (1.0 + scf_ref[...]) + shf_ref[...]).astype(BF16) for x1 in x1s]
    accs = [None] * len(x1s)
    hidden = {}
    for step in range(len(FF_CHUNKS) + 1):
        if step < len(FF_CHUNKS):
            lo, hi = FF_CHUNKS[step]
            hidden[step] = [(_dot(h2, win_ref[:, lo:hi]), _dot(h2, win_ref[:, D_FF + lo:D_FF + hi])) for h2 in h2s]
        if step > 0:
            lo, hi = FF_CHUNKS[step - 1]
            for i, (gate, up) in enumerate(hidden.pop(step - 1)):
                down = _dot((_silu(gate) * up).astype(BF16), wout_ref[lo:hi, :])
                accs[i] = down if accs[i] is None else accs[i] + down
    for rows, x1, acc in zip(parts, x1s, accs):
        out_ref[rows, :] = x1 + gtf_ref[...] * acc


def _post0_kernel(x_ref, o_ref, gtm_ref, shf_ref, scf_ref, gtf_ref, g2_ref, wo_ref, win_ref, wout_ref, out_ref):
    parts = _row_parts(x_ref, x_ref.shape[0] // FFN_PART_ROWS)
    mixed = [_dot(o_ref[rows, :], wo_ref[...]) for rows in parts]
    x1s = [x_ref[rows, :] + gtm_ref[...] * a for rows, a in zip(parts, mixed)]
    _ffn_tail(x1s, parts, shf_ref, scf_ref, gtf_ref, g2_ref, win_ref, wout_ref, out_ref)


def _post0(xin, o, mods, per_batch, g2, wo, win, wout, layer, tm):
    b, t, d = xin.shape
    return pl.pallas_call(
        _post0_kernel,
        grid=(b, t // tm),
        in_specs=[
            _tok_spec(tm, d), _tok_spec(tm, d),
            _mod_spec(2, per_batch), _mod_spec(3, per_batch), _mod_spec(4, per_batch), _mod_spec(5, per_batch),
            _const2((1, d)), _const2(wo.shape), _layer_spec(win, layer), _layer_spec(wout, layer),
        ],
        out_specs=_tok_spec(tm, d),
        out_shape=jax.ShapeDtypeStruct((b, t, d), F32),
        compiler_params=_params(2),
        name="attn_out_ffn",
    )(xin, o, mods, mods, mods, mods, g2, wo, win, wout)


def _post1_kernel(x_ref, y_ref, shm_ref, scm_ref, gtm_ref, shf_ref, scf_ref, gtf_ref, g1_ref, g2_ref, gn_ref,
                  wg_ref, wo_ref, win_ref, wout_ref, out_ref):
    parts = _row_parts(x_ref, x_ref.shape[0] // FFN_PART_ROWS)
    xs = [x_ref[rows, :] for rows in parts]
    h1s = [(_rms(x, g1_ref[...]) * (1.0 + scm_ref[...]) + shm_ref[...]).astype(BF16) for x in xs]
    gates = [_dot(h1, wg_ref[...]) for h1 in h1s]
    gated = []
    for rows, gate in zip(parts, gates):
        normed = []
        for h in range(RET_HEADS):
            o = y_ref[rows, h * RET_V_DIM:(h + 1) * RET_V_DIM].astype(F32)
            d = o - jnp.mean(o, axis=-1, keepdims=True)
            normed.append(d * lax.rsqrt(jnp.mean(d * d, axis=-1, keepdims=True) + EPS))
        gated.append((_silu(gate) * (jnp.concatenate(normed, axis=1) * gn_ref[...])).astype(BF16))
    mixed = [_dot(g, wo_ref[...]) for g in gated]
    x1s = [x + gtm_ref[...] * a for x, a in zip(xs, mixed)]
    _ffn_tail(x1s, parts, shf_ref, scf_ref, gtf_ref, g2_ref, win_ref, wout_ref, out_ref)


def _post1(xin, y, mods, g1, g2, gn, wg, wo, win, wout, layer, tm):
    b, t, d = xin.shape
    return pl.pallas_call(
        _post1_kernel,
        grid=(b, t // tm),
        in_specs=[
            _tok_spec(tm, d), _tok_spec(tm, y.shape[2]),
            *[_mod_spec(k, True) for k in range(6)],
            _const2((1, d)), _const2((1, d)), _const2(gn.shape),
            pl.BlockSpec((d, RET_VWIDTH), lambda b, j: (0, wg.shape[1] // RET_VWIDTH - 1), pipeline_mode=pl.Buffered(1)),
            _const2(wo.shape), _layer_spec(win, layer), _layer_spec(wout, layer),
        ],
        out_specs=_tok_spec(tm, d),
        out_shape=jax.ShapeDtypeStruct((b, t, d), F32),
        compiler_params=_params(2),
        name="ret_out_ffn",
    )(xin, y, mods, mods, mods, mods, mods, mods, g1, g2, gn, wg, wo, win, wout)


def _proj1_kernel(x_ref, sh_ref, sc_ref, g1_ref, w_ref, cos_ref, sin_ref, q_ref, k_ref, v_ref):
    qk_w = 2 * RET_HEADS * RET_QK_DIM
    hb = (_rms(x_ref[...], g1_ref[...]) * (1.0 + sc_ref[...]) + sh_ref[...]).astype(BF16)
    qk = _dot(hb, w_ref[:, :qk_w])
    k_scale = RET_QK_DIM ** -0.5
    for cb in range(qk_w // LANES):
        ts = slice((cb % 2) * LANES, (cb % 2 + 1) * LANES)
        tb = qk[:, cb * LANES:(cb + 1) * LANES]
        r = tb * cos_ref[:, ts] + pltpu.roll(tb, LANES // 2, 1) * sin_ref[:, ts]
        if cb < qk_w // (2 * LANES):
            q_ref[:, cb * LANES:(cb + 1) * LANES] = r.astype(BF16)
        else:
            cc = cb - qk_w // (2 * LANES)
            k_ref[:, cc * LANES:(cc + 1) * LANES] = (r * k_scale).astype(BF16)
    v_ref[...] = _dot(hb, w_ref[:, qk_w:]).astype(BF16)


def _proj1_ctx_kernel(x_ref, sh_ref, sc_ref, g1_ref, w_ref, k_ref, v_ref):
    qw = RET_HEADS * RET_QK_DIM
    hb = (_rms(x_ref[...], g1_ref[...]) * (1.0 + sc_ref[...]) + sh_ref[...]).astype(BF16)
    kv = _dot(hb, w_ref[:, qw:])
    k_ref[...] = (kv[:, :qw] * RET_QK_DIM ** -0.5).astype(BF16)
    v_ref[...] = kv[:, qw:].astype(BF16)


def _proj1(xin, mods, per_batch, g1, w, tables, tm):
    b, t, d = xin.shape
    qw = RET_HEADS * RET_QK_DIM
    specs = [
        _tok_spec(tm, d),
        _mod_spec(0, per_batch), _mod_spec(1, per_batch),
        _const2((1, d)),
        pl.BlockSpec((d, 2 * qw + RET_VWIDTH), lambda b, j: (0, 0), pipeline_mode=pl.Buffered(1)),
    ]
    outs = [(qw, _tok_spec(tm, qw)), (RET_VWIDTH, _tok_spec(tm, RET_VWIDTH))]
    if tables is not None:
        specs += [pl.BlockSpec((tm, 2 * LANES), lambda b, j: (j, 0))] * 2
        outs = [(qw, _tok_spec(tm, qw))] + outs
    return pl.pallas_call(
        _proj1_ctx_kernel if tables is None else _proj1_kernel,
        grid=(b, t // tm),
        in_specs=specs,
        out_specs=[spec for _, spec in outs],
        out_shape=[jax.ShapeDtypeStruct((b, t, width), BF16) for width, _ in outs],
        compiler_params=_params(2),
        name="ret_qkv_proj",
    )(xin, mods, mods, g1, w, *(tables or ()))


def _ret_kernel(dl_ref, q_ref, k_ref, v_ref, kc_ref, vc_ref, y_ref,
                o_scr, sf_scr, sb_scr, comb_scr, qdf_scr, qdb_scr, kdf_scr, kdb_scr, cdf_scr, cdb_scr,
                ctxf_scr, ctxb_scr, *, nc):
    hd = pl.program_id(0)
    c_len = RET_CHUNK
    l_ctx = kc_ref.shape[1]

    @pl.when(pl.program_id(1) == 0)
    def _decay_tables():
        def log_decay(direction, shp):
            logit = jnp.full(shp, dl_ref[direction, hd], F32)
            return jnp.minimum(logit, 0.0) - jnp.log1p(jnp.exp(-jnp.abs(logit)))

        shape = (c_len, c_len)
        n = lax.broadcasted_iota(jnp.int32, shape, 0).astype(F32)
        m = lax.broadcasted_iota(jnp.int32, shape, 1).astype(F32)
        comb_scr[...] = (jnp.where(n >= m, jnp.exp(log_decay(0, shape) * jnp.maximum(n - m, 0.0)), 0.0)
                         + jnp.where(m >= n, jnp.exp(log_decay(1, shape) * jnp.maximum(m - n, 0.0)), 0.0))
        nk = lax.broadcasted_iota(jnp.int32, (c_len, RET_QK_DIM), 0).astype(F32)
        nv = lax.broadcasted_iota(jnp.int32, (c_len, RET_V_DIM), 0).astype(F32)
        qdf_scr[...] = jnp.exp(log_decay(0, nv.shape) * (nv + 1.0))
        kdf_scr[...] = jnp.exp(log_decay(0, nk.shape) * (c_len - 1.0 - nk))
        cdf_scr[...] = jnp.exp(log_decay(0, (1, RET_V_DIM)) * float(c_len))
        qdb_scr[...] = jnp.exp(log_decay(1, nv.shape) * (c_len - nv))
        kdb_scr[...] = jnp.exp(log_decay(1, nk.shape) * nk)
        cdb_scr[...] = jnp.exp(log_decay(1, (1, RET_V_DIM)) * float(c_len))
        tt = lax.broadcasted_iota(jnp.int32, (l_ctx, RET_QK_DIM), 0).astype(F32)
        ctxf_scr[...] = jnp.exp(log_decay(0, tt.shape) * (l_ctx - 1.0 - tt))
        ctxb_scr[...] = jnp.exp(log_decay(1, tt.shape) * tt)

    seqs = range(q_ref.shape[0])

    for n in seqs:
        kc = kc_ref[n].astype(F32)
        sf_scr[n] = _dot_tn((kc * ctxf_scr[...]).astype(BF16), vc_ref[n])
        sb_scr[n] = _dot_tn((kc * ctxb_scr[...]).astype(BF16), vc_ref[n])

    def rows(c):
        return slice(c * c_len, (c + 1) * c_len)

    def masked_scores(n, c):
        return (_dot_nt(q_ref[n, rows(c), :], k_ref[n, rows(c), :]) * comb_scr[...]).astype(BF16)

    def forward(n, c, scores):
        qc, kx, vx = q_ref[n, rows(c), :], k_ref[n, rows(c), :], v_ref[n, rows(c), :]
        st = sf_scr[n]
        out = _dot(scores, vx) + _dot(qc, st.astype(BF16)) * qdf_scr[...]
        sf_scr[n] = st * cdf_scr[...] + _dot_tn((kx.astype(F32) * kdf_scr[...]).astype(BF16), vx)
        return out

    def backward(n, c):
        qc, kx, vx = q_ref[n, rows(c), :], k_ref[n, rows(c), :], v_ref[n, rows(c), :]
        st = sb_scr[n]
        out = _dot(qc, st.astype(BF16)) * qdb_scr[...]
        sb_scr[n] = st * cdb_scr[...] + _dot_tn((kx.astype(F32) * kdb_scr[...]).astype(BF16), vx)
        return out

    scores = {}
    for step in range(nc + RET_LOOKAHEAD):
        if step < nc:
            for n in seqs:
                scores[n, step] = masked_scores(n, step)
        i = step - RET_LOOKAHEAD
        if i < 0:
            continue
        for n in seqs:
            for c, out in ((i, forward(n, i, scores.pop((n, i)))), (nc - 1 - i, backward(n, nc - 1 - i))):
                if i < nc // 2:
                    o_scr[n, rows(c), :] = out
                else:
                    y_ref[n, rows(c), :] = (o_scr[n, rows(c), :] + out).astype(BF16)


def _retention(dl, q, k, v, kc, vc):
    b, s, _ = q.shape
    l = kc.shape[1]
    nc = s // RET_CHUNK
    assert nc % 2 == 0
    seqs = RET_SEQS if b % RET_SEQS == 0 else 1
    hspec = lambda t, w: pl.BlockSpec((seqs, t, w), lambda h, b: (b, 0, h))
    return pl.pallas_call(
        functools.partial(_ret_kernel, nc=nc),
        grid=(RET_HEADS, b // seqs),
        in_specs=[
            pl.BlockSpec(memory_space=pltpu.SMEM),
            hspec(s, RET_QK_DIM), hspec(s, RET_QK_DIM), hspec(s, RET_V_DIM),
            hspec(l, RET_QK_DIM), hspec(l, RET_V_DIM),
        ],
        out_specs=hspec(s, RET_V_DIM),
        out_shape=jax.ShapeDtypeStruct((b, s, RET_VWIDTH), BF16),
        scratch_shapes=[
            pltpu.VMEM((seqs, s, RET_V_DIM), F32),
            pltpu.VMEM((seqs, RET_QK_DIM, RET_V_DIM), F32),
            pltpu.VMEM((seqs, RET_QK_DIM, RET_V_DIM), F32),
            pltpu.VMEM((RET_CHUNK, RET_CHUNK), F32),
            pltpu.VMEM((RET_CHUNK, RET_V_DIM), F32),
            pltpu.VMEM((RET_CHUNK, RET_V_DIM), F32),
            pltpu.VMEM((RET_CHUNK, RET_QK_DIM), F32),
            pltpu.VMEM((RET_CHUNK, RET_QK_DIM), F32),
            pltpu.VMEM((1, RET_V_DIM), F32),
            pltpu.VMEM((1, RET_V_DIM), F32),
            pltpu.VMEM((l, RET_QK_DIM), F32),
            pltpu.VMEM((l, RET_QK_DIM), F32),
        ],
        compiler_params=_params(2),
        name="retention",
    )(dl, q, k, v, kc, vc)


def _rope_tables(n, head_dim):
    rows = n // GRID_W
    row = np.broadcast_to(np.arange(rows, dtype=np.float32)[:, None], (rows, GRID_W)).reshape(n)
    col = np.broadcast_to(np.arange(GRID_W, dtype=np.float32)[None, :], (rows, GRID_W)).reshape(n)
    axis_dim = head_dim // 2
    inv = (ROPE_BASE ** (-np.arange(0, axis_dim, 2, dtype=np.float32) / axis_dim)).astype(np.float32)
    ang_r = row[:, None] * inv
    ang_c = col[:, None] * inv
    cos = np.concatenate([np.cos(ang_r)] * 2 + [np.cos(ang_c)] * 2, axis=-1)
    sin = np.concatenate([-np.sin(ang_r), np.sin(ang_r), -np.sin(ang_c), np.sin(ang_c)], axis=-1)
    return cos.astype(np.float32), sin.astype(np.float32)


def kernel(x, c, ctx, c_ctx, ada_w, ada_b, norm1_g, norm2_g, ffn_w_in, ffn_w_out, attn_w_qkv, attn_q_norm,
           attn_k_norm, attn_sink, attn_w_o, ret_w_qkvg, ret_decay_logit, ret_gn_g, ret_w_o):
    b, s, d = x.shape
    l = ctx.shape[1]
    tm_x = min(FFN_TILE, s)
    tm_c = min(FFN_TILE, l)
    tp_x = min(PROJ_TILE, s)
    tp_c = min(PROJ_TILE, l)

    rows = -(-(b + 1) // SUBLANES) * SUBLANES
    cin = jnp.concatenate([c, c_ctx[None, :], jnp.zeros((rows - b - 1, d), F32)], axis=0)
    mods = _mods(cin, ada_w, ada_b)
    mx0 = mods[0, :b].reshape(b, 1, 6 * d)
    mc0 = mods[0, b:b + 1].reshape(1, 1, 6 * d)
    mx1 = mods[1, :b].reshape(b, 1, 6 * d)
    mc1 = mods[1, b:b + 1].reshape(1, 1, 6 * d)

    w_qkv = attn_w_qkv[0].astype(BF16)
    scale = HEAD_DIM ** -0.5 * LOG2E
    gain = jnp.concatenate([jnp.tile(attn_q_norm[0] * scale, N_HEADS), jnp.tile(attn_k_norm[0], N_KV_HEADS)])[None, :]
    lane_head = np.arange(MXU_TILE) // HEAD_DIM
    gsum = jnp.asarray((lane_head[:, None] == lane_head[None, :]) / HEAD_DIM, BF16)
    cos64, sin64 = _rope_tables(s, HEAD_DIM)
    cos_x = np.tile(cos64, (1, LANES // HEAD_DIM))
    sin_x = np.tile(sin64, (1, LANES // HEAD_DIM))
    cos_c = np.ones((l, LANES), np.float32)
    sin_c = np.zeros((l, LANES), np.float32)
    g1 = norm1_g[0][None, :]
    g2 = norm2_g[0][None, :]
    q_x, kk_x, vv_x, w_o, w_in, w_out, w_qkvg, w_ro = _qkv0(
        x, mx0, True, g1, w_qkv, gain, gsum, cos_x, sin_x, tp_x,
        side=(attn_w_o[0], ffn_w_in, ffn_w_out, ret_w_qkvg[0], ret_w_o[0]))
    q_c, kk_c, vv_c = _qkv0(ctx, mc0, False, g1, w_qkv, gain, gsum, cos_c, sin_c, tp_c)
    sink = attn_sink[0].astype(F32)
    o_x = _attn_x(sink, q_x, kk_x, vv_x, kk_c, vv_c)
    o_c = _attn_c(sink, q_c, kk_c, vv_c)
    x1 = _post0(x, o_x, mx0, True, g2, w_o, w_in, w_out, 0, min(FFN0_TILE, s))
    y_ctx = _post0(ctx, o_c, mc0, False, g2, w_o, w_in, w_out, 0, tm_c)

    cos256, sin256 = _rope_tables(s, RET_QK_DIM)
    g1 = norm1_g[1][None, :]
    g2 = norm2_g[1][None, :]
    q1, k1, v1 = _proj1(x1, mx1, True, g1, w_qkvg, (cos256, sin256), tp_x)
    k1c, v1c = _proj1(y_ctx, mc1, False, g1, w_qkvg, None, tp_c)
    y = _retention(ret_decay_logit[0].astype(F32), q1, k1, v1, k1c, v1c)
    return _post1(x1, y, mx1, g1, g2, ret_gn_g[0][None, :], w_qkvg, w_ro,
                  w_in, w_out, 1, tm_x)
```

```python
import functools

import jax
import jax.numpy as jnp
import numpy as np
from jax import lax
from jax.experimental import pallas as pl
from jax.experimental.pallas import tpu as pltpu

F32 = jnp.float32
BF16 = jnp.bfloat16

D_MODEL = 1024
GRID_W = 64
HEAD_DIM = 64
N_HEADS = D_MODEL // HEAD_DIM
N_KV_HEADS = N_HEADS // 4
WINDOW = 128
ATTN_BLOCK = 128
assert WINDOW == ATTN_BLOCK
RET_HEADS = 4
RET_QK_DIM = 256
RET_V_DIM = 512
RET_VWIDTH = 2 * D_MODEL
RET_CHUNK = 256
ATTN_PAIRS = N_HEADS // N_KV_HEADS // 2
ATTN_BLOCKS_PER_STEP = 4
ATTN_LOOKAHEAD = 1
RET_SEQS = 2
RET_LOOKAHEAD = 2
D_FF = 2816
MXU_TILE = 256
FF_CHUNKS = ((0, 6 * MXU_TILE), (6 * MXU_TILE, D_FF))
ROW_PARTS = 2
ROPE_BASE = 10000.0
EPS = 1e-6
NEG_INF = -1e30
LOG2E = 1.4426950408889634
LANES = 128
SUBLANES = 8
BF16_ROWS = 2 * SUBLANES
MODS_TILE = 6 * MXU_TILE

VMEM_LIMIT = 56 * 1024 * 1024
FFN_TILE = 512
PROJ_TILE = 1024


def _dot(a, b):
    return jnp.dot(a, b, preferred_element_type=F32)


def _dot_nt(a, b):
    return lax.dot_general(a, b, (((1,), (1,)), ((), ())), preferred_element_type=F32)


def _dot_tn(a, b):
    return lax.dot_general(a, b, (((0,), (0,)), ((), ())), preferred_element_type=F32)


def _rms(x, g):
    return x * lax.rsqrt(jnp.mean(x * x, axis=-1, keepdims=True) + EPS) * g


def _silu(x):
    return x * jax.nn.sigmoid(x)


def _params(n_axes):
    return pltpu.CompilerParams(dimension_semantics=("arbitrary",) * n_axes, vmem_limit_bytes=VMEM_LIMIT)


def _const2(shape):
    return pl.BlockSpec(shape, lambda b, j: (0,) * len(shape), pipeline_mode=pl.Buffered(1))


def _layer_spec(stacked, layer):
    return pl.BlockSpec((None,) + stacked.shape[1:], lambda b, j: (layer,) + (0,) * (stacked.ndim - 1),
                        pipeline_mode=pl.Buffered(1))


def _mod_spec(k, per_batch):
    if per_batch:
        return pl.BlockSpec((None, 1, D_MODEL), lambda b, j, k=k: (b, 0, k))
    return pl.BlockSpec((None, 1, D_MODEL), lambda b, j, k=k: (0, 0, k))


def _tok_spec(tm, width):
    return pl.BlockSpec((None, tm, width), lambda b, j: (b, j, 0))


def _row_parts(ref, n):
    tm = ref.shape[0]
    n = n if tm % (n * ATTN_BLOCK) == 0 else 1
    return [slice(i * tm // n, (i + 1) * tm // n) for i in range(n)]


def _mods_kernel(c_ref, w_ref, b_ref, o_ref):
    a = _silu(c_ref[...]).astype(BF16)
    o_ref[...] = _dot(a, w_ref[...].astype(BF16)) + b_ref[...]


def _mods(cin, ada_w, ada_b):
    depth, d, n = ada_w.shape
    rows = cin.shape[0]
    tn = MODS_TILE
    return pl.pallas_call(
        _mods_kernel,
        grid=(depth, n // tn),
        in_specs=[
            pl.BlockSpec((rows, d), lambda i, j: (0, 0)),
            pl.BlockSpec((None, d, tn), lambda i, j: (i, 0, j)),
            pl.BlockSpec((None, 1, tn), lambda i, j: (i, 0, j)),
        ],
        out_specs=pl.BlockSpec((None, rows, tn), lambda i, j: (i, 0, j)),
        out_shape=jax.ShapeDtypeStruct((depth, rows, n), F32),
        compiler_params=_params(2),
        name="adaln_mods",
    )(cin, ada_w, ada_b.reshape(depth, 1, n))


def _dup_halves(r):
    lane = lax.broadcasted_iota(jnp.int32, r.shape, 1)
    lo = lane < HEAD_DIM
    r64 = pltpu.roll(r, HEAD_DIM, 1)
    return jnp.where(lo, r, r64), jnp.where(lo, r64, r)


def _qkv0_kernel(x_ref, sh_ref, sc_ref, g1_ref, w_ref, gain_ref, gsum_ref, cos_ref, sin_ref, *rest):
    n_side = (len(rest) - 3) // 2
    q_ref, kk_ref, vt_ref = rest[n_side:n_side + 3]
    for src_ref, dst_ref in zip(rest[:n_side], rest[n_side + 3:]):
        dst_ref[...] = src_ref[...].astype(BF16)
    qd = N_HEADS * HEAD_DIM
    qkd = qd + N_KV_HEADS * HEAD_DIM
    parts = _row_parts(x_ref, ROW_PARTS)
    hs = [_rms(x_ref[rows, :], g1_ref[...]) * (1.0 + sc_ref[...]) + sh_ref[...] for rows in parts]
    qkvs = [_dot(h.astype(BF16), w_ref[...]) for h in hs]
    invfs = []
    for qkv in qkvs:
        sq = (qkv[:, :qkd] * qkv[:, :qkd]).astype(BF16)
        ms = jnp.concatenate([_dot(sq[:, c:c + MXU_TILE], gsum_ref[...]) for c in range(0, qkd, MXU_TILE)], axis=1)
        invfs.append(lax.rsqrt(ms + EPS))
    row = lax.broadcasted_iota(jnp.int32, (LANES - HEAD_DIM, ATTN_BLOCK), 0)
    tail = jnp.where(row == 0, 1.0, 0.0).astype(BF16)
    for rows, qkv, invf in zip(parts, qkvs, invfs):
        t = qkv[:, :qkd] * gain_ref[...]
        cos = cos_ref[rows, :]
        sin = sin_ref[rows, :]
        half = HEAD_DIM // 4
        lane = lax.broadcasted_iota(jnp.int32, cos.shape, 1)
        first = (lane % (2 * half)) < half
        for cb in range(qkd // LANES):
            sl = slice(cb * LANES, (cb + 1) * LANES)
            tb = t[:, sl]
            sw = jnp.where(first, pltpu.roll(tb, LANES - half, 1), pltpu.roll(tb, half, 1))
            r = (tb * cos + sw * sin) * invf[:, sl]
            if cb < qd // LANES:
                q_ref[rows, sl] = r.astype(BF16)
            else:
                p = cb - qd // LANES
                a, b = _dup_halves(r)
                kk_ref[rows, (2 * p) * LANES:(2 * p + 1) * LANES] = a.astype(BF16)
                kk_ref[rows, (2 * p + 1) * LANES:(2 * p + 2) * LANES] = b.astype(BF16)
        vt = qkv[:, qkd:].T
        for blk in range(vt.shape[1] // ATTN_BLOCK):
            cols = slice(blk * ATTN_BLOCK, (blk + 1) * ATTN_BLOCK)
            slab = rows.start // ATTN_BLOCK + blk
            for g in range(N_KV_HEADS):
                vt_ref[slab, g * LANES:g * LANES + HEAD_DIM, :] = vt[g * HEAD_DIM:(g + 1) * HEAD_DIM, cols].astype(BF16)
                vt_ref[slab, g * LANES + HEAD_DIM:(g + 1) * LANES, :] = tail


def _row_blocks(rows, steps):
    return next(n for n in range(min(steps, rows // BF16_ROWS), 0, -1)
                if rows % n == 0 and (rows // n) % BF16_ROWS == 0)


def _qkv0(xin, mods, per_batch, g1, w, gain, gsum, cos, sin, tm, side=()):
    b, t, d = xin.shape
    kw = N_KV_HEADS * LANES
    nj = t // tm
    flat = [a.reshape(-1, a.shape[-1]) for a in side]
    side_specs = []
    for a in flat:
        nblk = _row_blocks(a.shape[0], b * nj)
        side_specs.append(pl.BlockSpec((a.shape[0] // nblk, a.shape[1]),
                                       lambda b, j, nblk=nblk: (jnp.minimum(b * nj + j, nblk - 1), 0)))
    outs = pl.pallas_call(
        _qkv0_kernel,
        grid=(b, nj),
        in_specs=[
            _tok_spec(tm, d),
            _mod_spec(0, per_batch),
            _mod_spec(1, per_batch),
            _const2((1, d)),
            _const2(w.shape),
            _const2(gain.shape),
            _const2(gsum.shape),
            pl.BlockSpec((tm, LANES), lambda b, j: (j, 0)),
            pl.BlockSpec((tm, LANES), lambda b, j: (j, 0)),
            *side_specs,
        ],
        out_specs=[_tok_spec(tm, d), _tok_spec(tm, kw),
                   pl.BlockSpec((None, tm // ATTN_BLOCK, kw, ATTN_BLOCK), lambda b, j: (b, j, 0, 0)), *side_specs],
        out_shape=[
            jax.ShapeDtypeStruct((b, t, d), BF16),
            jax.ShapeDtypeStruct((b, t, kw), BF16),
            jax.ShapeDtypeStruct((b, t // ATTN_BLOCK, kw, ATTN_BLOCK), BF16),
            *[jax.ShapeDtypeStruct(a.shape, BF16) for a in flat],
        ],
        compiler_params=_params(2),
        name="attn_qkv_proj",
    )(xin, mods, mods, g1, w, gain, gsum, cos, sin, *flat)
    return list(outs[:3]) + [o.reshape(a.shape) for o, a in zip(outs[3:], side)]


def _band_masks(j, nb, heads):
    shape = (ATTN_BLOCK, heads * ATTN_BLOCK)
    key = lax.broadcasted_iota(jnp.int32, shape, 0)
    qry = lax.broadcasted_iota(jnp.int32, shape, 1) % ATTN_BLOCK
    return (key >= qry) & (j > 0), (key <= qry) & (j < nb - 1)


def _attn_scores(sink_ref, q_ref, key_refs, pr, pairs, masks):
    tq = q_ref.shape[0]
    g = pr // (N_HEADS // N_KV_HEADS // 2)
    gs = slice(g * LANES, (g + 1) * LANES)
    lo = lax.broadcasted_iota(jnp.int32, (tq, LANES), 1) < HEAD_DIM
    zero = jnp.zeros((tq, LANES), BF16)
    stack = []
    for p in range(pr, pr + pairs):
        qb = q_ref[:, p * LANES:(p + 1) * LANES]
        stack += [jnp.where(lo, qb, zero), jnp.where(lo, zero, qb)]
    keys = jnp.concatenate([r[:, gs] for r in key_refs], axis=0) if len(key_refs) > 1 else key_refs[0][:, gs]
    st = _dot_nt(keys, jnp.concatenate(stack, axis=0)).astype(BF16)
    parts = [st[i * LANES:(i + 1) * LANES, :] for i in range(st.shape[0] // LANES)]
    if masks is not None:
        ia = key_refs[0].shape[0] // LANES
        neg = jnp.full(parts[ia].shape, NEG_INF, BF16)
        parts[ia] = jnp.where(masks[0], parts[ia], neg)
        parts[ia + 2] = jnp.where(masks[1], parts[ia + 2], neg)
    sink = jnp.concatenate([jnp.full((1, tq), sink_ref[2 * pr + i] * LOG2E, F32) for i in range(2 * pairs)], axis=1)
    mx = parts[0]
    for part in parts[1:]:
        mx = jnp.maximum(mx, part)
    m = jnp.maximum(jnp.max(mx, axis=0, keepdims=True).astype(F32), sink).astype(BF16)
    return jnp.concatenate([jnp.exp2(part - m) for part in parts], axis=0), jnp.exp2(sink - m.astype(F32))


def _attn_values(probs, val_refs, o_ref, pr, pairs):
    pt, p_sink = probs
    tq = o_ref.shape[0]
    g = pr // (N_HEADS // N_KV_HEADS // 2)
    gs = slice(g * LANES, g * LANES + HEAD_DIM + BF16_ROWS)
    vt = jnp.concatenate([r[i, gs, :] for r in val_refs for i in range(r.shape[0])], axis=1)
    ot = _dot(vt, pt)
    den = jnp.sum(ot[HEAD_DIM:HEAD_DIM + SUBLANES, :], axis=0, keepdims=True) + p_sink
    out = ot[:HEAD_DIM, :] / den
    for i in range(pairs):
        pair = jnp.concatenate([out[:, 2 * i * tq:(2 * i + 1) * tq], out[:, (2 * i + 1) * tq:(2 * i + 2) * tq]], axis=0)
        o_ref[:, (pr + i) * LANES:(pr + i + 1) * LANES] = pair.T.astype(BF16)


def _attn_pipeline(sink_ref, tasks, pairs):
    items = [(task, g) for task in tasks for g in range(0, N_HEADS // 2, pairs)]
    probs = {}
    for step in range(len(items) + ATTN_LOOKAHEAD):
        if step < len(items):
            (q_ref, key_refs, _, _, masks), g = items[step]
            probs[step] = _attn_scores(sink_ref, q_ref, key_refs, g, pairs, masks)
        done = step - ATTN_LOOKAHEAD
        if done >= 0:
            (_, _, val_refs, o_ref, _), g = items[done]
            _attn_values(probs.pop(done), val_refs, o_ref, g, pairs)


def _attn_ctx_kernel(sink_ref, q_ref, kc_ref, vc_ref, o_ref, *, pairs):
    _attn_pipeline(sink_ref, [(q_ref, [kc_ref], [vc_ref], o_ref, None)], pairs)


def _attn_band_kernel(sink_ref, q_ref, kc_ref, k_ref, vc_ref, vt_ref, o_ref, *, nb, pairs):
    tasks = []
    for t in range(q_ref.shape[0] // ATTN_BLOCK):
        j = pl.program_id(1) * (q_ref.shape[0] // ATTN_BLOCK) + t
        rows = pl.ds(t * ATTN_BLOCK, ATTN_BLOCK)
        band = [jnp.maximum(j - 1, 0), j, jnp.minimum(j + 1, nb - 1)]
        key_refs = [kc_ref] + [k_ref.at[pl.ds(pl.multiple_of(blk * ATTN_BLOCK, ATTN_BLOCK), ATTN_BLOCK), :]
                               for blk in band]
        val_refs = [vc_ref] + [vt_ref.at[pl.ds(blk, 1)] for blk in band]
        tasks.append((q_ref.at[rows, :], key_refs, val_refs, o_ref.at[rows, :], _band_masks(j, nb, 2 * pairs)))
    _attn_pipeline(sink_ref, tasks, pairs)


def _attn_x(sink, q, kk, vv, kkc, vvc):
    b, s, d = q.shape
    nb = s // ATTN_BLOCK
    rows = ATTN_BLOCK * max(n for n in range(1, ATTN_BLOCKS_PER_STEP + 1) if nb % n == 0)
    per_batch = lambda a: pl.BlockSpec((None,) + a.shape[1:], lambda b, j: (b,) + (0,) * (a.ndim - 1))
    blk = pl.BlockSpec((None, rows, d), lambda b, j: (b, j, 0))
    return pl.pallas_call(
        functools.partial(_attn_band_kernel, nb=nb, pairs=ATTN_PAIRS),
        grid=(b, s // rows),
        in_specs=[pl.BlockSpec(memory_space=pltpu.SMEM), blk,
                  per_batch(kkc), per_batch(kk), per_batch(vvc), per_batch(vv)],
        out_specs=blk,
        out_shape=jax.ShapeDtypeStruct((b, s, d), BF16),
        compiler_params=_params(2),
        name="attn_latent",
    )(sink, q, kkc, kk, vvc, vv)


def _attn_c(sink, qc, kkc, vvc):
    b, l, d = qc.shape
    return pl.pallas_call(
        functools.partial(_attn_ctx_kernel, pairs=ATTN_PAIRS),
        grid=(b,),
        in_specs=[
            pl.BlockSpec(memory_space=pltpu.SMEM),
            pl.BlockSpec((None, l, d), lambda b: (b, 0, 0)),
            pl.BlockSpec((None, l, kkc.shape[2]), lambda b: (b, 0, 0)),
            pl.BlockSpec((None,) + vvc.shape[1:], lambda b: (b, 0, 0, 0)),
        ],
        out_specs=pl.BlockSpec((None, l, d), lambda b: (b, 0, 0)),
        out_shape=jax.ShapeDtypeStruct((b, l, d), BF16),
        compiler_params=_params(1),
        name="attn_context",
    )(sink, qc, kkc, vvc)


def _ffn_tail(x1s, parts, shf_ref, scf_ref, gtf_ref, g2_ref, win_ref, wout_ref, out_ref):
    h2s = [(_rms(x1, g2_ref[...]) * (1.0 + scf_ref[...]) + shf_ref[...]).astype(BF16) for x1 in x1s]
    accs = [None] * len(x1s)
    hidden = {}
    for step in range(len(FF_CHUNKS) + 1):
        if step < len(FF_CHUNKS):
            lo, hi = FF_CHUNKS[step]
            hidden[step] = [(_dot(h2, win_ref[:, lo:hi]), _dot(h2, win_ref[:, D_FF + lo:D_FF + hi])) for h2 in h2s]
        if step > 0:
            lo, hi = FF_CHUNKS[step - 1]
            for i, (gate, up) in enumerate(hidden.pop(step - 1)):
                down = _dot((_silu(gate) * up).astype(BF16), wout_ref[lo:hi, :])
                accs[i] = down if accs[i] is None else accs[i] + down
    for rows, x1, acc in zip(parts, x1s, accs):
        out_ref[rows, :] = x1 + gtf_ref[...] * acc


def _post0_kernel(x_ref, o_ref, gtm_ref, shf_ref, scf_ref, gtf_ref, g2_ref, wo_ref, win_ref, wout_ref, out_ref):
    parts = _row_parts(x_ref, ROW_PARTS)
    mixed = [_dot(o_ref[rows, :], wo_ref[...]) for rows in parts]
    x1s = [x_ref[rows, :] + gtm_ref[...] * a for rows, a in zip(parts, mixed)]
    _ffn_tail(x1s, parts, shf_ref, scf_ref, gtf_ref, g2_ref, win_ref, wout_ref, out_ref)


def _post0(xin, o, mods, per_batch, g2, wo, win, wout, layer, tm):
    b, t, d = xin.shape
    return pl.pallas_call(
        _post0_kernel,
        grid=(b, t // tm),
        in_specs=[
            _tok_spec(tm, d), _tok_spec(tm, d),
            _mod_spec(2, per_batch), _mod_spec(3, per_batch), _mod_spec(4, per_batch), _mod_spec(5, per_batch),
            _const2((1, d)), _const2(wo.shape), _layer_spec(win, layer), _layer_spec(wout, layer),
        ],
        out_specs=_tok_spec(tm, d),
        out_shape=jax.ShapeDtypeStruct((b, t, d), F32),
        compiler_params=_params(2),
        name="attn_out_ffn",
    )(xin, o, mods, mods, mods, mods, g2, wo, win, wout)


def _post1_kernel(x_ref, y_ref, shm_ref, scm_ref, gtm_ref, shf_ref, scf_ref, gtf_ref, g1_ref, g2_ref, gn_ref,
                  wg_ref, wo_ref, win_ref, wout_ref, out_ref):
    parts = _row_parts(x_ref, ROW_PARTS)
    xs = [x_ref[rows, :] for rows in parts]
    h1s = [(_rms(x, g1_ref[...]) * (1.0 + scm_ref[...]) + shm_ref[...]).astype(BF16) for x in xs]
    gates = [_dot(h1, wg_ref[...]) for h1 in h1s]
    gated = []
    for rows, gate in zip(parts, gates):
        normed = []
        for h in range(RET_HEADS):
            o = y_ref[rows, h * RET_V_DIM:(h + 1) * RET_V_DIM].astype(F32)
            d = o - jnp.mean(o, axis=-1, keepdims=True)
            normed.append(d * lax.rsqrt(jnp.mean(d * d, axis=-1, keepdims=True) + EPS))
        gated.append((_silu(gate) * (jnp.concatenate(normed, axis=1) * gn_ref[...])).astype(BF16))
    mixed = [_dot(g, wo_ref[...]) for g in gated]
    x1s = [x + gtm_ref[...] * a for x, a in zip(xs, mixed)]
    _ffn_tail(x1s, parts, shf_ref, scf_ref, gtf_ref, g2_ref, win_ref, wout_ref, out_ref)


def _post1(xin, y, mods, g1, g2, gn, wg, wo, win, wout, layer, tm):
    b, t, d = xin.shape
    return pl.pallas_call(
        _post1_kernel,
        grid=(b, t // tm),
        in_specs=[
            _tok_spec(tm, d), _tok_spec(tm, y.shape[2]),
            *[_mod_spec(k, True) for k in range(6)],
            _const2((1, d)), _const2((1, d)), _const2(gn.shape),
            pl.BlockSpec((d, RET_VWIDTH), lambda b, j: (0, wg.shape[1] // RET_VWIDTH - 1), pipeline_mode=pl.Buffered(1)),
            _const2(wo.shape), _layer_spec(win, layer), _layer_spec(wout, layer),
        ],
        out_specs=_tok_spec(tm, d),
        out_shape=jax.ShapeDtypeStruct((b, t, d), F32),
        compiler_params=_params(2),
        name="ret_out_ffn",
    )(xin, y, mods, mods, mods, mods, mods, mods, g1, g2, gn, wg, wo, win, wout)


def _proj1_kernel(x_ref, sh_ref, sc_ref, g1_ref, w_ref, cos_ref, sin_ref, q_ref, k_ref, v_ref):
    qk_w = 2 * RET_HEADS * RET_QK_DIM
    hb = (_rms(x_ref[...], g1_ref[...]) * (1.0 + sc_ref[...]) + sh_ref[...]).astype(BF16)
    qk = _dot(hb, w_ref[:, :qk_w])
    k_scale = RET_QK_DIM ** -0.5
    for cb in range(qk_w // LANES):
        ts = slice((cb % 2) * LANES, (cb % 2 + 1) * LANES)
        tb = qk[:, cb * LANES:(cb + 1) * LANES]
        r = tb * cos_ref[:, ts] + pltpu.roll(tb, LANES // 2, 1) * sin_ref[:, ts]
        if cb < qk_w // (2 * LANES):
            q_ref[:, cb * LANES:(cb + 1) * LANES] = r.astype(BF16)
        else:
            cc = cb - qk_w // (2 * LANES)
            k_ref[:, cc * LANES:(cc + 1) * LANES] = (r * k_scale).astype(BF16)
    v_ref[...] = _dot(hb, w_ref[:, qk_w:]).astype(BF16)


def _proj1_ctx_kernel(x_ref, sh_ref, sc_ref, g1_ref, w_ref, k_ref, v_ref):
    qw = RET_HEADS * RET_QK_DIM
    hb = (_rms(x_ref[...], g1_ref[...]) * (1.0 + sc_ref[...]) + sh_ref[...]).astype(BF16)
    kv = _dot(hb, w_ref[:, qw:])
    k_ref[...] = (kv[:, :qw] * RET_QK_DIM ** -0.5).astype(BF16)
    v_ref[...] = kv[:, qw:].astype(BF16)


def _proj1(xin, mods, per_batch, g1, w, tables, tm):
    b, t, d = xin.shape
    qw = RET_HEADS * RET_QK_DIM
    specs = [
        _tok_spec(tm, d),
        _mod_spec(0, per_batch), _mod_spec(1, per_batch),
        _const2((1, d)),
        pl.BlockSpec((d, 2 * qw + RET_VWIDTH), lambda b, j: (0, 0), pipeline_mode=pl.Buffered(1)),
    ]
    outs = [(qw, _tok_spec(tm, qw)), (RET_VWIDTH, _tok_spec(tm, RET_VWIDTH))]
    if tables is not None:
        specs += [pl.BlockSpec((tm, 2 * LANES), lambda b, j: (j, 0))] * 2
        outs = [(qw, _tok_spec(tm, qw))] + outs
    return pl.pallas_call(
        _proj1_ctx_kernel if tables is None else _proj1_kernel,
        grid=(b, t // tm),
        in_specs=specs,
        out_specs=[spec for _, spec in outs],
        out_shape=[jax.ShapeDtypeStruct((b, t, width), BF16) for width, _ in outs],
        compiler_params=_params(2),
        name="ret_qkv_proj",
    )(xin, mods, mods, g1, w, *(tables or ()))


def _ret_kernel(dl_ref, q_ref, k_ref, v_ref, kc_ref, vc_ref, y_ref,
                o_scr, sf_scr, sb_scr, comb_scr, qdf_scr, qdb_scr, kdf_scr, kdb_scr, cdf_scr, cdb_scr,
                ctxf_scr, ctxb_scr, *, nc):
    hd = pl.program_id(0)
    c_len = RET_CHUNK
    l_ctx = kc_ref.shape[1]

    @pl.when(pl.program_id(1) == 0)
    def _decay_tables():
        def log_decay(direction, shp):
            logit = jnp.full(shp, dl_ref[direction, hd], F32)
            return jnp.minimum(logit, 0.0) - jnp.log1p(jnp.exp(-jnp.abs(logit)))

        shape = (c_len, c_len)
        n = lax.broadcasted_iota(jnp.int32, shape, 0).astype(F32)
        m = lax.broadcasted_iota(jnp.int32, shape, 1).astype(F32)
        comb_scr[...] = (jnp.where(n >= m, jnp.exp(log_decay(0, shape) * jnp.maximum(n - m, 0.0)), 0.0)
                         + jnp.where(m >= n, jnp.exp(log_decay(1, shape) * jnp.maximum(m - n, 0.0)), 0.0))
        nk = lax.broadcasted_iota(jnp.int32, (c_len, RET_QK_DIM), 0).astype(F32)
        nv = lax.broadcasted_iota(jnp.int32, (c_len, RET_V_DIM), 0).astype(F32)
        qdf_scr[...] = jnp.exp(log_decay(0, nv.shape) * (nv + 1.0))
        kdf_scr[...] = jnp.exp(log_decay(0, nk.shape) * (c_len - 1.0 - nk))
        cdf_scr[...] = jnp.exp(log_decay(0, (1, RET_V_DIM)) * float(c_len))
        qdb_scr[...] = jnp.exp(log_decay(1, nv.shape) * (c_len - nv))
        kdb_scr[...] = jnp.exp(log_decay(1, nk.shape) * nk)
        cdb_scr[...] = jnp.exp(log_decay(1, (1, RET_V_DIM)) * float(c_len))
        tt = lax.broadcasted_iota(jnp.int32, (l_ctx, RET_QK_DIM), 0).astype(F32)
        ctxf_scr[...] = jnp.exp(log_decay(0, tt.shape) * (l_ctx - 1.0 - tt))
        ctxb_scr[...] = jnp.exp(log_decay(1, tt.shape) * tt)

    seqs = range(q_ref.shape[0])

    for n in seqs:
        kc = kc_ref[n].astype(F32)
        sf_scr[n] = _dot_tn((kc * ctxf_scr[...]).astype(BF16), vc_ref[n])
        sb_scr[n] = _dot_tn((kc * ctxb_scr[...]).astype(BF16), vc_ref[n])

    def rows(c):
        return slice(c * c_len, (c + 1) * c_len)

    def masked_scores(n, c):
        return (_dot_nt(q_ref[n, rows(c), :], k_ref[n, rows(c), :]) * comb_scr[...]).astype(BF16)

    def forward(n, c, scores):
        qc, kx, vx = q_ref[n, rows(c), :], k_ref[n, rows(c), :], v_ref[n, rows(c), :]
        st = sf_scr[n]
        out = _dot(scores, vx) + _dot(qc, st.astype(BF16)) * qdf_scr[...]
        sf_scr[n] = st * cdf_scr[...] + _dot_tn((kx.astype(F32) * kdf_scr[...]).astype(BF16), vx)
        return out

    def backward(n, c):
        qc, kx, vx = q_ref[n, rows(c), :], k_ref[n, rows(c), :], v_ref[n, rows(c), :]
        st = sb_scr[n]
        out = _dot(qc, st.astype(BF16)) * qdb_scr[...]
        sb_scr[n] = st * cdb_scr[...] + _dot_tn((kx.astype(F32) * kdb_scr[...]).astype(BF16), vx)
        return out

    scores = {}
    for step in range(nc + RET_LOOKAHEAD):
        if step < nc:
            for n in seqs:
                scores[n, step] = masked_scores(n, step)
        i = step - RET_LOOKAHEAD
        if i < 0:
            continue
        for n in seqs:
            for c, out in ((i, forward(n, i, scores.pop((n, i)))), (nc - 1 - i, backward(n, nc - 1 - i))):
                if i < nc // 2:
                    o_scr[n, rows(c), :] = out
                else:
                    y_ref[n, rows(c), :] = (o_scr[n, rows(c), :] + out).astype(BF16)


def _retention(dl, q, k, v, kc, vc):
    b, s, _ = q.shape
    l = kc.shape[1]
    nc = s // RET_CHUNK
    assert nc % 2 == 0
    seqs = RET_SEQS if b % RET_SEQS == 0 else 1
    hspec = lambda t, w: pl.BlockSpec((seqs, t, w), lambda h, b: (b, 0, h))
    return pl.pallas_call(
        functools.partial(_ret_kernel, nc=nc),
        grid=(RET_HEADS, b // seqs),
        in_specs=[
            pl.BlockSpec(memory_space=pltpu.SMEM),
            hspec(s, RET_QK_DIM), hspec(s, RET_QK_DIM), hspec(s, RET_V_DIM),
            hspec(l, RET_QK_DIM), hspec(l, RET_V_DIM),
        ],
        out_specs=hspec(s, RET_V_DIM),
        out_shape=jax.ShapeDtypeStruct((b, s, RET_VWIDTH), BF16),
        scratch_shapes=[
            pltpu.VMEM((seqs, s, RET_V_DIM), F32),
            pltpu.VMEM((seqs, RET_QK_DIM, RET_V_DIM), F32),
            pltpu.VMEM((seqs, RET_QK_DIM, RET_V_DIM), F32),
            pltpu.VMEM((RET_CHUNK, RET_CHUNK), F32),
            pltpu.VMEM((RET_CHUNK, RET_V_DIM), F32),
            pltpu.VMEM((RET_CHUNK, RET_V_DIM), F32),
            pltpu.VMEM((RET_CHUNK, RET_QK_DIM), F32),
            pltpu.VMEM((RET_CHUNK, RET_QK_DIM), F32),
            pltpu.VMEM((1, RET_V_DIM), F32),
            pltpu.VMEM((1, RET_V_DIM), F32),
            pltpu.VMEM((l, RET_QK_DIM), F32),
            pltpu.VMEM((l, RET_QK_DIM), F32),
        ],
        compiler_params=_params(2),
        name="retention",
    )(dl, q, k, v, kc, vc)


def _rope_tables(n, head_dim):
    rows = n // GRID_W
    row = np.broadcast_to(np.arange(rows, dtype=np.float32)[:, None], (rows, GRID_W)).reshape(n)
    col = np.broadcast_to(np.arange(GRID_W, dtype=np.float32)[None, :], (rows, GRID_W)).reshape(n)
    axis_dim = head_dim // 2
    inv = (ROPE_BASE ** (-np.arange(0, axis_dim, 2, dtype=np.float32) / axis_dim)).astype(np.float32)
    ang_r = row[:, None] * inv
    ang_c = col[:, None] * inv
    cos = np.concatenate([np.cos(ang_r)] * 2 + [np.cos(ang_c)] * 2, axis=-1)
    sin = np.concatenate([-np.sin(ang_r), np.sin(ang_r), -np.sin(ang_c), np.sin(ang_c)], axis=-1)
    return cos.astype(np.float32), sin.astype(np.float32)


def kernel(x, c, ctx, c_ctx, ada_w, ada_b, norm1_g, norm2_g, ffn_w_in, ffn_w_out, attn_w_qkv, attn_q_norm,
           attn_k_norm, attn_sink, attn_w_o, ret_w_qkvg, ret_decay_logit, ret_gn_g, ret_w_o):
    b, s, d = x.shape
    l = ctx.shape[1]
    tm_x = min(FFN_TILE, s)
    tm_c = min(FFN_TILE, l)
    tp_x = min(PROJ_TILE, s)
    tp_c = min(PROJ_TILE, l)

    rows = -(-(b + 1) // SUBLANES) * SUBLANES
    cin = jnp.concatenate([c, c_ctx[None, :], jnp.zeros((rows - b - 1, d), F32)], axis=0)
    mods = _mods(cin, ada_w, ada_b)
    mx0 = mods[0, :b].reshape(b, 1, 6 * d)
    mc0 = mods[0, b:b + 1].reshape(1, 1, 6 * d)
    mx1 = mods[1, :b].reshape(b, 1, 6 * d)
    mc1 = mods[1, b:b + 1].reshape(1, 1, 6 * d)

    w_qkv = attn_w_qkv[0].astype(BF16)
    scale = HEAD_DIM ** -0.5 * LOG2E
    gain = jnp.concatenate([jnp.tile(attn_q_norm[0] * scale, N_HEADS), jnp.tile(attn_k_norm[0], N_KV_HEADS)])[None, :]
    lane_head = np.arange(MXU_TILE) // HEAD_DIM
    gsum = jnp.asarray((lane_head[:, None] == lane_head[None, :]) / HEAD_DIM, BF16)
    cos64, sin64 = _rope_tables(s, HEAD_DIM)
    cos_x = np.tile(cos64, (1, LANES // HEAD_DIM))
    sin_x = np.tile(sin64, (1, LANES // HEAD_DIM))
    cos_c = np.ones((l, LANES), np.float32)
    sin_c = np.zeros((l, LANES), np.float32)
    g1 = norm1_g[0][None, :]
    g2 = norm2_g[0][None, :]
    q_x, kk_x, vv_x, w_o, w_in, w_out, w_qkvg, w_ro = _qkv0(
        x, mx0, True, g1, w_qkv, gain, gsum, cos_x, sin_x, tp_x,
        side=(attn_w_o[0], ffn_w_in, ffn_w_out, ret_w_qkvg[0], ret_w_o[0]))
    q_c, kk_c, vv_c = _qkv0(ctx, mc0, False, g1, w_qkv, gain, gsum, cos_c, sin_c, tp_c)
    sink = attn_sink[0].astype(F32)
    o_x = _attn_x(sink, q_x, kk_x, vv_x, kk_c, vv_c)
    o_c = _attn_c(sink, q_c, kk_c, vv_c)
    x1 = _post0(x, o_x, mx0, True, g2, w_o, w_in, w_out, 0, tm_x)
    y_ctx = _post0(ctx, o_c, mc0, False, g2, w_o, w_in, w_out, 0, tm_c)

    cos256, sin256 = _rope_tables(s, RET_QK_DIM)
    g1 = norm1_g[1][None, :]
    g2 = norm2_g[1][None, :]
    q1, k1, v1 = _proj1(x1, mx1, True, g1, w_qkvg, (cos256, sin256), tp_x)
    k1c, v1c = _proj1(y_ctx, mc1, False, g1, w_qkvg, None, tp_c)
    y = _retention(ret_decay_logit[0].astype(F32), q1, k1, v1, k1c, v1c)
    return _post1(x1, y, mx1, g1, g2, ret_gn_g[0][None, :], w_qkvg, w_ro,
                  w_in, w_out, 1, tm_x)
```

```python
import functools

import jax
import jax.numpy as jnp
import numpy as np
from jax import lax
from jax.experimental import pallas as pl
from jax.experimental.pallas import tpu as pltpu

F32 = jnp.float32
BF16 = jnp.bfloat16

D_MODEL = 1024
GRID_W = 64
HEAD_DIM = 64
N_HEADS = D_MODEL // HEAD_DIM
N_KV_HEADS = N_HEADS // 4
WINDOW = 128
ATTN_BLOCK = 128
assert WINDOW == ATTN_BLOCK
RET_HEADS = 4
RET_QK_DIM = 256
RET_V_DIM = 512
RET_VWIDTH = 2 * D_MODEL
RET_CHUNK = 256
ATTN_PAIRS = N_HEADS // N_KV_HEADS // 2
ATTN_BLOCKS_PER_STEP = 4
ATTN_LOOKAHEAD = 1
RET_SEQS = 2
RET_LOOKAHEAD = 2
D_FF = 2816
MXU_TILE = 256
FF_CHUNKS = ((0, 6 * MXU_TILE), (6 * MXU_TILE, D_FF))
ROW_PARTS = 2
ROPE_BASE = 10000.0
EPS = 1e-6
NEG_INF = -1e30
LOG2E = 1.4426950408889634
LANES = 128
SUBLANES = 8
BF16_ROWS = 2 * SUBLANES
MODS_TILE = 6 * MXU_TILE

VMEM_LIMIT = 56 * 1024 * 1024
FFN_TILE = 512
PROJ_TILE = 1024


def _dot(a, b):
    return jnp.dot(a, b, preferred_element_type=F32)


def _dot_nt(a, b):
    return lax.dot_general(a, b, (((1,), (1,)), ((), ())), preferred_element_type=F32)


def _dot_tn(a, b):
    return lax.dot_general(a, b, (((0,), (0,)), ((), ())), preferred_element_type=F32)


def _rms(x, g):
    return x * lax.rsqrt(jnp.mean(x * x, axis=-1, keepdims=True) + EPS) * g


def _silu(x):
    return x * jax.nn.sigmoid(x)


def _params(n_axes):
    return pltpu.CompilerParams(dimension_semantics=("arbitrary",) * n_axes, vmem_limit_bytes=VMEM_LIMIT)


def _const2(shape):
    return pl.BlockSpec(shape, lambda b, j: (0,) * len(shape), pipeline_mode=pl.Buffered(1))


def _layer_spec(stacked, layer):
    return pl.BlockSpec((None,) + stacked.shape[1:], lambda b, j: (layer,) + (0,) * (stacked.ndim - 1),
                        pipeline_mode=pl.Buffered(1))


def _mod_spec(k, per_batch):
    if per_batch:
        return pl.BlockSpec((None, 1, D_MODEL), lambda b, j, k=k: (b, 0, k))
    return pl.BlockSpec((None, 1, D_MODEL), lambda b, j, k=k: (0, 0, k))


def _tok_spec(tm, width):
    return pl.BlockSpec((None, tm, width), lambda b, j: (b, j, 0))


def _row_parts(ref, n):
    tm = ref.shape[0]
    n = n if tm % (n * ATTN_BLOCK) == 0 else 1
    return [slice(i * tm // n, (i + 1) * tm // n) for i in range(n)]


def _mods_kernel(c_ref, w_ref, b_ref, o_ref):
    a = _silu(c_ref[...]).astype(BF16)
    o_ref[...] = _dot(a, w_ref[...].astype(BF16)) + b_ref[...]


def _mods(cin, ada_w, ada_b):
    depth, d, n = ada_w.shape
    rows = cin.shape[0]
    tn = MODS_TILE
    return pl.pallas_call(
        _mods_kernel,
        grid=(depth, n // tn),
        in_specs=[
            pl.BlockSpec((rows, d), lambda i, j: (0, 0)),
            pl.BlockSpec((None, d, tn), lambda i, j: (i, 0, j)),
            pl.BlockSpec((None, 1, tn), lambda i, j: (i, 0, j)),
        ],
        out_specs=pl.BlockSpec((None, rows, tn), lambda i, j: (i, 0, j)),
        out_shape=jax.ShapeDtypeStruct((depth, rows, n), F32),
        compiler_params=_params(2),
        name="adaln_mods",
    )(cin, ada_w, ada_b.reshape(depth, 1, n))


def _dup_halves(r):
    lane = lax.broadcasted_iota(jnp.int32, r.shape, 1)
    lo = lane < HEAD_DIM
    r64 = pltpu.roll(r, HEAD_DIM, 1)
    return jnp.where(lo, r, r64), jnp.where(lo, r64, r)


def _qkv0_kernel(x_ref, sh_ref, sc_ref, g1_ref, w_ref, gain_ref, gsum_ref, cos_ref, sin_ref, *rest):
    n_side = (len(rest) - 3) // 2
    q_ref, kk_ref, vt_ref = rest[n_side:n_side + 3]
    for src_ref, dst_ref in zip(rest[:n_side], rest[n_side + 3:]):
        dst_ref[...] = src_ref[...].astype(BF16)
    qd = N_HEADS * HEAD_DIM
    qkd = qd + N_KV_HEADS * HEAD_DIM
    parts = _row_parts(x_ref, ROW_PARTS)
    hs = [_rms(x_ref[rows, :], g1_ref[...]) * (1.0 + sc_ref[...]) + sh_ref[...] for rows in parts]
    qkvs = [_dot(h.astype(BF16), w_ref[...]) for h in hs]
    invfs = []
    for qkv in qkvs:
        sq = (qkv[:, :qkd] * qkv[:, :qkd]).astype(BF16)
        ms = jnp.concatenate([_dot(sq[:, c:c + MXU_TILE], gsum_ref[...]) for c in range(0, qkd, MXU_TILE)], axis=1)
        invfs.append(lax.rsqrt(ms + EPS))
    row = lax.broadcasted_iota(jnp.int32, (LANES - HEAD_DIM, ATTN_BLOCK), 0)
    tail = jnp.where(row == 0, 1.0, 0.0).astype(BF16)
    for rows, qkv, invf in zip(parts, qkvs, invfs):
        t = qkv[:, :qkd] * gain_ref[...]
        cos = cos_ref[rows, :]
        sin = sin_ref[rows, :]
        half = HEAD_DIM // 4
        lane = lax.broadcasted_iota(jnp.int32, cos.shape, 1)
        first = (lane % (2 * half)) < half
        for cb in range(qkd // LANES):
            sl = slice(cb * LANES, (cb + 1) * LANES)
            tb = t[:, sl]
            sw = jnp.where(first, pltpu.roll(tb, LANES - half, 1), pltpu.roll(tb, half, 1))
            r = (tb * cos + sw * sin) * invf[:, sl]
            if cb < qd // LANES:
                q_ref[rows, sl] = r.astype(BF16)
            else:
                p = cb - qd // LANES
                a, b = _dup_halves(r)
                kk_ref[rows, (2 * p) * LANES:(2 * p + 1) * LANES] = a.astype(BF16)
                kk_ref[rows, (2 * p + 1) * LANES:(2 * p + 2) * LANES] = b.astype(BF16)
        vt = qkv[:, qkd:].T
        for blk in range(vt.shape[1] // ATTN_BLOCK):
            cols = slice(blk * ATTN_BLOCK, (blk + 1) * ATTN_BLOCK)
            slab = rows.start // ATTN_BLOCK + blk
            for g in range(N_KV_HEADS):
                vt_ref[slab, g * LANES:g * LANES + HEAD_DIM, :] = vt[g * HEAD_DIM:(g + 1) * HEAD_DIM, cols].astype(BF16)
                vt_ref[slab, g * LANES + HEAD_DIM:(g + 1) * LANES, :] = tail


def _row_blocks(rows, steps):
    return next(n for n in range(min(steps, rows // BF16_ROWS), 0, -1)
                if rows % n == 0 and (rows // n) % BF16_ROWS == 0)


def _qkv0(xin, mods, per_batch, g1, w, gain, gsum, cos, sin, tm, side=()):
    b, t, d = xin.shape
    kw = N_KV_HEADS * LANES
    nj = t // tm
    flat = [a.reshape(-1, a.shape[-1]) for a in side]
    side_specs = []
    for a in flat:
        nblk = _row_blocks(a.shape[0], b * nj)
        side_specs.append(pl.BlockSpec((a.shape[0] // nblk, a.shape[1]),
                                       lambda b, j, nblk=nblk: (jnp.minimum(b * nj + j, nblk - 1), 0)))
    outs = pl.pallas_call(
        _qkv0_kernel,
        grid=(b, nj),
        in_specs=[
            _tok_spec(tm, d),
            _mod_spec(0, per_batch),
            _mod_spec(1, per_batch),
            _const2((1, d)),
            _const2(w.shape),
            _const2(gain.shape),
            _const2(gsum.shape),
            pl.BlockSpec((tm, LANES), lambda b, j: (j, 0)),
            pl.BlockSpec((tm, LANES), lambda b, j: (j, 0)),
            *side_specs,
        ],
        out_specs=[_tok_spec(tm, d), _tok_spec(tm, kw),
                   pl.BlockSpec((None, tm // ATTN_BLOCK, kw, ATTN_BLOCK), lambda b, j: (b, j, 0, 0)), *side_specs],
        out_shape=[
            jax.ShapeDtypeStruct((b, t, d), BF16),
            jax.ShapeDtypeStruct((b, t, kw), BF16),
            jax.ShapeDtypeStruct((b, t // ATTN_BLOCK, kw, ATTN_BLOCK), BF16),
            *[jax.ShapeDtypeStruct(a.shape, BF16) for a in flat],
        ],
        compiler_params=_params(2),
        name="attn_qkv_proj",
    )(xin, mods, mods, g1, w, gain, gsum, cos, sin, *flat)
    return list(outs[:3]) + [o.reshape(a.shape) for o, a in zip(outs[3:], side)]


def _band_masks(j, nb, heads):
    shape = (ATTN_BLOCK, heads * ATTN_BLOCK)
    key = lax.broadcasted_iota(jnp.int32, shape, 0)
    qry = lax.broadcasted_iota(jnp.int32, shape, 1) % ATTN_BLOCK
    return (key >= qry) & (j > 0), (key <= qry) & (j < nb - 1)


def _attn_scores(sink_ref, q_ref, key_refs, pr, pairs, masks):
    tq = q_ref.shape[0]
    g = pr // (N_HEADS // N_KV_HEADS // 2)
    gs = slice(g * LANES, (g + 1) * LANES)
    lo = lax.broadcasted_iota(jnp.int32, (tq, LANES), 1) < HEAD_DIM
    zero = jnp.zeros((tq, LANES), BF16)
    stack = []
    for p in range(pr, pr + pairs):
        qb = q_ref[:, p * LANES:(p + 1) * LANES]
        stack += [jnp.where(lo, qb, zero), jnp.where(lo, zero, qb)]
    keys = jnp.concatenate([r[:, gs] for r in key_refs], axis=0) if len(key_refs) > 1 else key_refs[0][:, gs]
    st = _dot_nt(keys, jnp.concatenate(stack, axis=0)).astype(BF16)
    parts = [st[i * LANES:(i + 1) * LANES, :] for i in range(st.shape[0] // LANES)]
    if masks is not None:
        ia = key_refs[0].shape[0] // LANES
        neg = jnp.full(parts[ia].shape, NEG_INF, BF16)
        parts[ia] = jnp.where(masks[0], parts[ia], neg)
        parts[ia + 2] = jnp.where(masks[1], parts[ia + 2], neg)
    sink = jnp.concatenate([jnp.full((1, tq), sink_ref[2 * pr + i] * LOG2E, F32) for i in range(2 * pairs)], axis=1)
    mx = parts[0]
    for part in parts[1:]:
        mx = jnp.maximum(mx, part)
    m = jnp.maximum(jnp.max(mx, axis=0, keepdims=True).astype(F32), sink).astype(BF16)
    return jnp.concatenate([jnp.exp2(part - m) for part in parts], axis=0), jnp.exp2(sink - m.astype(F32))


def _attn_values(probs, val_refs, o_ref, pr, pairs):
    pt, p_sink = probs
    tq = o_ref.shape[0]
    g = pr // (N_HEADS // N_KV_HEADS // 2)
    gs = slice(g * LANES, (g + 1) * LANES)
    vt = jnp.concatenate([r[i, gs, :] for r in val_refs for i in range(r.shape[0])], axis=1)
    ot = lax.dot_general(pt, vt, (((0,), (1,)), ((), ())), preferred_element_type=F32).T
    den = jnp.sum(ot[HEAD_DIM:HEAD_DIM + SUBLANES, :], axis=0, keepdims=True) + p_sink
    out = ot[:HEAD_DIM, :] / den
    for i in range(pairs):
        pair = jnp.concatenate([out[:, 2 * i * tq:(2 * i + 1) * tq], out[:, (2 * i + 1) * tq:(2 * i + 2) * tq]], axis=0)
        o_ref[:, (pr + i) * LANES:(pr + i + 1) * LANES] = pair.T.astype(BF16)


def _attn_pipeline(sink_ref, tasks, pairs):
    items = [(task, g) for task in tasks for g in range(0, N_HEADS // 2, pairs)]
    probs = {}
    for step in range(len(items) + ATTN_LOOKAHEAD):
        if step < len(items):
            (q_ref, key_refs, _, _, masks), g = items[step]
            probs[step] = _attn_scores(sink_ref, q_ref, key_refs, g, pairs, masks)
        done = step - ATTN_LOOKAHEAD
        if done >= 0:
            (_, _, val_refs, o_ref, _), g = items[done]
            _attn_values(probs.pop(done), val_refs, o_ref, g, pairs)


def _attn_ctx_kernel(sink_ref, q_ref, kc_ref, vc_ref, o_ref, *, pairs):
    _attn_pipeline(sink_ref, [(q_ref, [kc_ref], [vc_ref], o_ref, None)], pairs)


def _attn_band_kernel(sink_ref, q_ref, kc_ref, k_ref, vc_ref, vt_ref, o_ref, *, nb, pairs):
    tasks = []
    for t in range(q_ref.shape[0] // ATTN_BLOCK):
        j = pl.program_id(1) * (q_ref.shape[0] // ATTN_BLOCK) + t
        rows = pl.ds(t * ATTN_BLOCK, ATTN_BLOCK)
        band = [jnp.maximum(j - 1, 0), j, jnp.minimum(j + 1, nb - 1)]
        key_refs = [kc_ref] + [k_ref.at[pl.ds(pl.multiple_of(blk * ATTN_BLOCK, ATTN_BLOCK), ATTN_BLOCK), :]
                               for blk in band]
        val_refs = [vc_ref] + [vt_ref.at[pl.ds(blk, 1)] for blk in band]
        tasks.append((q_ref.at[rows, :], key_refs, val_refs, o_ref.at[rows, :], _band_masks(j, nb, 2 * pairs)))
    _attn_pipeline(sink_ref, tasks, pairs)


def _attn_x(sink, q, kk, vv, kkc, vvc):
    b, s, d = q.shape
    nb = s // ATTN_BLOCK
    rows = ATTN_BLOCK * max(n for n in range(1, ATTN_BLOCKS_PER_STEP + 1) if nb % n == 0)
    per_batch = lambda a: pl.BlockSpec((None,) + a.shape[1:], lambda b, j: (b,) + (0,) * (a.ndim - 1))
    blk = pl.BlockSpec((None, rows, d), lambda b, j: (b, j, 0))
    return pl.pallas_call(
        functools.partial(_attn_band_kernel, nb=nb, pairs=ATTN_PAIRS),
        grid=(b, s // rows),
        in_specs=[pl.BlockSpec(memory_space=pltpu.SMEM), blk,
                  per_batch(kkc), per_batch(kk), per_batch(vvc), per_batch(vv)],
        out_specs=blk,
        out_shape=jax.ShapeDtypeStruct((b, s, d), BF16),
        compiler_params=_params(2),
        name="attn_latent",
    )(sink, q, kkc, kk, vvc, vv)


def _attn_c(sink, qc, kkc, vvc):
    b, l, d = qc.shape
    return pl.pallas_call(
        functools.partial(_attn_ctx_kernel, pairs=ATTN_PAIRS),
        grid=(b,),
        in_specs=[
            pl.BlockSpec(memory_space=pltpu.SMEM),
            pl.BlockSpec((None, l, d), lambda b: (b, 0, 0)),
            pl.BlockSpec((None, l, kkc.shape[2]), lambda b: (b, 0, 0)),
            pl.BlockSpec((None,) + vvc.shape[1:], lambda b: (b, 0, 0, 0)),
        ],
        out_specs=pl.BlockSpec((None, l, d), lambda b: (b, 0, 0)),
        out_shape=jax.ShapeDtypeStruct((b, l, d), BF16),
        compiler_params=_params(1),
        name="attn_context",
    )(sink, qc, kkc, vvc)


def _ffn_tail(x1s, parts, shf_ref, scf_ref, gtf_ref, g2_ref, win_ref, wout_ref, out_ref):
    h2s = [(_rms(x1, g2_ref[...]) * (1.0 + scf_ref[...]) + shf_ref[...]).astype(BF16) for x1 in x1s]
    accs = [None] * len(x1s)
    hidden = {}
    for step in range(len(FF_CHUNKS) + 1):
        if step < len(FF_CHUNKS):
            lo, hi = FF_CHUNKS[step]
            hidden[step] = [(_dot(h2, win_ref[:, lo:hi]), _dot(h2, win_ref[:, D_FF + lo:D_FF + hi])) for h2 in h2s]
        if step > 0:
            lo, hi = FF_CHUNKS[step - 1]
            for i, (gate, up) in enumerate(hidden.pop(step - 1)):
                down = _dot((_silu(gate) * up).astype(BF16), wout_ref[lo:hi, :])
                accs[i] = down if accs[i] is None else accs[i] + down
    for rows, x1, acc in zip(parts, x1s, accs):
        out_ref[rows, :] = x1 + gtf_ref[...] * acc


def _post0_kernel(x_ref, o_ref, gtm_ref, shf_ref, scf_ref, gtf_ref, g2_ref, wo_ref, win_ref, wout_ref, out_ref):
    parts = _row_parts(x_ref, ROW_PARTS)
    mixed = [_dot(o_ref[rows, :], wo_ref[...]) for rows in parts]
    x1s = [x_ref[rows, :] + gtm_ref[...] * a for rows, a in zip(parts, mixed)]
    _ffn_tail(x1s, parts, shf_ref, scf_ref, gtf_ref, g2_ref, win_ref, wout_ref, out_ref)


def _post0(xin, o, mods, per_batch, g2, wo, win, wout, layer, tm):
    b, t, d = xin.shape
    return pl.pallas_call(
        _post0_kernel,
        grid=(b, t // tm),
        in_specs=[
            _tok_spec(tm, d), _tok_spec(tm, d),
            _mod_spec(2, per_batch), _mod_spec(3, per_batch), _mod_spec(4, per_batch), _mod_spec(5, per_batch),
            _const2((1, d)), _const2(wo.shape), _layer_spec(win, layer), _layer_spec(wout, layer),
        ],
        out_specs=_tok_spec(tm, d),
        out_shape=jax.ShapeDtypeStruct((b, t, d), F32),
        compiler_params=_params(2),
        name="attn_out_ffn",
    )(xin, o, mods, mods, mods, mods, g2, wo, win, wout)


def _post1_kernel(x_ref, y_ref, shm_ref, scm_ref, gtm_ref, shf_ref, scf_ref, gtf_ref, g1_ref, g2_ref, gn_ref,
                  wg_ref, wo_ref, win_ref, wout_ref, out_ref):
    parts = _row_parts(x_ref, ROW_PARTS)
    xs = [x_ref[rows, :] for rows in parts]
    h1s = [(_rms(x, g1_ref[...]) * (1.0 + scm_ref[...]) + shm_ref[...]).astype(BF16) for x in xs]
    gates = [_dot(h1, wg_ref[...]) for h1 in h1s]
    gated = []
    for rows, gate in zip(parts, gates):
        normed = []
        for h in range(RET_HEADS):
            o = y_ref[rows, h * RET_V_DIM:(h + 1) * RET_V_DIM].astype(F32)
            d = o - jnp.mean(o, axis=-1, keepdims=True)
            normed.append(d * lax.rsqrt(jnp.mean(d * d, axis=-1, keepdims=True) + EPS))
        gated.append((_silu(gate) * (jnp.concatenate(normed, axis=1) * gn_ref[...])).astype(BF16))
    mixed = [_dot(g, wo_ref[...]) for g in gated]
    x1s = [x + gtm_ref[...] * a for x, a in zip(xs, mixed)]
    _ffn_tail(x1s, parts, shf_ref, scf_ref, gtf_ref, g2_ref, win_ref, wout_ref, out_ref)


def _post1(xin, y, mods, g1, g2, gn, wg, wo, win, wout, layer, tm):
    b, t, d = xin.shape
    return pl.pallas_call(
        _post1_kernel,
        grid=(b, t // tm),
        in_specs=[
            _tok_spec(tm, d), _tok_spec(tm, y.shape[2]),
            *[_mod_spec(k, True) for k in range(6)],
            _const2((1, d)), _const2((1, d)), _const2(gn.shape),
            pl.BlockSpec((d, RET_VWIDTH), lambda b, j: (0, wg.shape[1] // RET_VWIDTH - 1), pipeline_mode=pl.Buffered(1)),
            _const2(wo.shape), _layer_spec(win, layer), _layer_spec(wout, layer),
        ],
        out_specs=_tok_spec(tm, d),
        out_shape=jax.ShapeDtypeStruct((b, t, d), F32),
        compiler_params=_params(2),
        name="ret_out_ffn",
    )(xin, y, mods, mods, mods, mods, mods, mods, g1, g2, gn, wg, wo, win, wout)


def _proj1_kernel(x_ref, sh_ref, sc_ref, g1_ref, w_ref, cos_ref, sin_ref, q_ref, k_ref, v_ref):
    qk_w = 2 * RET_HEADS * RET_QK_DIM
    hb = (_rms(x_ref[...], g1_ref[...]) * (1.0 + sc_ref[...]) + sh_ref[...]).astype(BF16)
    qk = _dot(hb, w_ref[:, :qk_w])
    k_scale = RET_QK_DIM ** -0.5
    for cb in range(qk_w // LANES):
        ts = slice((cb % 2) * LANES, (cb % 2 + 1) * LANES)
        tb = qk[:, cb * LANES:(cb + 1) * LANES]
        r = tb * cos_ref[:, ts] + pltpu.roll(tb, LANES // 2, 1) * sin_ref[:, ts]
        if cb < qk_w // (2 * LANES):
            q_ref[:, cb * LANES:(cb + 1) * LANES] = r.astype(BF16)
        else:
            cc = cb - qk_w // (2 * LANES)
            k_ref[:, cc * LANES:(cc + 1) * LANES] = (r * k_scale).astype(BF16)
    v_ref[...] = _dot(hb, w_ref[:, qk_w:]).astype(BF16)


def _proj1_ctx_kernel(x_ref, sh_ref, sc_ref, g1_ref, w_ref, k_ref, v_ref):
    qw = RET_HEADS * RET_QK_DIM
    hb = (_rms(x_ref[...], g1_ref[...]) * (1.0 + sc_ref[...]) + sh_ref[...]).astype(BF16)
    kv = _dot(hb, w_ref[:, qw:])
    k_ref[...] = (kv[:, :qw] * RET_QK_DIM ** -0.5).astype(BF16)
    v_ref[...] = kv[:, qw:].astype(BF16)


def _proj1(xin, mods, per_batch, g1, w, tables, tm):
    b, t, d = xin.shape
    qw = RET_HEADS * RET_QK_DIM
    specs = [
        _tok_spec(tm, d),
        _mod_spec(0, per_batch), _mod_spec(1, per_batch),
        _const2((1, d)),
        pl.BlockSpec((d, 2 * qw + RET_VWIDTH), lambda b, j: (0, 0), pipeline_mode=pl.Buffered(1)),
    ]
    outs = [(qw, _tok_spec(tm, qw)), (RET_VWIDTH, _tok_spec(tm, RET_VWIDTH))]
    if tables is not None:
        specs += [pl.BlockSpec((tm, 2 * LANES), lambda b, j: (j, 0))] * 2
        outs = [(qw, _tok_spec(tm, qw))] + outs
    return pl.pallas_call(
        _proj1_ctx_kernel if tables is None else _proj1_kernel,
        grid=(b, t // tm),
        in_specs=specs,
        out_specs=[spec for _, spec in outs],
        out_shape=[jax.ShapeDtypeStruct((b, t, width), BF16) for width, _ in outs],
        compiler_params=_params(2),
        name="ret_qkv_proj",
    )(xin, mods, mods, g1, w, *(tables or ()))


def _ret_kernel(dl_ref, q_ref, k_ref, v_ref, kc_ref, vc_ref, y_ref,
                o_scr, sf_scr, sb_scr, comb_scr, qdf_scr, qdb_scr, kdf_scr, kdb_scr, cdf_scr, cdb_scr,
                ctxf_scr, ctxb_scr, *, nc):
    hd = pl.program_id(0)
    c_len = RET_CHUNK
    l_ctx = kc_ref.shape[1]

    @pl.when(pl.program_id(1) == 0)
    def _decay_tables():
        def log_decay(direction, shp):
            logit = jnp.full(shp, dl_ref[direction, hd], F32)
            return jnp.minimum(logit, 0.0) - jnp.log1p(jnp.exp(-jnp.abs(logit)))

        shape = (c_len, c_len)
        n = lax.broadcasted_iota(jnp.int32, shape, 0).astype(F32)
        m = lax.broadcasted_iota(jnp.int32, shape, 1).astype(F32)
        comb_scr[...] = (jnp.where(n >= m, jnp.exp(log_decay(0, shape) * jnp.maximum(n - m, 0.0)), 0.0)
                         + jnp.where(m >= n, jnp.exp(log_decay(1, shape) * jnp.maximum(m - n, 0.0)), 0.0))
        nk = lax.broadcasted_iota(jnp.int32, (c_len, RET_QK_DIM), 0).astype(F32)
        nv = lax.broadcasted_iota(jnp.int32, (c_len, RET_V_DIM), 0).astype(F32)
        qdf_scr[...] = jnp.exp(log_decay(0, nv.shape) * (nv + 1.0))
        kdf_scr[...] = jnp.exp(log_decay(0, nk.shape) * (c_len - 1.0 - nk))
        cdf_scr[...] = jnp.exp(log_decay(0, (1, RET_V_DIM)) * float(c_len))
        qdb_scr[...] = jnp.exp(log_decay(1, nv.shape) * (c_len - nv))
        kdb_scr[...] = jnp.exp(log_decay(1, nk.shape) * nk)
        cdb_scr[...] = jnp.exp(log_decay(1, (1, RET_V_DIM)) * float(c_len))
        tt = lax.broadcasted_iota(jnp.int32, (l_ctx, RET_QK_DIM), 0).astype(F32)
        ctxf_scr[...] = jnp.exp(log_decay(0, tt.shape) * (l_ctx - 1.0 - tt))
        ctxb_scr[...] = jnp.exp(log_decay(1, tt.shape) * tt)

    seqs = range(q_ref.shape[0])

    for n in seqs:
        kc = kc_ref[n].astype(F32)
        sf_scr[n] = _dot_tn((kc * ctxf_scr[...]).astype(BF16), vc_ref[n])
        sb_scr[n] = _dot_tn((kc * ctxb_scr[...]).astype(BF16), vc_ref[n])

    def rows(c):
        return slice(c * c_len, (c + 1) * c_len)

    def masked_scores(n, c):
        return (_dot_nt(q_ref[n, rows(c), :], k_ref[n, rows(c), :]) * comb_scr[...]).astype(BF16)

    def forward(n, c, scores):
        qc, kx, vx = q_ref[n, rows(c), :], k_ref[n, rows(c), :], v_ref[n, rows(c), :]
        st = sf_scr[n]
        out = _dot(scores, vx) + _dot(qc, st.astype(BF16)) * qdf_scr[...]
        sf_scr[n] = st * cdf_scr[...] + _dot_tn((kx.astype(F32) * kdf_scr[...]).astype(BF16), vx)
        return out

    def backward(n, c):
        qc, kx, vx = q_ref[n, rows(c), :], k_ref[n, rows(c), :], v_ref[n, rows(c), :]
        st = sb_scr[n]
        out = _dot(qc, st.astype(BF16)) * qdb_scr[...]
        sb_scr[n] = st * cdb_scr[...] + _dot_tn((kx.astype(F32) * kdb_scr[...]).astype(BF16), vx)
        return out

    scores = {}
    for step in range(nc + RET_LOOKAHEAD):
        if step < nc:
            for n in seqs:
                scores[n, step] = masked_scores(n, step)
        i = step - RET_LOOKAHEAD
        if i < 0:
            continue
        for n in seqs:
            for c, out in ((i, forward(n, i, scores.pop((n, i)))), (nc - 1 - i, backward(n, nc - 1 - i))):
                if i < nc // 2:
                    o_scr[n, rows(c), :] = out
                else:
                    y_ref[n, rows(c), :] = (o_scr[n, rows(c), :] + out).astype(BF16)


def _retention(dl, q, k, v, kc, vc):
    b, s, _ = q.shape
    l = kc.shape[1]
    nc = s // RET_CHUNK
    assert nc % 2 == 0
    seqs = RET_SEQS if b % RET_SEQS == 0 else 1
    hspec = lambda t, w: pl.BlockSpec((seqs, t, w), lambda h, b: (b, 0, h))
    return pl.pallas_call(
        functools.partial(_ret_kernel, nc=nc),
        grid=(RET_HEADS, b // seqs),
        in_specs=[
            pl.BlockSpec(memory_space=pltpu.SMEM),
            hspec(s, RET_QK_DIM), hspec(s, RET_QK_DIM), hspec(s, RET_V_DIM),
            hspec(l, RET_QK_DIM), hspec(l, RET_V_DIM),
        ],
        out_specs=hspec(s, RET_V_DIM),
        out_shape=jax.ShapeDtypeStruct((b, s, RET_VWIDTH), BF16),
        scratch_shapes=[
            pltpu.VMEM((seqs, s, RET_V_DIM), F32),
            pltpu.VMEM((seqs, RET_QK_DIM, RET_V_DIM), F32),
            pltpu.VMEM((seqs, RET_QK_DIM, RET_V_DIM), F32),
            pltpu.VMEM((RET_CHUNK, RET_CHUNK), F32),
            pltpu.VMEM((RET_CHUNK, RET_V_DIM), F32),
            pltpu.VMEM((RET_CHUNK, RET_V_DIM), F32),
            pltpu.VMEM((RET_CHUNK, RET_QK_DIM), F32),
            pltpu.VMEM((RET_CHUNK, RET_QK_DIM), F32),
            pltpu.VMEM((1, RET_V_DIM), F32),
            pltpu.VMEM((1, RET_V_DIM), F32),
            pltpu.VMEM((l, RET_QK_DIM), F32),
            pltpu.VMEM((l, RET_QK_DIM), F32),
        ],
        compiler_params=_params(2),
        name="retention",
    )(dl, q, k, v, kc, vc)


def _rope_tables(n, head_dim):
    rows = n // GRID_W
    row = np.broadcast_to(np.arange(rows, dtype=np.float32)[:, None], (rows, GRID_W)).reshape(n)
    col = np.broadcast_to(np.arange(GRID_W, dtype=np.float32)[None, :], (rows, GRID_W)).reshape(n)
    axis_dim = head_dim // 2
    inv = (ROPE_BASE ** (-np.arange(0, axis_dim, 2, dtype=np.float32) / axis_dim)).astype(np.float32)
    ang_r = row[:, None] * inv
    ang_c = col[:, None] * inv
    cos = np.concatenate([np.cos(ang_r)] * 2 + [np.cos(ang_c)] * 2, axis=-1)
    sin = np.concatenate([-np.sin(ang_r), np.sin(ang_r), -np.sin(ang_c), np.sin(ang_c)], axis=-1)
    return cos.astype(np.float32), sin.astype(np.float32)


def kernel(x, c, ctx, c_ctx, ada_w, ada_b, norm1_g, norm2_g, ffn_w_in, ffn_w_out, attn_w_qkv, attn_q_norm,
           attn_k_norm, attn_sink, attn_w_o, ret_w_qkvg, ret_decay_logit, ret_gn_g, ret_w_o):
    b, s, d = x.shape
    l = ctx.shape[1]
    tm_x = min(FFN_TILE, s)
    tm_c = min(FFN_TILE, l)
    tp_x = min(PROJ_TILE, s)
    tp_c = min(PROJ_TILE, l)

    rows = -(-(b + 1) // SUBLANES) * SUBLANES
    cin = jnp.concatenate([c, c_ctx[None, :], jnp.zeros((rows - b - 1, d), F32)], axis=0)
    mods = _mods(cin, ada_w, ada_b)
    mx0 = mods[0, :b].reshape(b, 1, 6 * d)
    mc0 = mods[0, b:b + 1].reshape(1, 1, 6 * d)
    mx1 = mods[1, :b].reshape(b, 1, 6 * d)
    mc1 = mods[1, b:b + 1].reshape(1, 1, 6 * d)

    w_qkv = attn_w_qkv[0].astype(BF16)
    scale = HEAD_DIM ** -0.5 * LOG2E
    gain = jnp.concatenate([jnp.tile(attn_q_norm[0] * scale, N_HEADS), jnp.tile(attn_k_norm[0], N_KV_HEADS)])[None, :]
    lane_head = np.arange(MXU_TILE) // HEAD_DIM
    gsum = jnp.asarray((lane_head[:, None] == lane_head[None, :]) / HEAD_DIM, BF16)
    cos64, sin64 = _rope_tables(s, HEAD_DIM)
    cos_x = np.tile(cos64, (1, LANES // HEAD_DIM))
    sin_x = np.tile(sin64, (1, LANES // HEAD_DIM))
    cos_c = np.ones((l, LANES), np.float32)
    sin_c = np.zeros((l, LANES), np.float32)
    g1 = norm1_g[0][None, :]
    g2 = norm2_g[0][None, :]
    q_x, kk_x, vv_x, w_o, w_in, w_out, w_qkvg, w_ro = _qkv0(
        x, mx0, True, g1, w_qkv, gain, gsum, cos_x, sin_x, tp_x,
        side=(attn_w_o[0], ffn_w_in, ffn_w_out, ret_w_qkvg[0], ret_w_o[0]))
    q_c, kk_c, vv_c = _qkv0(ctx, mc0, False, g1, w_qkv, gain, gsum, cos_c, sin_c, tp_c)
    sink = attn_sink[0].astype(F32)
    o_x = _attn_x(sink, q_x, kk_x, vv_x, kk_c, vv_c)
    o_c = _attn_c(sink, q_c, kk_c, vv_c)
    x1 = _post0(x, o_x, mx0, True, g2, w_o, w_in, w_out, 0, tm_x)
    y_ctx = _post0(ctx, o_c, mc0, False, g2, w_o, w_in, w_out, 0, tm_c)

    cos256, sin256 = _rope_tables(s, RET_QK_DIM)
    g1 = norm1_g[1][None, :]
    g2 = norm2_g[1][None, :]
    q1, k1, v1 = _proj1(x1, mx1, True, g1, w_qkvg, (cos256, sin256), tp_x)
    k1c, v1c = _proj1(y_ctx, mc1, False, g1, w_qkvg, None, tp_c)
    y = _retention(ret_decay_logit[0].astype(F32), q1, k1, v1, k1c, v1c)
    return _post1(x1, y, mx1, g1, g2, ret_gn_g[0][None, :], w_qkvg, w_ro,
                  w_in, w_out, 1, tm_x)
```
